```python
import math
import jax, jax.numpy as jnp
from jax import lax
import numpy as np

D_MODEL = 1024
BATCH = 8
SEQ = 2048
DEPTH = 4
DEC_BATCH = 32
DEC_SEQ = 4
PAST_LEN = 8192
PAGE_SIZE = 128

N_GDN_LAYERS = (DEPTH + 1) // 2
N_NSA_LAYERS = DEPTH // 2
GDN_HEADS = 8
GDN_DK = D_MODEL // GDN_HEADS
GDN_DV = GDN_DK
GDN_WIDTH = GDN_HEADS * GDN_DK
GDN_CONV = 4
GDN_CHUNK = 64
NSA_HEADS = 16
NSA_KV_HEADS = 4
NSA_GROUP = NSA_HEADS // NSA_KV_HEADS
HEAD_DIM = D_MODEL // NSA_HEADS
Q_WIDTH = NSA_HEADS * HEAD_DIM
KV_WIDTH = NSA_KV_HEADS * HEAD_DIM
NSA_BLOCK = 64
N_SEL = 16
WINDOW = 512
Q_BLOCK = 128
ROPE_THETA = 10000.0
ATTN_SCALE = HEAD_DIM ** -0.5
D_FF = 2816
FFN_CONV = 3
RMS_EPS = 1e-6
NEG = -1e30

kernel_name = 'gdn_nsa_convffn_hybrid_step'

f32 = jnp.float32


def rmsnorm(x, w):
    xf = x.astype(f32)
    y = xf * lax.rsqrt(jnp.mean(xf * xf, axis=-1, keepdims=True) + RMS_EPS)
    return (y * w.astype(f32)).astype(x.dtype)


def l2norm(x):
    xf = x.astype(f32)
    return xf * lax.rsqrt(jnp.sum(xf * xf, axis=-1, keepdims=True) + RMS_EPS)


def causal_conv(ext, w, L):
    out = ext[:, 0:L] * w[0]
    for j in range(1, w.shape[0]):
        out = out + ext[:, j:j + L] * w[j]
    return out


def rope(x, pos):
    half = HEAD_DIM // 2
    inv = ROPE_THETA ** (-jnp.arange(half, dtype=f32) / half)
    ang = pos.astype(f32)[:, None] * inv[None, :]
    cos, sin = jnp.cos(ang)[:, None, :], jnp.sin(ang)[:, None, :]
    xf = x.astype(f32)
    x1, x2 = xf[..., :half], xf[..., half:]
    return jnp.concatenate([x1 * cos - x2 * sin, x2 * cos + x1 * sin], axis=-1).astype(x.dtype)


def gated_delta_chunked(q, k, v, g, beta, S0):
    B, L, H, _ = q.shape
    DVh = v.shape[-1]
    C = min(GDN_CHUNK, L)
    N = -(-L // C)
    pad = N * C - L

    def prep(t):
        t = t.astype(f32)
        if pad:
            t = jnp.pad(t, [(0, 0), (0, pad)] + [(0, 0)] * (t.ndim - 2))
        t = t.reshape((B, N, C) + t.shape[2:])
        return jnp.moveaxis(t, 3, 1)

    q, k, v, g, beta = prep(q), prep(k), prep(v), prep(g), prep(beta)
    G = jnp.cumsum(g, axis=-1)
    ii = jnp.arange(C)
    tril = ii[:, None] >= ii[None, :]
    strict = ii[:, None] > ii[None, :]
    decay = jnp.exp(jnp.where(tril, G[..., :, None] - G[..., None, :], NEG))
    kk = jnp.einsum('bhncd,bhnsd->bhncs', k, k)
    A = jnp.where(strict, beta[..., :, None] * kk * decay, 0.0)
    eye = jnp.eye(C, dtype=f32)
    T = lax.linalg.triangular_solve(eye + A, jnp.broadcast_to(eye, A.shape), left_side=True, lower=True, unit_diagonal=True)
    u_v = T @ (v * beta[..., None])
    w_k = T @ (k * (beta * jnp.exp(G))[..., None])
    qk = jnp.einsum('bhncd,bhnsd->bhncs', q, k) * decay
    gl = G[..., -1]
    k_dec = k * jnp.exp(gl[..., None] - G)[..., None]

    def step(S, xs):
        qc, qkc, uvc, wkc, kdc, Gc, glc = xs
        u = uvc - wkc @ S
        o = (qc * jnp.exp(Gc)[..., None]) @ S + qkc @ u
        S = S * jnp.exp(glc)[..., None, None] + jnp.einsum('bhcd,bhce->bhde', kdc, u)
        return S, o

    xs = (jnp.moveaxis(q, 2, 0), jnp.moveaxis(qk, 2, 0), jnp.moveaxis(u_v, 2, 0), jnp.moveaxis(w_k, 2, 0),
          jnp.moveaxis(k_dec, 2, 0), jnp.moveaxis(G, 2, 0), jnp.moveaxis(gl, 2, 0))
    S, o = lax.scan(step, S0.astype(f32), xs)
    o = jnp.moveaxis(jnp.moveaxis(o, 0, 2), 1, 3).reshape(B, N * C, H, DVh)[:, :L]
    return o, S


def gdn_mixer(h, S0, conv_buf, w_in, conv_w, a_log, dt_bias, norm_w, w_out):
    B, L, _ = h.shape
    proj = h @ w_in
    qkv = proj[..., :3 * GDN_WIDTH]
    z = proj[..., 3 * GDN_WIDTH:4 * GDN_WIDTH].reshape(B, L, GDN_HEADS, GDN_DV)
    a = proj[..., 4 * GDN_WIDTH:4 * GDN_WIDTH + GDN_HEADS].astype(f32)
    b = proj[..., 4 * GDN_WIDTH + GDN_HEADS:].astype(f32)
    ext = jnp.concatenate([conv_buf.astype(qkv.dtype), qkv], axis=1)
    new_buf = ext[:, -(GDN_CONV - 1):]
    c = jax.nn.silu(causal_conv(ext, conv_w, L))
    q = l2norm(c[..., :GDN_WIDTH].reshape(B, L, GDN_HEADS, GDN_DK)) * (GDN_DK ** -0.5)
    k = l2norm(c[..., GDN_WIDTH:2 * GDN_WIDTH].reshape(B, L, GDN_HEADS, GDN_DK))
    v = c[..., 2 * GDN_WIDTH:].reshape(B, L, GDN_HEADS, GDN_DV)
    beta = jax.nn.sigmoid(b)
    g = -jnp.exp(a_log.astype(f32)) * jax.nn.softplus(a + dt_bias.astype(f32))
    o, S = gated_delta_chunked(q, k, v, g, beta, S0)
    o = rmsnorm(o, norm_w) * jax.nn.silu(z.astype(f32))
    return o.astype(h.dtype).reshape(B, L, GDN_WIDTH) @ w_out, S.astype(S0.dtype), new_buf


def nsa_project(h, w_in, pos):
    B, L, _ = h.shape
    proj = h @ w_in
    q = proj[..., :Q_WIDTH].reshape(B, L, NSA_HEADS, HEAD_DIM)
    kv = proj[..., Q_WIDTH:Q_WIDTH + 6 * KV_WIDTH].reshape(B, L, 3, 2, NSA_KV_HEADS, HEAD_DIM)
    gates = jax.nn.sigmoid(proj[..., Q_WIDTH + 6 * KV_WIDTH:].astype(f32)).reshape(B, L, NSA_KV_HEADS, NSA_GROUP, 3)
    q_raw = q.reshape(B, L, NSA_KV_HEADS, NSA_GROUP, HEAD_DIM)
    q_rot = rope(q, pos).reshape(B, L, NSA_KV_HEADS, NSA_GROUP, HEAD_DIM)
    kv_cmp = kv[:, :, 0]
    kv_sel = jnp.stack([rope(kv[:, :, 1, 0], pos), kv[:, :, 1, 1]], axis=2)
    kv_win = jnp.stack([rope(kv[:, :, 2, 0], pos), kv[:, :, 2, 1]], axis=2)
    return q_raw, q_rot, kv_cmp, kv_sel, kv_win, gates


def compress(rows, pe, w):
    B, T = rows.shape[:2]
    nb = T // NSA_BLOCK
    blk = rows.reshape(B, nb, NSA_BLOCK, 2, NSA_KV_HEADS, HEAD_DIM) + pe[:, :, None, :]
    return jnp.einsum('bnlcgd,lcde->bncge', blk, w)


def cmp_attend_select(q, pos, ckv, n_top):
    NB = ckv.shape[1]
    s = jnp.einsum('bqgnd,bkgd->bqgnk', q, ckv[:, :, 0]).astype(f32) * ATTN_SCALE
    blk_end = (jnp.arange(NB, dtype=jnp.int32) + 1) * NSA_BLOCK - 1
    m = (blk_end[None, :] <= pos[:, None])[None, :, None, None, :]
    p = jax.nn.softmax(jnp.where(m, s, NEG), axis=-1) * m
    o = jnp.einsum('bqgnk,bkgd->bqgnd', p, ckv[:, :, 1].astype(f32))
    imp = jnp.sum(p, axis=3)
    cur = pos // NSA_BLOCK
    cand = (jnp.arange(NB, dtype=jnp.int32)[None, :] < cur[:, None])[None, :, None, :]
    vals, top = lax.top_k(jnp.where(cand, imp, -1.0), n_top)
    cur_b = jnp.broadcast_to(cur[None, :, None, None], top.shape[:-1] + (1,)).astype(jnp.int32)
    idx = jnp.concatenate([top.astype(jnp.int32), cur_b], axis=-1)
    valid = jnp.concatenate([vals > -0.5, jnp.ones(cur_b.shape, dtype=bool)], axis=-1)
    return o, idx, valid


def sel_attend(q, pos, idx, valid, k, v):
    B, Lq, nk, ng, _ = q.shape
    KS = idx.shape[-1]
    s = jnp.einsum('bqgnd,bqgkld->bqgnkl', q, k).astype(f32) * ATTN_SCALE
    kpos = idx[..., None] * NSA_BLOCK + jnp.arange(NSA_BLOCK, dtype=jnp.int32)
    m = (kpos <= pos[None, :, None, None, None]) & valid[..., None]
    s = jnp.where(m[:, :, :, None], s, NEG).reshape(B, Lq, nk, ng, KS * NSA_BLOCK)
    p = jax.nn.softmax(s, axis=-1).reshape(B, Lq, nk, ng, KS, NSA_BLOCK)
    return jnp.einsum('bqgnkl,bqgkld->bqgnd', p, v.astype(f32))


def window_attend(q, pos, k, v, kpos):
    s = jnp.einsum('bqgnd,bkgd->bqgnk', q, k).astype(f32) * ATTN_SCALE
    dist = pos[:, None] - kpos[None, :]
    m = (dist >= 0) & (dist < WINDOW) & (kpos[None, :] >= 0)
    p = jax.nn.softmax(jnp.where(m[None, :, None, None, :], s, NEG), axis=-1)
    return jnp.einsum('bqgnk,bkgd->bqgnd', p, v.astype(f32))


def nsa_prompt_attend(q_raw, q_rot, kv_cmp, kv_sel, kv_win, pos, cmp_pe, cmp_w):
    B, L = q_raw.shape[:2]
    ckv = compress(kv_cmp, cmp_pe, cmp_w)
    o_c, idx, valid = cmp_attend_select(q_raw, pos, ckv, min(N_SEL - 1, ckv.shape[1]))
    qb = min(Q_BLOCK, L)
    nq = L // qb
    kvb = kv_sel.reshape(B, L // NSA_BLOCK, NSA_BLOCK, 2, NSA_KV_HEADS, HEAD_DIM)
    kw_pad = jnp.pad(kv_win, ((0, 0), (WINDOW, 0), (0, 0), (0, 0), (0, 0)))
    bi = jnp.arange(B)[:, None, None, None]
    gi = jnp.arange(NSA_KV_HEADS)[None, None, :, None]

    def blocks(t):
        return jnp.swapaxes(t.reshape((B, nq, qb) + t.shape[2:]), 0, 1)

    def body(xs):
        jb, qblk, iblk, vblk = xs
        pblk = jb * qb + jnp.arange(qb, dtype=jnp.int32)
        gth = kvb[bi, iblk, :, :, gi]
        o_s = sel_attend(qblk, pblk, iblk, vblk, gth[..., 0, :], gth[..., 1, :])
        span = lax.dynamic_slice_in_dim(kw_pad, jb * qb, WINDOW + qb, axis=1)
        kpos = jb * qb - WINDOW + jnp.arange(WINDOW + qb, dtype=jnp.int32)
        o_w = window_attend(qblk, pblk, span[:, :, 0], span[:, :, 1], kpos)
        return o_s, o_w

    o_s, o_w = lax.map(body, (jnp.arange(nq, dtype=jnp.int32), blocks(q_rot), blocks(idx), blocks(valid)))
    o_s = jnp.swapaxes(o_s, 0, 1).reshape((B, L) + o_s.shape[3:])
    o_w = jnp.swapaxes(o_w, 0, 1).reshape((B, L) + o_w.shape[3:])
    return o_c, o_s, o_w


def nsa_sample_attend(q_raw, q_rot, kv_cmp, kv_sel, kv_win, pos, pool_cmp, pool_sel, win_buf, page_table, cmp_pe, cmp_w):
    DB, DS = q_raw.shape[:2]
    past = page_table.shape[1] * PAGE_SIZE
    past_cmp = pool_cmp[page_table].reshape(DB, past, 2, NSA_KV_HEADS, HEAD_DIM)
    n_new_full = (DS // NSA_BLOCK) * NSA_BLOCK
    ckv = jnp.concatenate([compress(past_cmp, cmp_pe, cmp_w),
                           compress(kv_cmp[:, :n_new_full], cmp_pe, cmp_w).astype(past_cmp.dtype)], axis=1)
    o_c, idx, valid = cmp_attend_select(q_raw, pos, ckv, min(N_SEL - 1, ckv.shape[1]))
    ppb = PAGE_SIZE // NSA_BLOCK
    npb = past // NSA_BLOCK
    pool_b = pool_sel.reshape(-1, NSA_BLOCK, 2, NSA_KV_HEADS, HEAD_DIM)
    si = jnp.arange(DB)[:, None, None, None]
    gi = jnp.arange(NSA_KV_HEADS)[None, None, :, None]
    b_past = jnp.minimum(idx, npb - 1)
    phys = page_table[si, b_past // ppb] * ppb + b_past % ppb
    past_blk = pool_b[phys, :, :, gi]
    nbn = -(-DS // NSA_BLOCK)
    new_b = jnp.pad(kv_sel, ((0, 0), (0, nbn * NSA_BLOCK - DS), (0, 0), (0, 0), (0, 0)))
    new_b = new_b.reshape(DB, nbn, NSA_BLOCK, 2, NSA_KV_HEADS, HEAD_DIM)
    new_blk = new_b[si, jnp.clip(idx - npb, 0, nbn - 1), :, :, gi]
    blk = jnp.where((idx >= npb)[..., None, None, None], new_blk.astype(past_blk.dtype), past_blk)
    o_s = sel_attend(q_rot, pos, idx, valid, blk[..., 0, :], blk[..., 1, :])
    wbuf = win_buf.shape[1]
    wk = jnp.concatenate([win_buf.astype(kv_win.dtype), kv_win], axis=1)
    kpos = past - wbuf + jnp.arange(wbuf + DS, dtype=jnp.int32)
    o_w = window_attend(q_rot, pos, wk[:, :, 0], wk[:, :, 1], kpos)
    return o_c, o_s, o_w, wk[:, -wbuf:]


def nsa_merge(o_c, o_s, o_w, gates, w_out, dtype):
    o = gates[..., 0:1] * o_c + gates[..., 1:2] * o_s + gates[..., 2:3] * o_w
    B, L = o.shape[:2]
    return o.astype(dtype).reshape(B, L, Q_WIDTH) @ w_out


def conv_ffn(h, buf, w_up, conv_w, conv_b, w_down):
    L = h.shape[1]
    u = h @ w_up
    ext = jnp.concatenate([buf.astype(u.dtype), u], axis=1)
    c = causal_conv(ext, conv_w, L) + conv_b
    out = (jax.nn.silu(c[..., :D_FF]) * c[..., D_FF:]) @ w_down
    return out, ext[:, -(FFN_CONV - 1):]


def setup_inputs(seed: int = 0) -> dict:
    key = jax.random.key(seed)
    ks = iter(jax.random.split(key, 40))

    def nrm(shape, scale):
        return jax.random.normal(next(ks), shape, f32) * scale

    n_pages = PAST_LEN // PAGE_SIZE
    n_used = DEC_BATCH * n_pages
    n_phys = n_used + (n_used + 3) // 4
    wbuf = min(WINDOW, PAST_LEN)
    gdn_in_w = 4 * GDN_WIDTH + 2 * GDN_HEADS
    nsa_in_w = Q_WIDTH + 6 * KV_WIDTH + 3 * NSA_HEADS

    x_prompt = nrm((BATCH, SEQ, D_MODEL), 1.0)
    x_sample = nrm((DEC_BATCH, DEC_SEQ, D_MODEL), 1.0)
    state_gdn = nrm((N_GDN_LAYERS, DEC_BATCH, GDN_HEADS, GDN_DK, GDN_DV), 0.1)
    state_gdn_conv = nrm((N_GDN_LAYERS, DEC_BATCH, GDN_CONV - 1, 3 * GDN_WIDTH), 1.0)
    cache_cmp = nrm((N_NSA_LAYERS, n_phys, PAGE_SIZE, 2, NSA_KV_HEADS, HEAD_DIM), 1.0)
    cache_sel = nrm((N_NSA_LAYERS, n_phys, PAGE_SIZE, 2, NSA_KV_HEADS, HEAD_DIM), 1.0)
    state_win = nrm((N_NSA_LAYERS, DEC_BATCH, wbuf, 2, NSA_KV_HEADS, HEAD_DIM), 1.0)
    state_ffn_conv = nrm((DEPTH, DEC_BATCH, FFN_CONV - 1, 2 * D_FF), 1.0)
    page_table = jax.random.permutation(next(ks), n_phys)[:n_used].reshape(DEC_BATCH, n_pages).astype(jnp.int32)

    norm_mix = 1.0 + nrm((DEPTH, D_MODEL), 0.01)
    norm_ffn = 1.0 + nrm((DEPTH, D_MODEL), 0.01)
    norm_final = 1.0 + nrm((D_MODEL,), 0.01)
    gdn_w_in = nrm((N_GDN_LAYERS, D_MODEL, gdn_in_w), D_MODEL ** -0.5)
    gdn_conv_w = nrm((N_GDN_LAYERS, GDN_CONV, 3 * GDN_WIDTH), GDN_CONV ** -0.5)
    gdn_a_log = jnp.log(jax.random.uniform(next(ks), (N_GDN_LAYERS, GDN_HEADS), f32, 1.0, 16.0))
    dt = jnp.exp(jax.random.uniform(next(ks), (N_GDN_LAYERS, GDN_HEADS), f32, math.log(1e-3), math.log(1e-1)))
    gdn_dt_bias = dt + jnp.log(-jnp.expm1(-dt))
    gdn_norm_w = 1.0 + nrm((N_GDN_LAYERS, GDN_DV), 0.01)
    gdn_w_out = nrm((N_GDN_LAYERS, GDN_WIDTH, D_MODEL), GDN_WIDTH ** -0.5)
    nsa_w_in = nrm((N_NSA_LAYERS, D_MODEL, nsa_in_w), D_MODEL ** -0.5)
    nsa_cmp_pe = nrm((N_NSA_LAYERS, NSA_BLOCK, 2, HEAD_DIM), 0.1)
    nsa_cmp_w = nrm((N_NSA_LAYERS, NSA_BLOCK, 2, HEAD_DIM, HEAD_DIM), (NSA_BLOCK * HEAD_DIM) ** -0.5)
    nsa_w_out = nrm((N_NSA_LAYERS, Q_WIDTH, D_MODEL), Q_WIDTH ** -0.5)
    ffn_w_up = nrm((DEPTH, D_MODEL, 2 * D_FF), D_MODEL ** -0.5)
    ffn_conv_w = nrm((DEPTH, FFN_CONV, 2 * D_FF), FFN_CONV ** -0.5)
    ffn_conv_b = nrm((DEPTH, 2 * D_FF), 0.01)
    ffn_w_down = nrm((DEPTH, D_FF, D_MODEL), D_FF ** -0.5)
    return {'x_prompt': x_prompt, 'x_sample': x_sample, 'state_gdn': state_gdn, 'state_gdn_conv': state_gdn_conv,
            'cache_cmp': cache_cmp, 'cache_sel': cache_sel, 'state_win': state_win, 'state_ffn_conv': state_ffn_conv,
            'page_table': page_table, 'norm_mix': norm_mix, 'norm_ffn': norm_ffn, 'norm_final': norm_final,
            'gdn_w_in': gdn_w_in, 'gdn_conv_w': gdn_conv_w, 'gdn_a_log': gdn_a_log, 'gdn_dt_bias': gdn_dt_bias,
            'gdn_norm_w': gdn_norm_w, 'gdn_w_out': gdn_w_out, 'nsa_w_in': nsa_w_in, 'nsa_cmp_pe': nsa_cmp_pe,
            'nsa_cmp_w': nsa_cmp_w, 'nsa_w_out': nsa_w_out, 'ffn_w_up': ffn_w_up, 'ffn_conv_w': ffn_conv_w,
            'ffn_conv_b': ffn_conv_b, 'ffn_w_down': ffn_w_down}


def reference(x_prompt, x_sample, state_gdn, state_gdn_conv, cache_cmp, cache_sel, state_win, state_ffn_conv,
              page_table, norm_mix, norm_ffn, norm_final, gdn_w_in, gdn_conv_w, gdn_a_log, gdn_dt_bias, gdn_norm_w,
              gdn_w_out, nsa_w_in, nsa_cmp_pe, nsa_cmp_w, nsa_w_out, ffn_w_up, ffn_conv_w, ffn_conv_b, ffn_w_down):
    B, L, _ = x_prompt.shape
    DB, DS, _ = x_sample.shape
    past = page_table.shape[1] * PAGE_SIZE
    pos_p = jnp.arange(L, dtype=jnp.int32)
    pos_s = past + jnp.arange(DS, dtype=jnp.int32)
    dt = x_prompt.dtype
    yp, ys = x_prompt, x_sample
    gdn_p, gdn_s, gconv_p, gconv_s = [], [], [], []
    cmp_p, cmp_s, sel_p, sel_s, win_p, win_s = [], [], [], [], [], []
    ffn_p, ffn_s = [], []
    for i in range(DEPTH):
        j = i // 2
        hp = rmsnorm(yp, norm_mix[i])
        hs = rmsnorm(ys, norm_mix[i])
        if i % 2 == 0:
            S0p = jnp.zeros((B, GDN_HEADS, GDN_DK, GDN_DV), dt)
            c0p = jnp.zeros((B, GDN_CONV - 1, 3 * GDN_WIDTH), dt)
            op, Sp, cp = gdn_mixer(hp, S0p, c0p, gdn_w_in[j], gdn_conv_w[j], gdn_a_log[j], gdn_dt_bias[j],
                                   gdn_norm_w[j], gdn_w_out[j])
            os_, Ss, cs = gdn_mixer(hs, state_gdn[j], state_gdn_conv[j], gdn_w_in[j], gdn_conv_w[j], gdn_a_log[j],
                                    gdn_dt_bias[j], gdn_norm_w[j], gdn_w_out[j])
            gdn_p.append(Sp)
            gdn_s.append(Ss)
            gconv_p.append(cp)
            gconv_s.append(cs)
        else:
            qr, qo, kc, ksl, kw, gt = nsa_project(hp, nsa_w_in[j], pos_p)
            o_c, o_s, o_w = nsa_prompt_attend(qr, qo, kc, ksl, kw, pos_p, nsa_cmp_pe[j], nsa_cmp_w[j])
            op = nsa_merge(o_c, o_s, o_w, gt, nsa_w_out[j], dt)
            cmp_p.append(kc)
            sel_p.append(ksl)
            win_p.append(kw[:, -min(WINDOW, L):])
            qr2, qo2, kc2, ksl2, kw2, gt2 = nsa_project(hs, nsa_w_in[j], pos_s)
            o_c2, o_s2, o_w2, nwin = nsa_sample_attend(qr2, qo2, kc2, ksl2, kw2, pos_s, cache_cmp[j], cache_sel[j],
                                                       state_win[j], page_table, nsa_cmp_pe[j], nsa_cmp_w[j])
            os_ = nsa_merge(o_c2, o_s2, o_w2, gt2, nsa_w_out[j], dt)
            cmp_s.append(kc2)
            sel_s.append(ksl2)
            win_s.append(nwin)
        yp = yp + op
        ys = ys + os_
        fp, bp = conv_ffn(rmsnorm(yp, norm_ffn[i]), jnp.zeros((B, FFN_CONV - 1, 2 * D_FF), dt),
                          ffn_w_up[i], ffn_conv_w[i], ffn_conv_b[i], ffn_w_down[i])
        fs, bs = conv_ffn(rmsnorm(ys, norm_ffn[i]), state_ffn_conv[i],
                          ffn_w_up[i], ffn_conv_w[i], ffn_conv_b[i], ffn_w_down[i])
        yp = yp + fp
        ys = ys + fs
        ffn_p.append(bp)
        ffn_s.append(bs)
    y_prompt = rmsnorm(yp, norm_final)
    y_sample = rmsnorm(ys, norm_final)
    new_gdn_p = jnp.stack(gdn_p)
    new_gdn_s = jnp.stack(gdn_s)
    new_gdn_conv_p = jnp.stack(gconv_p)
    new_gdn_conv_s = jnp.stack(gconv_s)
    new_cmp_p = jnp.stack(cmp_p)
    new_cmp_s = jnp.stack(cmp_s)
    new_sel_p = jnp.stack(sel_p)
    new_sel_s = jnp.stack(sel_s)
    new_win_p = jnp.stack(win_p)
    new_win_s = jnp.stack(win_s)
    new_ffn_conv_p = jnp.stack(ffn_p)
    new_ffn_conv_s = jnp.stack(ffn_s)
    return (y_prompt, y_sample, new_gdn_p, new_gdn_s, new_gdn_conv_p, new_gdn_conv_s, new_cmp_p, new_cmp_s,
            new_sel_p, new_sel_s, new_win_p, new_win_s, new_ffn_conv_p, new_ffn_conv_s)
```

```python
import functools
import math

import jax
import jax.numpy as jnp
from jax import lax
from jax.experimental import pallas as pl
from jax.experimental.pallas import tpu as pltpu

f32 = jnp.float32
bf16 = jnp.bfloat16

D_MODEL = 1024
GDN_HEADS = 8
GDN_DK = 128
GDN_WIDTH = GDN_HEADS * GDN_DK
GDN_CONV = 4
GDN_CHUNK = 64
NSA_HEADS = 16
NSA_KV_HEADS = 4
NSA_GROUP = 4
HEAD_DIM = 64
Q_WIDTH = NSA_HEADS * HEAD_DIM
KV_WIDTH = NSA_KV_HEADS * HEAD_DIM
NSA_BLOCK = 64
N_SEL = 16
WINDOW = 512
PAGE_SIZE = 128
ROPE_THETA = 10000.0
ATTN_SCALE = HEAD_DIM ** -0.5
D_FF = 2816
FFN_CONV = 3
RMS_EPS = 1e-6
NEG = -1e30

V7X_LANES = 128
V7X_SUBLANES = 8
V7X_VMEM_BYTES = 64 * 1024 * 1024
VMEM_LIMIT = 56 * 1024 * 1024


def _cparams(*sem):
    return pltpu.CompilerParams(dimension_semantics=sem, vmem_limit_bytes=VMEM_LIMIT)


def _round_up(x, m):
    return (x + m - 1) // m * m


def _pick_tile(n, cap):
    best = None
    for t in range(V7X_SUBLANES, min(n, cap) + 1, V7X_SUBLANES):
        if n % t == 0:
            best = t
    assert best is not None, (n, cap)
    return best


def _dot(a, b):
    return jnp.dot(a.astype(bf16), b.astype(bf16), preferred_element_type=f32)


def _dot_nt(a, b):
    return lax.dot_general(a.astype(bf16), b.astype(bf16), (((1,), (1,)), ((), ())), preferred_element_type=f32)


def _dot_tn(a, b):
    return lax.dot_general(a.astype(bf16), b.astype(bf16), (((0,), (0,)), ((), ())), preferred_element_type=f32)


def _split3(x):
    hi = x.astype(bf16)
    r = x - hi.astype(f32)
    mid = r.astype(bf16)
    lo = (r - mid.astype(f32)).astype(bf16)
    return hi, mid, lo


def _dot_exact_lhs(a01, x):
    a = a01.astype(bf16)
    hi, mid, lo = _split3(x)
    return (jnp.dot(a, hi, preferred_element_type=f32) + jnp.dot(a, mid, preferred_element_type=f32)
            + jnp.dot(a, lo, preferred_element_type=f32))


def _silu(x):
    return x / (1.0 + jnp.exp(-x))


def _sigmoid(x):
    return 1.0 / (1.0 + jnp.exp(-x))


def _rms_rows(x, w):
    return x * lax.rsqrt(jnp.mean(x * x, axis=-1, keepdims=True) + RMS_EPS) * w


def _ffn_kernel(*refs, tm, shift, hist_rows, has_hist, tiles_per_seq, n_chunk):
    if has_hist:
        y_ref, hist_ref, nw_ref, wup_ref, cw_ref, cb_ref, wdn_ref, out_ref, st_ref, ext_ref, g_ref = refs
    else:
        y_ref, nw_ref, wup_ref, cw_ref, cb_ref, wdn_ref, out_ref, st_ref, ext_ref, g_ref = refs
        hist_ref = None
    H = hist_rows
    s = shift
    t = pl.program_id(0) % tiles_per_seq

    @pl.when(t == 0)
    def _():
        if has_hist:
            ext_ref[H - 2 * s:H, :] = hist_ref[...]
        else:
            ext_ref[0:H, :] = jnp.zeros((H, 2 * D_FF), f32)

    @pl.when(t != 0)
    def _():
        ext_ref[H - 2 * s:H, :] = ext_ref[H + tm - 2 * s:H + tm, :]

    y = y_ref[...]
    h = _rms_rows(y, nw_ref[...]).astype(bf16)
    for c in range(2 * D_FF // n_chunk):
        ext_ref[H:H + tm, c * n_chunk:(c + 1) * n_chunk] = jnp.dot(
            h, wup_ref[:, c * n_chunk:(c + 1) * n_chunk], preferred_element_type=f32)
    st_ref[...] = ext_ref[H + tm - 2 * s:H + tm, :]

    gc = 256
    for j in range(D_FF // gc):
        def conv(col):
            w = cw_ref[:, col:col + gc]
            return (ext_ref[H - 2 * s:H - 2 * s + tm, col:col + gc] * w[0:1, :]
                    + ext_ref[H - s:H - s + tm, col:col + gc] * w[1:2, :]
                    + ext_ref[H:H + tm, col:col + gc] * w[2:3, :]
                    + cb_ref[:, col:col + gc])
        a = conv(j * gc)
        b = conv(D_FF + j * gc)
        g_ref[:, j * gc:(j + 1) * gc] = (_silu(a) * b).astype(bf16)
    out_ref[...] = y + jnp.dot(g_ref[...], wdn_ref[...], preferred_element_type=f32)


def _ffn(y, hist, nw, wup, cw, cb, wdn, *, tm, shift, rows_per_seq):
    T = y.shape[0]
    n_seq = T // rows_per_seq
    tiles_per_seq = rows_per_seq // tm
    H = _round_up(2 * shift, V7X_SUBLANES)
    kern = functools.partial(_ffn_kernel, tm=tm, shift=shift, hist_rows=H, has_hist=hist is not None,
                             tiles_per_seq=tiles_per_seq, n_chunk=512)
    const = lambda i: (0, 0)
    in_specs = [pl.BlockSpec((tm, D_MODEL), lambda i: (i, 0))]
    args = [y]
    if hist is not None:
        in_specs.append(pl.BlockSpec((2 * shift, 2 * D_FF), lambda i: (i // tiles_per_seq, 0)))
        args.append(hist)
    in_specs += [pl.BlockSpec((1, D_MODEL), const),
                 pl.BlockSpec((D_MODEL, 2 * D_FF), const, pipeline_mode=pl.Buffered(1)),
                 pl.BlockSpec((FFN_CONV, 2 * D_FF), const),
                 pl.BlockSpec((1, 2 * D_FF), const),
                 pl.BlockSpec((D_FF, D_MODEL), const, pipeline_mode=pl.Buffered(1))]
    args += [nw, wup, cw, cb, wdn]
    out, st = pl.pallas_call(
        kern,
        grid=(T // tm,),
        in_specs=in_specs,
        out_specs=[pl.BlockSpec((tm, D_MODEL), lambda i: (i, 0)),
                   pl.BlockSpec((None, 2 * shift, 2 * D_FF), lambda i: (i // tiles_per_seq, 0, 0))],
        out_shape=[jax.ShapeDtypeStruct((T, D_MODEL), f32),
                   jax.ShapeDtypeStruct((n_seq, 2 * shift, 2 * D_FF), f32)],
        scratch_shapes=[pltpu.VMEM((H + tm, 2 * D_FF), f32), pltpu.VMEM((tm, D_FF), bf16)],
        compiler_params=_cparams("arbitrary"),
        name="conv_ffn",
    )(*args)
    return out, st


def _gdn_in_kernel(*refs, tm, shift, hist_rows, has_hist, tiles_per_seq):
    if has_hist:
        (y_ref, hist_ref, nw_ref, wqkv_ref, wz_ref, wab_ref, cw_ref, alog_ref, dtb_ref,
         q_ref, k_ref, v_ref, z_ref, slab_ref, st_ref, ext_ref) = refs
    else:
        (y_ref, nw_ref, wqkv_ref, wz_ref, wab_ref, cw_ref, alog_ref, dtb_ref,
         q_ref, k_ref, v_ref, z_ref, slab_ref, st_ref, ext_ref) = refs
        hist_ref = None
    H = hist_rows
    s = shift
    nh = GDN_CONV - 1
    W3 = 3 * GDN_WIDTH
    t = pl.program_id(0) % tiles_per_seq

    @pl.when(t == 0)
    def _():
        if has_hist:
            ext_ref[H - nh * s:H, :] = hist_ref[...]
        else:
            ext_ref[0:H, :] = jnp.zeros((H, W3), f32)

    @pl.when(t != 0)
    def _():
        ext_ref[H - nh * s:H, :] = ext_ref[H + tm - nh * s:H + tm, :]

    h = _rms_rows(y_ref[...], nw_ref[...]).astype(bf16)
    nc = 512
    for c in range(W3 // nc):
        ext_ref[H:H + tm, c * nc:(c + 1) * nc] = jnp.dot(h, wqkv_ref[:, c * nc:(c + 1) * nc], preferred_element_type=f32)
    for c in range(GDN_WIDTH // nc):
        z_ref[:, c * nc:(c + 1) * nc] = jnp.dot(h, wz_ref[:, c * nc:(c + 1) * nc], preferred_element_type=f32)
    ab = jnp.dot(h, wab_ref[...], preferred_element_type=f32)
    x = ab + dtb_ref[...]
    softplus = jnp.maximum(x, 0.0) + jnp.log(1.0 + jnp.exp(-jnp.abs(x)))
    g = -jnp.exp(alog_ref[...]) * softplus
    lane = lax.broadcasted_iota(jnp.int32, ab.shape, 1)
    slab_ref[...] = jnp.where(lane < GDN_HEADS, g, _sigmoid(ab))
    st_ref[...] = ext_ref[H + tm - nh * s:H + tm, :]

    hw = GDN_DK
    for j in range(W3 // hw):
        col = j * hw
        w = cw_ref[:, col:col + hw]
        c = ext_ref[H:H + tm, col:col + hw] * w[nh:nh + 1, :]
        for i in range(nh):
            c = c + ext_ref[H - (nh - i) * s:H - (nh - i) * s + tm, col:col + hw] * w[i:i + 1, :]
        c = _silu(c)
        if j < 2 * GDN_HEADS:
            c = c * lax.rsqrt(jnp.sum(c * c, axis=-1, keepdims=True) + RMS_EPS)
        if j < GDN_HEADS:
            q_ref[:, col:col + hw] = c * (GDN_DK ** -0.5)
        elif j < 2 * GDN_HEADS:
            k_ref[:, col - GDN_WIDTH:col - GDN_WIDTH + hw] = c
        else:
            v_ref[:, col - 2 * GDN_WIDTH:col - 2 * GDN_WIDTH + hw] = c


def _gdn_in(y, hist, nw, wqkv, wz, wab, cw, alog, dtb, *, tm, shift, rows_per_seq):
    T = y.shape[0]
    n_seq = T // rows_per_seq
    tiles_per_seq = rows_per_seq // tm
    nh = GDN_CONV - 1
    H = _round_up(nh * shift, V7X_SUBLANES)
    W3 = 3 * GDN_WIDTH
    kern = functools.partial(_gdn_in_kernel, tm=tm, shift=shift, hist_rows=H, has_hist=hist is not None,
                             tiles_per_seq=tiles_per_seq)
    const = lambda i: (0, 0)
    row = lambda i: (i, 0)
    in_specs = [pl.BlockSpec((tm, D_MODEL), row)]
    args = [y]
    if hist is not None:
        in_specs.append(pl.BlockSpec((nh * shift, W3), lambda i: (i // tiles_per_seq, 0)))
        args.append(hist)
    in_specs += [pl.BlockSpec((1, D_MODEL), const),
                 pl.BlockSpec((D_MODEL, W3), const, pipeline_mode=pl.Buffered(1)),
                 pl.BlockSpec((D_MODEL, GDN_WIDTH), const, pipeline_mode=pl.Buffered(1)),
                 pl.BlockSpec((D_MODEL, V7X_LANES), const),
                 pl.BlockSpec((GDN_CONV, W3), const),
                 pl.BlockSpec((1, V7X_LANES), const),
                 pl.BlockSpec((1, V7X_LANES), const)]
    args += [nw, wqkv, wz, wab, cw, alog, dtb]
    wide = pl.BlockSpec((tm, GDN_WIDTH), row)
    return pl.pallas_call(
        kern,
        grid=(T // tm,),
        in_specs=in_specs,
        out_specs=[wide, wide, wide, wide, pl.BlockSpec((tm, V7X_LANES), row),
                   pl.BlockSpec((None, nh * shift, W3), lambda i: (i // tiles_per_seq, 0, 0))],
        out_shape=[jax.ShapeDtypeStruct((T, GDN_WIDTH), f32)] * 4
        + [jax.ShapeDtypeStruct((T, V7X_LANES), f32), jax.ShapeDtypeStruct((n_seq, nh * shift, W3), f32)],
        scratch_shapes=[pltpu.VMEM((H + tm, W3), f32)],
        compiler_params=_cparams("arbitrary"),
        name="gdn_in",
    )(*args)


def _gdn_chunk_kernel(q_ref, k_ref, v_ref, slab_ref, s0_ref, o_ref, s_ref):
    C = GDN_CHUNK
    n = pl.program_id(1)

    @pl.when(n == 0)
    def _():
        s_ref[...] = s0_ref[...]

    ri = lax.broadcasted_iota(jnp.int32, (C, C), 0)
    ci = lax.broadcasted_iota(jnp.int32, (C, C), 1)
    tril = ri >= ci
    strict = ri > ci
    eye = (ri == ci).astype(f32)
    slab = slab_ref[...]
    G = _dot_exact_lhs(tril.astype(f32), slab)
    GT = G.T
    for h in range(GDN_HEADS):
        lo = h * GDN_DK
        qh = q_ref[:, lo:lo + GDN_DK]
        kh = k_ref[:, lo:lo + GDN_DK]
        vh = v_ref[:, lo:lo + GDN_DK]
        Gc = G[:, h:h + 1]
        Gr = GT[h:h + 1, :]
        bc = slab[:, GDN_HEADS + h:GDN_HEADS + h + 1]
        gl = G[C - 1:C, h:h + 1]
        decay = jnp.exp(jnp.where(tril, Gc - Gr, NEG))
        kb = kh.astype(bf16)
        kk = _dot_nt(kb, kb)
        A = jnp.where(strict, bc * kk * decay, 0.0)
        P = eye - A
        Ap = A
        for _ in range(int(math.log2(C)) - 1):
            Ap = _dot(Ap, Ap)
            P = P + _dot(P, Ap)
        Tb = P.astype(bf16)
        uv = _dot(Tb, vh * bc)
        wk = _dot(Tb, kh * (bc * jnp.exp(Gc)))
        qk = _dot_nt(qh, kb) * decay
        kdec = kh * jnp.exp(gl - Gc)
        S = s_ref[h]
        Sb = S.astype(bf16)
        u = uv - _dot(wk, Sb)
        o_ref[:, lo:lo + GDN_DK] = _dot(qh * jnp.exp(Gc), Sb) + _dot(qk, u)
        s_ref[h] = S * jnp.exp(gl) + _dot_tn(kdec, u)


def _gdn_chunk(q, k, v, slab, s0, *, seq_len):
    T = q.shape[0]
    B = T // seq_len
    N = seq_len // GDN_CHUNK
    row = lambda b, n: (b * N + n, 0)
    st = pl.BlockSpec((None, GDN_HEADS, GDN_DK, GDN_DK), lambda b, n: (b, 0, 0, 0))
    wide = pl.BlockSpec((GDN_CHUNK, GDN_WIDTH), row)
    return pl.pallas_call(
        _gdn_chunk_kernel,
        grid=(B, N),
        in_specs=[wide, wide, wide, pl.BlockSpec((GDN_CHUNK, V7X_LANES), row), st],
        out_specs=[wide, st],
        out_shape=[jax.ShapeDtypeStruct((T, GDN_WIDTH), f32),
                   jax.ShapeDtypeStruct((B, GDN_HEADS, GDN_DK, GDN_DK), f32)],
        compiler_params=_cparams("parallel", "arbitrary"),
        name="gdn_chunk",
    )(q, k, v, slab, s0)


def _gdn_out_kernel(y_ref, o_ref, z_ref, nw_ref, wout_ref, out_ref, g_ref):
    for h in range(GDN_HEADS):
        lo = h * GDN_DK
        o = _rms_rows(o_ref[:, lo:lo + GDN_DK], nw_ref[...])
        g_ref[:, lo:lo + GDN_DK] = (o * _silu(z_ref[:, lo:lo + GDN_DK])).astype(bf16)
    out_ref[...] = y_ref[...] + jnp.dot(g_ref[...], wout_ref[...], preferred_element_type=f32)


def _gdn_out(y, o, z, nw, wout, *, tm):
    T = y.shape[0]
    row = lambda i: (i, 0)
    const = lambda i: (0, 0)
    wide = pl.BlockSpec((tm, GDN_WIDTH), row)
    return pl.pallas_call(
        _gdn_out_kernel,
        grid=(T // tm,),
        in_specs=[pl.BlockSpec((tm, D_MODEL), row), wide, wide, pl.BlockSpec((1, GDN_DK), const),
                  pl.BlockSpec((GDN_WIDTH, D_MODEL), const)],
        out_specs=pl.BlockSpec((tm, D_MODEL), row),
        out_shape=jax.ShapeDtypeStruct((T, D_MODEL), f32),
        scratch_shapes=[pltpu.VMEM((tm, GDN_WIDTH), bf16)],
        compiler_params=_cparams("parallel"),
        name="gdn_out",
    )(y, o, z, nw, wout)


NSA_IN_W = Q_WIDTH + 6 * KV_WIDTH + 3 * NSA_HEADS
NSA_IN_WP = _round_up(NSA_IN_W, V7X_LANES)
KV0 = Q_WIDTH
GATE0 = Q_WIDTH + 6 * KV_WIDTH


def _lane_iota(shape):
    return lax.broadcasted_iota(jnp.int32, shape, 1)


def _rope_pair(x, cos, sin):
    half = HEAD_DIM // 2
    lane = _lane_iota(x.shape)
    rot = jnp.where(lane % HEAD_DIM < half, pltpu.roll(x, V7X_LANES - half, 1), pltpu.roll(x, half, 1))
    return x * cos + rot * sin


def _split_pair(x, fill):
    lane = _lane_iota(x.shape)
    return jnp.where(lane < HEAD_DIM, x, fill), jnp.where(lane < HEAD_DIM, pltpu.roll(x, HEAD_DIM, 1), fill)


def _nsa_in_kernel(y_ref, nw_ref, w_ref, cos_ref, sin_ref, oh_ref,
                   qraw_ref, qrot_ref, cmp_ref, sel_ref, win_ref, ksel_ref, vsel_ref, kwin_ref, vwin_ref, gate_ref,
                   proj_ref):
    h = _rms_rows(y_ref[...], nw_ref[...]).astype(bf16)
    nc = NSA_IN_WP // 3
    for c in range(3):
        proj_ref[:, c * nc:(c + 1) * nc] = jnp.dot(h, w_ref[:, c * nc:(c + 1) * nc], preferred_element_type=f32)
    cos = cos_ref[...]
    sin = sin_ref[...]
    oh = oh_ref[...]
    L2 = 2 * V7X_LANES
    for p in range(Q_WIDTH // V7X_LANES):
        x = proj_ref[:, p * V7X_LANES:(p + 1) * V7X_LANES]
        qraw_ref[:, p * V7X_LANES:(p + 1) * V7X_LANES] = x.astype(bf16)
        a, b = _split_pair(_rope_pair(x, cos, sin) * ATTN_SCALE, 0.0)
        qrot_ref[:, p * L2:p * L2 + V7X_LANES] = a.astype(bf16)
        qrot_ref[:, p * L2 + V7X_LANES:(p + 1) * L2] = b.astype(bf16)
    cmp_ref[...] = proj_ref[:, KV0:KV0 + 2 * KV_WIDTH]
    for br, (o_ref, k_ref, v_ref, fill) in enumerate(((sel_ref, ksel_ref, vsel_ref, oh), (win_ref, kwin_ref, vwin_ref, 0.0))):
        base = KV0 + (br + 1) * 2 * KV_WIDTH
        for p in range(KV_WIDTH // V7X_LANES):
            lo = p * V7X_LANES
            kx = _rope_pair(proj_ref[:, base + lo:base + lo + V7X_LANES], cos, sin)
            o_ref[:, lo:lo + V7X_LANES] = kx
            a, b = _split_pair(kx, fill)
            k_ref[:, p * L2:p * L2 + V7X_LANES] = a.astype(bf16)
            k_ref[:, p * L2 + V7X_LANES:(p + 1) * L2] = b.astype(bf16)
            vx = proj_ref[:, base + KV_WIDTH + lo:base + KV_WIDTH + lo + V7X_LANES]
            o_ref[:, KV_WIDTH + lo:KV_WIDTH + lo + V7X_LANES] = vx
            a, b = _split_pair(vx, 0.0)
            v_ref[:, p * L2:p * L2 + V7X_LANES] = a.astype(bf16)
            v_ref[:, p * L2 + V7X_LANES:(p + 1) * L2] = b.astype(bf16)
    gate_ref[...] = _sigmoid(proj_ref[:, GATE0:GATE0 + V7X_LANES])


def _nsa_in(y, nw, w, cos, sin, onehot, *, tm, tiles_per_seq):
    T = y.shape[0]
    row = lambda i: (i, 0)
    const = lambda i: (0, 0)
    tab = lambda i: (i % tiles_per_seq, 0)
    P4 = NSA_KV_HEADS * V7X_LANES
    blk = lambda w_: pl.BlockSpec((tm, w_), row)
    shp = lambda w_, dt: jax.ShapeDtypeStruct((T, w_), dt)
    return pl.pallas_call(
        _nsa_in_kernel,
        grid=(T // tm,),
        in_specs=[blk(D_MODEL), pl.BlockSpec((1, D_MODEL), const),
                  pl.BlockSpec((D_MODEL, NSA_IN_WP), const, pipeline_mode=pl.Buffered(1)),
                  pl.BlockSpec((tm, V7X_LANES), tab), pl.BlockSpec((tm, V7X_LANES), tab),
                  pl.BlockSpec((tm, V7X_LANES), tab)],
        out_specs=[blk(Q_WIDTH), blk(2 * Q_WIDTH), blk(2 * KV_WIDTH), blk(2 * KV_WIDTH), blk(2 * KV_WIDTH),
                   blk(P4), blk(P4), blk(P4), blk(P4), blk(V7X_LANES)],
        out_shape=[shp(Q_WIDTH, bf16), shp(2 * Q_WIDTH, bf16), shp(2 * KV_WIDTH, f32), shp(2 * KV_WIDTH, f32),
                   shp(2 * KV_WIDTH, f32), shp(P4, bf16), shp(P4, bf16), shp(P4, bf16), shp(P4, bf16),
                   shp(V7X_LANES, f32)],
        scratch_shapes=[pltpu.VMEM((tm, NSA_IN_WP), f32)],
        compiler_params=_cparams("parallel"),
        name="nsa_in",
    )(y, nw, w, cos, sin, onehot)


CMP_ROW = NSA_BLOCK * 2 * KV_WIDTH


def _compress_kernel(x_ref, pe_ref, w_ref, o_ref, acc_ref):
    kk = pl.program_id(1)

    @pl.when(kk == 0)
    def _():
        acc_ref[...] = jnp.zeros_like(acc_ref)

    acc_ref[...] += jnp.dot((x_ref[...] + pe_ref[...]).astype(bf16), w_ref[...], preferred_element_type=f32)

    @pl.when(kk == pl.num_programs(1) - 1)
    def _():
        o_ref[...] = acc_ref[...]


def _compress(x, pe_row, wbig, *, tm, tk):
    M = x.shape[0]
    N = 2 * KV_WIDTH
    return pl.pallas_call(
        _compress_kernel,
        grid=(M // tm, CMP_ROW // tk),
        in_specs=[pl.BlockSpec((tm, tk), lambda i, k: (i, k)), pl.BlockSpec((1, tk), lambda i, k: (0, k)),
                  pl.BlockSpec((tk, N), lambda i, k: (k, 0))],
        out_specs=pl.BlockSpec((tm, N), lambda i, k: (i, 0)),
        out_shape=jax.ShapeDtypeStruct((M, N), f32),
        scratch_shapes=[pltpu.VMEM((tm, N), f32)],
        compiler_params=_cparams("parallel", "arbitrary"),
        name="nsa_compress",
    )(x, pe_row, wbig)


def _cmp_select_kernel(q_ref, kbd_ref, vbd_ref, oc_ref, bias_ref, *, tq, nb, n_top):
    i = pl.program_id(1)
    G = NSA_GROUP
    gw = G * HEAD_DIM
    blk = lax.broadcasted_iota(jnp.int32, (nb, tq), 0)
    pos = i * tq + lax.broadcasted_iota(jnp.int32, (nb, tq), 1)
    complete = (blk + 1) * NSA_BLOCK - 1 <= pos
    cur = lax.shift_right_logical(pos, int(math.log2(NSA_BLOCK)))
    cand = blk < cur
    for g in range(NSA_KV_HEADS):
        sT = _dot_nt(kbd_ref[g], q_ref[:, g * gw:(g + 1) * gw]) * ATTN_SCALE
        ps = []
        imp = jnp.zeros((nb, tq), f32)
        for n in range(G):
            sm = jnp.where(complete, sT[n * nb:(n + 1) * nb], NEG)
            mx = jnp.max(sm, axis=0, keepdims=True)
            e = jnp.where(complete, jnp.exp(sm - mx), 0.0)
            den = jnp.sum(e, axis=0, keepdims=True)
            p = e / jnp.where(den > 0.0, den, 1.0)
            ps.append(p)
            imp = imp + p
        oc_ref[:, g * gw:(g + 1) * gw] = _dot_tn(jnp.concatenate(ps, axis=0), vbd_ref[g])
        vals = jnp.where(cand, imp, -1.0)
        cnt = jnp.zeros((nb, tq), jnp.int32)
        for j in range(nb):
            vj = vals[j:j + 1, :]
            cnt = cnt + ((vj > vals) | ((vj == vals) & (blk > j))).astype(jnp.int32)
        sel = (cand & (cnt < n_top)) | (blk == cur)
        biasT = jnp.where(sel, 0.0, NEG)
        full = jnp.concatenate([jnp.zeros((HEAD_DIM, tq), f32), biasT,
                                jnp.zeros((V7X_LANES - HEAD_DIM - nb, tq), f32)], axis=0)
        bias_ref[:, g * V7X_LANES:(g + 1) * V7X_LANES] = full.T.astype(bf16)


def _cmp_select(q_raw, kbd, vbd, *, seq_len, tq):
    T = q_raw.shape[0]
    nq = seq_len // tq
    nb = seq_len // NSA_BLOCK
    row = lambda b, i: (b * nq + i, 0)
    kern = functools.partial(_cmp_select_kernel, tq=tq, nb=nb, n_top=min(N_SEL - 1, nb))
    return pl.pallas_call(
        kern,
        grid=(T // seq_len, nq),
        in_specs=[pl.BlockSpec((tq, Q_WIDTH), row),
                  pl.BlockSpec((None,) + kbd.shape[1:], lambda b, i: (b, 0, 0, 0)),
                  pl.BlockSpec((None,) + vbd.shape[1:], lambda b, i: (b, 0, 0, 0))],
        out_specs=[pl.BlockSpec((tq, Q_WIDTH), row), pl.BlockSpec((tq, NSA_KV_HEADS * V7X_LANES), row)],
        out_shape=[jax.ShapeDtypeStruct((T, Q_WIDTH), f32),
                   jax.ShapeDtypeStruct((T, NSA_KV_HEADS * V7X_LANES), bf16)],
        compiler_params=_cparams("parallel", "parallel"),
        name="nsa_cmp_select",
    )(q_raw, kbd, vbd)


def _attn_kernel(*refs, tq, mode):
    if mode == "sel":
        q_ref, bias_ref, k_ref, v_ref, o_ref, m_ref, l_ref, acc_ref = refs
    else:
        q_ref, k_ref, v_ref, o_ref, m_ref, l_ref, acc_ref = refs
    i = pl.program_id(2)
    G = NSA_GROUP
    qs = []
    for n in range(G):
        qn = q_ref[:, n * V7X_LANES:(n + 1) * V7X_LANES]
        if mode == "sel":
            qn = qn + bias_ref[...]
        qs.append(qn)
    qa = jnp.concatenate(qs, axis=0)
    qpos = i * tq + lax.broadcasted_iota(jnp.int32, (G * tq, tq), 0) % tq
    kofs = lax.broadcasted_iota(jnp.int32, (G * tq, tq), 1)
    m_ref[...] = jnp.full(m_ref.shape, NEG, f32)
    l_ref[...] = jnp.zeros(l_ref.shape, f32)
    acc_ref[...] = jnp.zeros(acc_ref.shape, f32)

    def tile(t, mask_fn):
        start = pl.multiple_of(t * tq, tq)
        s = _dot_nt(qa, k_ref[pl.ds(start, tq), :])
        if mask_fn is not None:
            s = jnp.where(mask_fn(t * tq + kofs), s, NEG)
        m_old = m_ref[...]
        m_new = jnp.maximum(m_old, jnp.max(s, axis=-1, keepdims=True))
        alpha = jnp.exp(m_old - m_new)
        p = jnp.exp(s - m_new)
        l_ref[...] = alpha * l_ref[...] + jnp.sum(p, axis=-1, keepdims=True)
        acc_ref[...] = alpha * acc_ref[...] + jnp.dot(p.astype(bf16), v_ref[pl.ds(start, tq), :],
                                                      preferred_element_type=f32)
        m_ref[...] = m_new

    tile(i, lambda kpos: kpos <= qpos)
    if mode == "sel":
        def body(t, c):
            tile(t, None)
            return c
        lax.fori_loop(0, i, body, 0)
    else:
        @pl.when(i >= 1)
        def _():
            tile(i - 1, None)

        @pl.when(i >= 2)
        def _():
            tile(i - 2, lambda kpos: qpos - kpos < WINDOW)

    o = acc_ref[...] / l_ref[...]
    lane = _lane_iota((tq, V7X_LANES))
    for p in range(G // 2):
        a = o[(2 * p) * tq:(2 * p + 1) * tq]
        b = o[(2 * p + 1) * tq:(2 * p + 2) * tq]
        o_ref[:, p * V7X_LANES:(p + 1) * V7X_LANES] = jnp.where(lane < HEAD_DIM, a, pltpu.roll(b, HEAD_DIM, 1))


def _attn(q_pad, bias, k_pad, v_pad, *, seq_len, tq, mode):
    assert mode in ("sel", "win") and (mode == "sel" or 2 * tq == WINDOW)
    T = q_pad.shape[0]
    B = T // seq_len
    nq = seq_len // tq
    G = NSA_GROUP
    qrow = lambda b, g, i: (b * nq + i, g)
    kv = lambda b, g, i: (b, g)
    in_specs = [pl.BlockSpec((tq, G * V7X_LANES), qrow)]
    args = [q_pad]
    if mode == "sel":
        in_specs.append(pl.BlockSpec((tq, V7X_LANES), qrow))
        args.append(bias)
    in_specs += [pl.BlockSpec((seq_len, V7X_LANES), kv), pl.BlockSpec((seq_len, V7X_LANES), kv)]
    args += [k_pad, v_pad]
    return pl.pallas_call(
        functools.partial(_attn_kernel, tq=tq, mode=mode),
        grid=(B, NSA_KV_HEADS, nq),
        in_specs=in_specs,
        out_specs=pl.BlockSpec((tq, G * HEAD_DIM), qrow),
        out_shape=jax.ShapeDtypeStruct((T, Q_WIDTH), f32),
        scratch_shapes=[pltpu.VMEM((G * tq, 1), f32), pltpu.VMEM((G * tq, 1), f32),
                        pltpu.VMEM((G * tq, V7X_LANES), f32)],
        compiler_params=_cparams("parallel", "parallel", "parallel"),
        name="nsa_attn_" + mode,
    )(*args)


def _nsa_out_kernel(y_ref, oc_ref, os_ref, ow_ref, gate_ref, e_ref, wout_ref, out_ref):
    g = gate_ref[...]
    ghi = g.astype(bf16)
    glo = (g - ghi.astype(f32)).astype(bf16)
    o = jnp.zeros(oc_ref.shape, f32)
    for j, br in enumerate((oc_ref, os_ref, ow_ref)):
        ge = (jnp.dot(ghi, e_ref[j], preferred_element_type=f32) + jnp.dot(glo, e_ref[j], preferred_element_type=f32))
        o = o + ge * br[...]
    out_ref[...] = y_ref[...] + jnp.dot(o.astype(bf16), wout_ref[...], preferred_element_type=f32)


def _nsa_out(y, oc, osel, ow, gates, expand, wout, *, tm):
    T = y.shape[0]
    row = lambda i: (i, 0)
    wide = pl.BlockSpec((tm, Q_WIDTH), row)
    return pl.pallas_call(
        _nsa_out_kernel,
        grid=(T // tm,),
        in_specs=[pl.BlockSpec((tm, D_MODEL), row), wide, wide, wide, pl.BlockSpec((tm, V7X_LANES), row),
                  pl.BlockSpec((3, V7X_LANES, Q_WIDTH), lambda i: (0, 0, 0)),
                  pl.BlockSpec((Q_WIDTH, D_MODEL), lambda i: (0, 0))],
        out_specs=pl.BlockSpec((tm, D_MODEL), row),
        out_shape=jax.ShapeDtypeStruct((T, D_MODEL), f32),
        compiler_params=_cparams("parallel"),
        name="nsa_out",
    )(y, oc, osel, ow, gates, expand, wout)


def _final_norm_kernel(y_ref, nw_ref, o_ref):
    o_ref[...] = _rms_rows(y_ref[...], nw_ref[...])


def _final_norm(y, nw, *, tm):
    T = y.shape[0]
    return pl.pallas_call(
        _final_norm_kernel,
        grid=(T // tm,),
        in_specs=[pl.BlockSpec((tm, D_MODEL), lambda i: (i, 0)), pl.BlockSpec((1, D_MODEL), lambda i: (0, 0))],
        out_specs=pl.BlockSpec((tm, D_MODEL), lambda i: (i, 0)),
        out_shape=jax.ShapeDtypeStruct((T, D_MODEL), f32),
        compiler_params=_cparams("parallel"),
        name="final_norm",
    )(y, nw)


DEC_ROWS = V7X_SUBLANES


def _page_gather_kernel(pt_ref, tbl_ref, o_ref, *, n_pages):
    b = pl.program_id(0)
    for p in range(n_pages):
        o_ref[p:p + 1, :] = tbl_ref[pl.ds(pt_ref[b, p], 1), :]


def _page_gather(page_table, tbl):
    DB, n_pages = page_table.shape
    n_phys, W = tbl.shape
    return pl.pallas_call(
        functools.partial(_page_gather_kernel, n_pages=n_pages),
        grid_spec=pltpu.PrefetchScalarGridSpec(
            num_scalar_prefetch=1,
            grid=(DB,),
            in_specs=[pl.BlockSpec((n_phys, W), lambda b, pt: (0, 0), pipeline_mode=pl.Buffered(1))],
            out_specs=pl.BlockSpec((None, n_pages, W), lambda b, pt: (b, 0, 0)),
        ),
        out_shape=jax.ShapeDtypeStruct((DB, n_pages, W), tbl.dtype),
        compiler_params=_cparams("arbitrary"),
        name="nsa_page_gather",
    )(page_table, tbl)


def _cmp_decode_kernel(q_ref, ckv_ref, oc_ref, bias_ref, *, past, n_top):
    R = DEC_ROWS
    nb = ckv_ref.shape[0]
    blk = lax.broadcasted_iota(jnp.int32, (R, nb), 1)
    pos = past + lax.broadcasted_iota(jnp.int32, (R, nb), 0)
    complete = (blk + 1) * NSA_BLOCK - 1 <= pos
    cand = blk < lax.shift_right_logical(pos, int(math.log2(NSA_BLOCK)))
    for g in range(NSA_KV_HEADS):
        ck = ckv_ref[:, g * HEAD_DIM:(g + 1) * HEAD_DIM]
        cv = ckv_ref[:, KV_WIDTH + g * HEAD_DIM:KV_WIDTH + (g + 1) * HEAD_DIM]
        imp = jnp.zeros((R, nb), f32)
        for n in range(NSA_GROUP):
            lo = (g * NSA_GROUP + n) * HEAD_DIM
            sm = jnp.where(complete, _dot_nt(q_ref[:, lo:lo + HEAD_DIM], ck) * ATTN_SCALE, NEG)
            mx = jnp.max(sm, axis=-1, keepdims=True)
            e = jnp.where(complete, jnp.exp(sm - mx), 0.0)
            den = jnp.sum(e, axis=-1, keepdims=True)
            p = e / jnp.where(den > 0.0, den, 1.0)
            imp = imp + p
            oc_ref[:, lo:lo + HEAD_DIM] = _dot(p, cv)
        vals = jnp.where(cand, imp, -1.0)
        cnt = jnp.zeros((R, nb), jnp.int32)
        for s in range(1, nb):
            other = pltpu.roll(vals, s, 1)
            cnt = cnt + ((other > vals) | ((other == vals) & (blk >= s))).astype(jnp.int32)
        bias_ref[g] = jnp.where(cand & (cnt < n_top), 0.0, NEG)


def _cmp_decode(q_raw, ckv, *, past):
    DB, nb = ckv.shape[:2]
    assert nb == V7X_LANES
    return pl.pallas_call(
        functools.partial(_cmp_decode_kernel, past=past, n_top=min(N_SEL - 1, nb)),
        grid=(DB,),
        in_specs=[pl.BlockSpec((None, DEC_ROWS, Q_WIDTH), lambda b: (b, 0, 0)),
                  pl.BlockSpec((None, nb, 2 * KV_WIDTH), lambda b: (b, 0, 0))],
        out_specs=[pl.BlockSpec((None, DEC_ROWS, Q_WIDTH), lambda b: (b, 0, 0)),
                   pl.BlockSpec((None, NSA_KV_HEADS, DEC_ROWS, nb), lambda b: (b, 0, 0, 0))],
        out_shape=[jax.ShapeDtypeStruct((DB, DEC_ROWS, Q_WIDTH), f32),
                   jax.ShapeDtypeStruct((DB, NSA_KV_HEADS, DEC_ROWS, nb), f32)],
        compiler_params=_cparams("parallel"),
        name="nsa_cmp_decode",
    )(q_raw, ckv)


SEL_PAGES_PER_STEP = 8


def _sel_decode_kernel(pt_ref, q_ref, bias_ref, new_ref, pool_ref, o_ref, buf_ref, sem_ref, m_ref, l_ref, acc_ref,
                       *, n_seq, n_pages, n_new):
    PG = SEL_PAGES_PER_STEP
    NG = n_pages // PG
    R = NSA_GROUP * DEC_ROWS
    keys = PG * PAGE_SIZE
    bps = keys // NSA_BLOCK

    def copies(step, slot):
        b = step // NG
        j = step % NG
        return [pltpu.make_async_copy(pool_ref.at[pt_ref[b, j * PG + p]], buf_ref.at[slot, p], sem_ref.at[slot])
                for p in range(PG)]

    def online(g, s, v):
        m_old = m_ref[g]
        m_new = jnp.maximum(m_old, jnp.max(s, axis=-1, keepdims=True))
        alpha = jnp.exp(m_old - m_new)
        p = jnp.exp(s - m_new)
        l_ref[g] = alpha * l_ref[g] + jnp.sum(p, axis=-1, keepdims=True)
        acc_ref[g] = alpha * acc_ref[g] + _dot(p, v)
        m_ref[g] = m_new

    for c in copies(0, 0):
        c.start()

    def body(step, carry):
        slot = step % 2
        b = step // NG
        j = step % NG

        @pl.when(step + 1 < n_seq * NG)
        def _():
            for c in copies(step + 1, 1 - slot):
                c.start()

        @pl.when(j == 0)
        def _():
            m_ref[...] = jnp.full(m_ref.shape, NEG, f32)
            l_ref[...] = jnp.zeros(l_ref.shape, f32)
            acc_ref[...] = jnp.zeros(acc_ref.shape, f32)
            tok = lax.broadcasted_iota(jnp.int32, (R, DEC_ROWS), 0) % DEC_ROWS
            key = lax.broadcasted_iota(jnp.int32, (R, DEC_ROWS), 1)
            causal = (key <= tok) & (key < n_new)
            for g in range(NSA_KV_HEADS):
                kn = new_ref[b, :, g * HEAD_DIM:(g + 1) * HEAD_DIM]
                vn = new_ref[b, :, KV_WIDTH + g * HEAD_DIM:KV_WIDTH + (g + 1) * HEAD_DIM]
                s = jnp.where(causal, _dot_nt(q_ref[b, g][:, :HEAD_DIM], kn), NEG)
                online(g, s, vn)

        for c in copies(step, slot):
            c.wait()
        kv = buf_ref[slot].reshape(keys, 2 * KV_WIDTH)
        blk = lax.broadcasted_iota(jnp.int32, (V7X_LANES, keys), 0)
        kblk = j * bps + lax.shift_right_logical(lax.broadcasted_iota(jnp.int32, (V7X_LANES, keys), 1),
                                                 int(math.log2(NSA_BLOCK)))
        expand = (blk == kblk).astype(bf16)
        for g in range(NSA_KV_HEADS):
            kg = kv[:, g * HEAD_DIM:(g + 1) * HEAD_DIM]
            vg = kv[:, KV_WIDTH + g * HEAD_DIM:KV_WIDTH + (g + 1) * HEAD_DIM]
            bias = jnp.concatenate([bias_ref[b, g]] * NSA_GROUP, axis=0).astype(bf16)
            s = _dot_nt(q_ref[b, g][:, :HEAD_DIM], kg) + jnp.dot(bias, expand, preferred_element_type=f32)
            online(g, s, vg)

        @pl.when(j == NG - 1)
        def _():
            for g in range(NSA_KV_HEADS):
                o = acc_ref[g] / l_ref[g]
                for n in range(NSA_GROUP):
                    lo = (g * NSA_GROUP + n) * HEAD_DIM
                    o_ref[b, :, lo:lo + HEAD_DIM] = o[n * DEC_ROWS:(n + 1) * DEC_ROWS]

        return carry

    lax.fori_loop(0, n_seq * NG, body, 0)


def _sel_decode(page_table, q, bias, new_kv, pool, *, n_new):
    DB, n_pages = page_table.shape
    assert n_pages % SEL_PAGES_PER_STEP == 0 and n_pages * PAGE_SIZE // NSA_BLOCK == V7X_LANES
    R = NSA_GROUP * DEC_ROWS
    whole = lambda shape: pl.BlockSpec(shape, lambda i, pt, _n=len(shape): (0,) * _n)
    return pl.pallas_call(
        functools.partial(_sel_decode_kernel, n_seq=DB, n_pages=n_pages, n_new=n_new),
        grid_spec=pltpu.PrefetchScalarGridSpec(
            num_scalar_prefetch=1,
            grid=(1,),
            in_specs=[whole(q.shape), whole(bias.shape), whole(new_kv.shape), pl.BlockSpec(memory_space=pl.ANY)],
            out_specs=whole((DB, DEC_ROWS, Q_WIDTH)),
            scratch_shapes=[pltpu.VMEM((2, SEL_PAGES_PER_STEP, PAGE_SIZE, 2 * KV_WIDTH), f32),
                            pltpu.SemaphoreType.DMA((2,)),
                            pltpu.VMEM((NSA_KV_HEADS, R, 1), f32), pltpu.VMEM((NSA_KV_HEADS, R, 1), f32),
                            pltpu.VMEM((NSA_KV_HEADS, R, HEAD_DIM), f32)],
        ),
        out_shape=jax.ShapeDtypeStruct((DB, DEC_ROWS, Q_WIDTH), f32),
        compiler_params=_cparams("arbitrary"),
        name="nsa_sel_decode",
    )(page_table, q, bias, new_kv, pool)


def _win_decode_kernel(q_ref, buf_ref, new_ref, o_ref, win_ref, ext_ref, *, n_new):
    wbuf = buf_ref.shape[0]
    R = NSA_GROUP * DEC_ROWS
    ext_ref[0:wbuf, :] = buf_ref[...]
    ext_ref[wbuf:wbuf + DEC_ROWS, :] = new_ref[...]
    win_ref[...] = ext_ref[n_new:n_new + wbuf, :]
    tok = lax.broadcasted_iota(jnp.int32, (R, wbuf), 0) % DEC_ROWS
    dist = wbuf + tok - lax.broadcasted_iota(jnp.int32, (R, wbuf), 1)
    in_buf = (dist >= 0) & (dist < WINDOW)
    tokn = lax.broadcasted_iota(jnp.int32, (R, DEC_ROWS), 0) % DEC_ROWS
    keyn = lax.broadcasted_iota(jnp.int32, (R, DEC_ROWS), 1)
    in_new = (keyn <= tokn) & (keyn < n_new)
    for g in range(NSA_KV_HEADS):
        q = q_ref[g][:, :HEAD_DIM]
        kc = slice(g * HEAD_DIM, (g + 1) * HEAD_DIM)
        vc = slice(KV_WIDTH + g * HEAD_DIM, KV_WIDTH + (g + 1) * HEAD_DIM)
        s1 = jnp.where(in_buf, _dot_nt(q, buf_ref[:, kc]), NEG)
        s2 = jnp.where(in_new, _dot_nt(q, new_ref[:, kc]), NEG)
        m = jnp.maximum(jnp.max(s1, axis=-1, keepdims=True), jnp.max(s2, axis=-1, keepdims=True))
        p1 = jnp.exp(s1 - m)
        p2 = jnp.exp(s2 - m)
        den = jnp.sum(p1, axis=-1, keepdims=True) + jnp.sum(p2, axis=-1, keepdims=True)
        o = (_dot(p1, buf_ref[:, vc]) + _dot(p2, new_ref[:, vc])) / den
        for n in range(NSA_GROUP):
            lo = (g * NSA_GROUP + n) * HEAD_DIM
            o_ref[:, lo:lo + HEAD_DIM] = o[n * DEC_ROWS:(n + 1) * DEC_ROWS]


def _win_decode(q, win_buf, new_kv, *, n_new):
    DB, wbuf, W = win_buf.shape
    R = NSA_GROUP * DEC_ROWS
    return pl.pallas_call(
        functools.partial(_win_decode_kernel, n_new=n_new),
        grid=(DB,),
        in_specs=[pl.BlockSpec((None, NSA_KV_HEADS, R, V7X_LANES), lambda b: (b, 0, 0, 0)),
                  pl.BlockSpec((None, wbuf, W), lambda b: (b, 0, 0)),
                  pl.BlockSpec((None, DEC_ROWS, W), lambda b: (b, 0, 0))],
        out_specs=[pl.BlockSpec((None, DEC_ROWS, Q_WIDTH), lambda b: (b, 0, 0)),
                   pl.BlockSpec((None, wbuf, W), lambda b: (b, 0, 0))],
        out_shape=[jax.ShapeDtypeStruct((DB, DEC_ROWS, Q_WIDTH), f32), jax.ShapeDtypeStruct((DB, wbuf, W), f32)],
        scratch_shapes=[pltpu.VMEM((wbuf + DEC_ROWS, W), f32)],
        compiler_params=_cparams("parallel"),
        name="nsa_win_decode",
    )(q, win_buf, new_kv)


def _rope_tables(pos):
    half = HEAD_DIM // 2
    inv = ROPE_THETA ** (-jnp.arange(half, dtype=f32) / half)
    ang = pos.astype(f32)[:, None] * inv[None, :]
    cos, sin = jnp.cos(ang), jnp.sin(ang)
    reps = V7X_LANES // HEAD_DIM
    return jnp.tile(cos, (1, 2 * reps)), jnp.tile(jnp.concatenate([-sin, sin], axis=1), (1, reps))


def _block_onehot(pos):
    return (jnp.arange(V7X_LANES, dtype=jnp.int32)[None, :] == HEAD_DIM + pos[:, None] // NSA_BLOCK).astype(f32)


def _gate_expand():
    src = jnp.arange(V7X_LANES)[None, :, None]
    head = jnp.arange(Q_WIDTH)[None, None, :] // HEAD_DIM
    j = jnp.arange(3)[:, None, None]
    return (src == head * 3 + j).astype(bf16)


def _compress_weights(pe, w):
    wb = jnp.einsum("lcde,cx,gy->lcgdxye", w, jnp.eye(2, dtype=f32), jnp.eye(NSA_KV_HEADS, dtype=f32))
    wb = wb.reshape(CMP_ROW, 2 * KV_WIDTH).astype(bf16)
    pe_row = jnp.broadcast_to(pe[:, :, None, :], (NSA_BLOCK, 2, NSA_KV_HEADS, HEAD_DIM)).reshape(1, CMP_ROW)
    return pe_row, wb


def _time_major(x):
    return jnp.swapaxes(x, 0, 1).reshape((x.shape[0] * x.shape[1],) + x.shape[2:])


def _batch_major(x, db):
    return jnp.swapaxes(x.reshape((x.shape[0] // db, db) + x.shape[1:]), 0, 1)


def _gdn_weights(w_in, a_log, dt_bias):
    W3 = 3 * GDN_WIDTH
    pad = V7X_LANES - 2 * GDN_HEADS
    return (w_in[:, :W3].astype(bf16), w_in[:, W3:W3 + GDN_WIDTH].astype(bf16),
            jnp.pad(w_in[:, W3 + GDN_WIDTH:], ((0, 0), (0, pad))).astype(bf16),
            jnp.pad(a_log, (0, V7X_LANES - GDN_HEADS))[None], jnp.pad(dt_bias, (0, V7X_LANES - GDN_HEADS))[None])


def _gdn_layer(yp, ys, S0s, conv_s, nw, w_in, conv_w, a_log, dt_bias, norm_w, w_out, *, B, L, DB, DS):
    wqkv, wz, wab, alog, dtb = _gdn_weights(w_in, a_log, dt_bias)
    wout = w_out.astype(bf16)
    nw = nw[None]
    norm_w = norm_w[None]
    q, k, v, z, slab, conv_p = _gdn_in(yp, None, nw, wqkv, wz, wab, conv_w, alog, dtb, tm=256, shift=1, rows_per_seq=L)
    o, S_p = _gdn_chunk(q, k, v, slab, jnp.zeros((B, GDN_HEADS, GDN_DK, GDN_DK), f32), seq_len=L)
    yp = _gdn_out(yp, o, z, norm_w, wout, tm=512)
    Ts = DS * DB
    hist = _time_major(conv_s).reshape((GDN_CONV - 1) * DB, 3 * GDN_WIDTH)
    q, k, v, z, slab, conv_tm = _gdn_in(ys, hist, nw, wqkv, wz, wab, conv_w, alog, dtb, tm=Ts, shift=DB, rows_per_seq=Ts)

    def chunked(x):
        x = _batch_major(x, DB)
        return jnp.pad(x, ((0, 0), (0, GDN_CHUNK - DS), (0, 0))).reshape(DB * GDN_CHUNK, x.shape[-1])

    o, S_s = _gdn_chunk(chunked(q), chunked(k), chunked(v), chunked(slab), S0s, seq_len=GDN_CHUNK)
    o = _time_major(o.reshape(DB, GDN_CHUNK, GDN_WIDTH)[:, :DS])
    ys = _gdn_out(ys, o, z, norm_w, wout, tm=Ts)
    conv_s_new = _batch_major(conv_tm.reshape((GDN_CONV - 1) * DB, 3 * GDN_WIDTH), DB)
    return yp, ys, S_p, S_s, conv_p, conv_s_new


def _ffn_layer(yp, ys, conv_s, nw, w_up, conv_w, conv_b, w_down, *, B, L, DB, DS):
    wup = w_up.astype(bf16)
    wdn = w_down.astype(bf16)
    nw = nw[None]
    cb = conv_b[None]
    yp, st_p = _ffn(yp, None, nw, wup, conv_w, cb, wdn, tm=256, shift=1, rows_per_seq=L)
    Ts = DS * DB
    hist = _time_major(conv_s).reshape((FFN_CONV - 1) * DB, 2 * D_FF)
    ys, st_tm = _ffn(ys, hist, nw, wup, conv_w, cb, wdn, tm=Ts, shift=DB, rows_per_seq=Ts)
    st_s = _batch_major(st_tm.reshape((FFN_CONV - 1) * DB, 2 * D_FF), DB)
    return yp, ys, st_p, st_s


def _nsa_layer(yp, ys, pool_cmp, pool_sel, win_buf, page_table, nw, w_in, cmp_pe, cmp_w, w_out, *, B, L, DB, DS):
    past = page_table.shape[1] * PAGE_SIZE
    w = jnp.pad(w_in, ((0, 0), (0, NSA_IN_WP - NSA_IN_W))).astype(bf16)
    wout = w_out.astype(bf16)
    nw = nw[None]
    expand = _gate_expand()
    pe_row, wbig = _compress_weights(cmp_pe, cmp_w)
    tq = WINDOW // 2
    pos_p = jnp.arange(L, dtype=jnp.int32)
    cos, sin = _rope_tables(pos_p)
    (q_raw, q_rot, kv_cmp, kv_sel, kv_win, ksel, vsel, kwin, vwin, gates) = _nsa_in(
        yp, nw, w, cos, sin, _block_onehot(pos_p), tm=256, tiles_per_seq=L // 256)
    nb = L // NSA_BLOCK
    ckv = _compress(kv_cmp.reshape(B * nb, CMP_ROW), pe_row, wbig, tm=B * nb, tk=2048)
    ck = ckv.reshape(B, nb, 2, NSA_KV_HEADS, HEAD_DIM)
    eye = jnp.eye(NSA_GROUP, dtype=f32)
    kbd = jnp.einsum("bkgd,nm->bgnkmd", ck[:, :, 0], eye).reshape(B, NSA_KV_HEADS, NSA_GROUP * nb, NSA_GROUP * HEAD_DIM)
    vbd = jnp.einsum("bkgd,nm->bgnkmd", ck[:, :, 1], eye).reshape(B, NSA_KV_HEADS, NSA_GROUP * nb, NSA_GROUP * HEAD_DIM)
    o_c, bias = _cmp_select(q_raw, kbd.astype(bf16), vbd.astype(bf16), seq_len=L, tq=tq)
    o_s = _attn(q_rot, bias, ksel, vsel, seq_len=L, tq=tq, mode="sel")
    o_w = _attn(q_rot, None, kwin, vwin, seq_len=L, tq=tq, mode="win")
    yp = _nsa_out(yp, o_c, o_s, o_w, gates, expand, wout, tm=512)
    shp = (B, L, 2, NSA_KV_HEADS, HEAD_DIM)
    cmp_p, sel_p, win_p = kv_cmp.reshape(shp), kv_sel.reshape(shp), kv_win.reshape(shp)[:, -min(WINDOW, L):]
    Ts = DS * DB
    pos_s = past + jnp.arange(DS, dtype=jnp.int32)
    pos_rows = jnp.repeat(pos_s, DB)
    cos, sin = _rope_tables(pos_rows)
    (q_raw, q_rot, kv_cmp, kv_sel, kv_win, _, _, _, _, gates) = _nsa_in(
        ys, nw, w, cos, sin, jnp.zeros((Ts, V7X_LANES), f32), tm=Ts, tiles_per_seq=1)
    n_phys = pool_cmp.shape[0]
    ppb = PAGE_SIZE // NSA_BLOCK
    W = 2 * KV_WIDTH
    ckv_all = _compress(pool_cmp.reshape(n_phys * ppb, CMP_ROW), pe_row, wbig, tm=_pick_tile(n_phys * ppb, 1024), tk=2048)
    ckv = _page_gather(page_table, ckv_all.reshape(n_phys, ppb * W)).reshape(DB, past // NSA_BLOCK, W)

    def rows8(x):
        return jnp.pad(_batch_major(x, DB), ((0, 0), (0, DEC_ROWS - DS), (0, 0)))

    o_c, bias = _cmp_decode(rows8(q_raw), ckv, past=past)
    q4 = rows8(q_rot).reshape(DB, DEC_ROWS, NSA_KV_HEADS, NSA_GROUP, V7X_LANES)
    q4 = q4.transpose(0, 2, 3, 1, 4).reshape(DB, NSA_KV_HEADS, NSA_GROUP * DEC_ROWS, V7X_LANES)
    o_s = _sel_decode(page_table, q4, bias, rows8(kv_sel), pool_sel.reshape(n_phys, PAGE_SIZE, W), n_new=DS)
    wbuf = win_buf.shape[1]
    o_w, win_s = _win_decode(q4, win_buf.reshape(DB, wbuf, W), rows8(kv_win), n_new=DS)
    ks = (DB, DS, 2, NSA_KV_HEADS, HEAD_DIM)
    cmp_s, sel_s = (_batch_major(t, DB).reshape(ks) for t in (kv_cmp, kv_sel))
    win_s = win_s.reshape((DB, wbuf) + ks[2:])
    tm_rows = lambda o: _time_major(o[:, :DS])
    ys = _nsa_out(ys, tm_rows(o_c), tm_rows(o_s), tm_rows(o_w), gates, expand, wout, tm=Ts)
    return yp, ys, cmp_p, cmp_s, sel_p, sel_s, win_p, win_s


def kernel(x_prompt, x_sample, state_gdn, state_gdn_conv, cache_cmp, cache_sel, state_win, state_ffn_conv, page_table,
           norm_mix, norm_ffn, norm_final, gdn_w_in, gdn_conv_w, gdn_a_log, gdn_dt_bias, gdn_norm_w, gdn_w_out,
           nsa_w_in, nsa_cmp_pe, nsa_cmp_w, nsa_w_out, ffn_w_up, ffn_conv_w, ffn_conv_b, ffn_w_down):
    B, L, _ = x_prompt.shape
    DB, DS, _ = x_sample.shape
    depth = norm_mix.shape[0]
    dims = dict(B=B, L=L, DB=DB, DS=DS)
    yp = x_prompt.reshape(B * L, D_MODEL)
    ys = _time_major(x_sample)
    outs = [[] for _ in range(12)]
    for i in range(depth):
        j = i // 2
        if i % 2 == 0:
            yp, ys, S_p, S_s, c_p, c_s = _gdn_layer(yp, ys, state_gdn[j], state_gdn_conv[j], norm_mix[i], gdn_w_in[j],
                                                    gdn_conv_w[j], gdn_a_log[j], gdn_dt_bias[j], gdn_norm_w[j],
                                                    gdn_w_out[j], **dims)
            for lst, val in zip(outs[0:4], (S_p, S_s, c_p, c_s)):
                lst.append(val)
        else:
            yp, ys, *kv = _nsa_layer(yp, ys, cache_cmp[j], cache_sel[j], state_win[j], page_table, norm_mix[i],
                                     nsa_w_in[j], nsa_cmp_pe[j], nsa_cmp_w[j], nsa_w_out[j], **dims)
            for lst, val in zip(outs[4:10], kv):
                lst.append(val)
        yp, ys, f_p, f_s = _ffn_layer(yp, ys, state_ffn_conv[i], norm_ffn[i], ffn_w_up[i], ffn_conv_w[i],
                                      ffn_conv_b[i], ffn_w_down[i], **dims)
        outs[10].append(f_p)
        outs[11].append(f_s)
    y_prompt = _final_norm(yp, norm_final[None], tm=512).reshape(B, L, D_MODEL)
    y_sample = _batch_major(_final_norm(ys, norm_final[None], tm=DS * DB), DB)
    return (y_prompt, y_sample) + tuple(jnp.stack(o) for o in outs)
```

```python
import functools
import math

import jax
import jax.numpy as jnp
from jax import lax
from jax.experimental import pallas as pl
from jax.experimental.pallas import tpu as pltpu

f32 = jnp.float32
bf16 = jnp.bfloat16

D_MODEL = 1024
GDN_HEADS = 8
GDN_DK = 128
GDN_WIDTH = GDN_HEADS * GDN_DK
GDN_CONV = 4
GDN_CHUNK = 64
NSA_HEADS = 16
NSA_KV_HEADS = 4
NSA_GROUP = 4
HEAD_DIM = 64
Q_WIDTH = NSA_HEADS * HEAD_DIM
KV_WIDTH = NSA_KV_HEADS * HEAD_DIM
NSA_BLOCK = 64
N_SEL = 16
WINDOW = 512
PAGE_SIZE = 128
ROPE_THETA = 10000.0
ATTN_SCALE = HEAD_DIM ** -0.5
D_FF = 2816
FFN_CONV = 3
RMS_EPS = 1e-6
NEG = -1e30

V7X_LANES = 128
V7X_SUBLANES = 8
V7X_VMEM_BYTES = 64 * 1024 * 1024
VMEM_LIMIT = 56 * 1024 * 1024


def _cparams(*sem):
    return pltpu.CompilerParams(dimension_semantics=sem, vmem_limit_bytes=VMEM_LIMIT)


def _round_up(x, m):
    return (x + m - 1) // m * m


def _pick_tile(n, cap):
    best = None
    for t in range(V7X_SUBLANES, min(n, cap) + 1, V7X_SUBLANES):
        if n % t == 0:
            best = t
    assert best is not None, (n, cap)
    return best


def _dot(a, b):
    return jnp.dot(a.astype(bf16), b.astype(bf16), preferred_element_type=f32)


def _dot_nt(a, b):
    return lax.dot_general(a.astype(bf16), b.astype(bf16), (((1,), (1,)), ((), ())), preferred_element_type=f32)


def _dot_tn(a, b):
    return lax.dot_general(a.astype(bf16), b.astype(bf16), (((0,), (0,)), ((), ())), preferred_element_type=f32)


def _split3(x):
    hi = x.astype(bf16)
    r = x - hi.astype(f32)
    mid = r.astype(bf16)
    lo = (r - mid.astype(f32)).astype(bf16)
    return hi, mid, lo


def _dot_exact_lhs(a01, x):
    a = a01.astype(bf16)
    hi, mid, lo = _split3(x)
    return (jnp.dot(a, hi, preferred_element_type=f32) + jnp.dot(a, mid, preferred_element_type=f32)
            + jnp.dot(a, lo, preferred_element_type=f32))


def _silu(x):
    return x / (1.0 + jnp.exp(-x))


def _sigmoid(x):
    return 1.0 / (1.0 + jnp.exp(-x))


def _rms_rows(x, w):
    return x * lax.rsqrt(jnp.mean(x * x, axis=-1, keepdims=True) + RMS_EPS) * w


def _ffn_kernel(*refs, tm, shift, hist_rows, has_hist, tiles_per_seq, n_chunk):
    if has_hist:
        y_ref, hist_ref, nw_ref, wup_ref, cw_ref, cb_ref, wdn_ref, out_ref, st_ref, ext_ref, g_ref = refs
    else:
        y_ref, nw_ref, wup_ref, cw_ref, cb_ref, wdn_ref, out_ref, st_ref, ext_ref, g_ref = refs
        hist_ref = None
    H = hist_rows
    s = shift
    t = pl.program_id(0) % tiles_per_seq

    @pl.when(t == 0)
    def _():
        if has_hist:
            ext_ref[H - 2 * s:H, :] = hist_ref[...]
        else:
            ext_ref[0:H, :] = jnp.zeros((H, 2 * D_FF), f32)

    @pl.when(t != 0)
    def _():
        ext_ref[H - 2 * s:H, :] = ext_ref[H + tm - 2 * s:H + tm, :]

    y = y_ref[...]
    h = _rms_rows(y, nw_ref[...]).astype(bf16)
    for c in range(2 * D_FF // n_chunk):
        ext_ref[H:H + tm, c * n_chunk:(c + 1) * n_chunk] = jnp.dot(
            h, wup_ref[:, c * n_chunk:(c + 1) * n_chunk], preferred_element_type=f32)
    st_ref[...] = ext_ref[H + tm - 2 * s:H + tm, :]

    gc = 256
    for j in range(D_FF // gc):
        def conv(col):
            w = cw_ref[:, col:col + gc]
            return (ext_ref[H - 2 * s:H - 2 * s + tm, col:col + gc] * w[0:1, :]
                    + ext_ref[H - s:H - s + tm, col:col + gc] * w[1:2, :]
                    + ext_ref[H:H + tm, col:col + gc] * w[2:3, :]
                    + cb_ref[:, col:col + gc])
        a = conv(j * gc)
        b = conv(D_FF + j * gc)
        g_ref[:, j * gc:(j + 1) * gc] = (_silu(a) * b).astype(bf16)
    out_ref[...] = y + jnp.dot(g_ref[...], wdn_ref[...], preferred_element_type=f32)


def _ffn(y, hist, nw, wup, cw, cb, wdn, *, tm, shift, rows_per_seq):
    T = y.shape[0]
    n_seq = T // rows_per_seq
    tiles_per_seq = rows_per_seq // tm
    H = _round_up(2 * shift, V7X_SUBLANES)
    kern = functools.partial(_ffn_kernel, tm=tm, shift=shift, hist_rows=H, has_hist=hist is not None,
                             tiles_per_seq=tiles_per_seq, n_chunk=512)
    const = lambda i: (0, 0)
    in_specs = [pl.BlockSpec((tm, D_MODEL), lambda i: (i, 0))]
    args = [y]
    if hist is not None:
        in_specs.append(pl.BlockSpec((2 * shift, 2 * D_FF), lambda i: (i // tiles_per_seq, 0)))
        args.append(hist)
    in_specs += [pl.BlockSpec((1, D_MODEL), const),
                 pl.BlockSpec((D_MODEL, 2 * D_FF), const, pipeline_mode=pl.Buffered(1)),
                 pl.BlockSpec((FFN_CONV, 2 * D_FF), const),
                 pl.BlockSpec((1, 2 * D_FF), const),
                 pl.BlockSpec((D_FF, D_MODEL), const, pipeline_mode=pl.Buffered(1))]
    args += [nw, wup, cw, cb, wdn]
    out, st = pl.pallas_call(
        kern,
        grid=(T // tm,),
        in_specs=in_specs,
        out_specs=[pl.BlockSpec((tm, D_MODEL), lambda i: (i, 0)),
                   pl.BlockSpec((None, 2 * shift, 2 * D_FF), lambda i: (i // tiles_per_seq, 0, 0))],
        out_shape=[jax.ShapeDtypeStruct((T, D_MODEL), f32),
                   jax.ShapeDtypeStruct((n_seq, 2 * shift, 2 * D_FF), f32)],
        scratch_shapes=[pltpu.VMEM((H + tm, 2 * D_FF), f32), pltpu.VMEM((tm, D_FF), bf16)],
        compiler_params=_cparams("arbitrary"),
        name="conv_ffn",
    )(*args)
    return out, st


def _gdn_in_kernel(*refs, tm, shift, hist_rows, has_hist, tiles_per_seq):
    if has_hist:
        (y_ref, hist_ref, nw_ref, wqkv_ref, wz_ref, wab_ref, cw_ref, alog_ref, dtb_ref,
         q_ref, k_ref, v_ref, z_ref, slab_ref, st_ref, ext_ref) = refs
    else:
        (y_ref, nw_ref, wqkv_ref, wz_ref, wab_ref, cw_ref, alog_ref, dtb_ref,
         q_ref, k_ref, v_ref, z_ref, slab_ref, st_ref, ext_ref) = refs
        hist_ref = None
    H = hist_rows
    s = shift
    nh = GDN_CONV - 1
    W3 = 3 * GDN_WIDTH
    t = pl.program_id(0) % tiles_per_seq

    @pl.when(t == 0)
    def _():
        if has_hist:
            ext_ref[H - nh * s:H, :] = hist_ref[...]
        else:
            ext_ref[0:H, :] = jnp.zeros((H, W3), f32)

    @pl.when(t != 0)
    def _():
        ext_ref[H - nh * s:H, :] = ext_ref[H + tm - nh * s:H + tm, :]

    h = _rms_rows(y_ref[...], nw_ref[...]).astype(bf16)
    nc = 512
    for c in range(W3 // nc):
        ext_ref[H:H + tm, c * nc:(c + 1) * nc] = jnp.dot(h, wqkv_ref[:, c * nc:(c + 1) * nc], preferred_element_type=f32)
    for c in range(GDN_WIDTH // nc):
        z_ref[:, c * nc:(c + 1) * nc] = jnp.dot(h, wz_ref[:, c * nc:(c + 1) * nc], preferred_element_type=f32)
    ab = jnp.dot(h, wab_ref[...], preferred_element_type=f32)
    x = ab + dtb_ref[...]
    softplus = jnp.maximum(x, 0.0) + jnp.log(1.0 + jnp.exp(-jnp.abs(x)))
    g = -jnp.exp(alog_ref[...]) * softplus
    lane = lax.broadcasted_iota(jnp.int32, ab.shape, 1)
    slab_ref[...] = jnp.where(lane < GDN_HEADS, g, _sigmoid(ab))
    st_ref[...] = ext_ref[H + tm - nh * s:H + tm, :]

    hw = GDN_DK
    for j in range(W3 // hw):
        col = j * hw
        w = cw_ref[:, col:col + hw]
        c = ext_ref[H:H + tm, col:col + hw] * w[nh:nh + 1, :]
        for i in range(nh):
            c = c + ext_ref[H - (nh - i) * s:H - (nh - i) * s + tm, col:col + hw] * w[i:i + 1, :]
        c = _silu(c)
        if j < 2 * GDN_HEADS:
            c = c * lax.rsqrt(jnp.sum(c * c, axis=-1, keepdims=True) + RMS_EPS)
        if j < GDN_HEADS:
            q_ref[:, col:col + hw] = c * (GDN_DK ** -0.5)
        elif j < 2 * GDN_HEADS:
            k_ref[:, col - GDN_WIDTH:col - GDN_WIDTH + hw] = c
        else:
            v_ref[:, col - 2 * GDN_WIDTH:col - 2 * GDN_WIDTH + hw] = c


def _gdn_in(y, hist, nw, wqkv, wz, wab, cw, alog, dtb, *, tm, shift, rows_per_seq):
    T = y.shape[0]
    n_seq = T // rows_per_seq
    tiles_per_seq = rows_per_seq // tm
    nh = GDN_CONV - 1
    H = _round_up(nh * shift, V7X_SUBLANES)
    W3 = 3 * GDN_WIDTH
    kern = functools.partial(_gdn_in_kernel, tm=tm, shift=shift, hist_rows=H, has_hist=hist is not None,
                             tiles_per_seq=tiles_per_seq)
    const = lambda i: (0, 0)
    row = lambda i: (i, 0)
    in_specs = [pl.BlockSpec((tm, D_MODEL), row)]
    args = [y]
    if hist is not None:
        in_specs.append(pl.BlockSpec((nh * shift, W3), lambda i: (i // tiles_per_seq, 0)))
        args.append(hist)
    in_specs += [pl.BlockSpec((1, D_MODEL), const),
                 pl.BlockSpec((D_MODEL, W3), const, pipeline_mode=pl.Buffered(1)),
                 pl.BlockSpec((D_MODEL, GDN_WIDTH), const, pipeline_mode=pl.Buffered(1)),
                 pl.BlockSpec((D_MODEL, V7X_LANES), const),
                 pl.BlockSpec((GDN_CONV, W3), const),
                 pl.BlockSpec((1, V7X_LANES), const),
                 pl.BlockSpec((1, V7X_LANES), const)]
    args += [nw, wqkv, wz, wab, cw, alog, dtb]
    wide = pl.BlockSpec((tm, GDN_WIDTH), row)
    return pl.pallas_call(
        kern,
        grid=(T // tm,),
        in_specs=in_specs,
        out_specs=[wide, wide, wide, wide, pl.BlockSpec((tm, V7X_LANES), row),
                   pl.BlockSpec((None, nh * shift, W3), lambda i: (i // tiles_per_seq, 0, 0))],
        out_shape=[jax.ShapeDtypeStruct((T, GDN_WIDTH), f32)] * 4
        + [jax.ShapeDtypeStruct((T, V7X_LANES), f32), jax.ShapeDtypeStruct((n_seq, nh * shift, W3), f32)],
        scratch_shapes=[pltpu.VMEM((H + tm, W3), f32)],
        compiler_params=_cparams("arbitrary"),
        name="gdn_in",
    )(*args)


def _gdn_chunk_kernel(q_ref, k_ref, v_ref, slab_ref, s0_ref, o_ref, s_ref):
    C = GDN_CHUNK
    n = pl.program_id(1)

    @pl.when(n == 0)
    def _():
        s_ref[...] = s0_ref[...]

    ri = lax.broadcasted_iota(jnp.int32, (C, C), 0)
    ci = lax.broadcasted_iota(jnp.int32, (C, C), 1)
    tril = ri >= ci
    strict = ri > ci
    eye = (ri == ci).astype(f32)
    slab = slab_ref[...]
    G = _dot_exact_lhs(tril.astype(f32), slab)
    GT = G.T
    hs = range(GDN_HEADS)
    cols = [slice(h * GDN_DK, (h + 1) * GDN_DK) for h in hs]
    qh = [q_ref[:, c] for c in cols]
    kh = [k_ref[:, c] for c in cols]
    vh = [v_ref[:, c] for c in cols]
    Gc = [G[:, h:h + 1] for h in hs]
    bc = [slab[:, GDN_HEADS + h:GDN_HEADS + h + 1] for h in hs]
    gl = [G[C - 1:C, h:h + 1] for h in hs]
    decay = [jnp.exp(jnp.where(tril, Gc[h] - GT[h:h + 1, :], NEG)) for h in hs]
    kb = [kh[h].astype(bf16) for h in hs]
    A = [jnp.where(strict, bc[h] * _dot_nt(kb[h], kb[h]) * decay[h], 0.0) for h in hs]
    P = [eye - A[h] for h in hs]
    Ap = A
    for _ in range(int(math.log2(C)) - 1):
        Ap = [_dot(Ap[h], Ap[h]) for h in hs]
        P = [P[h] + _dot(P[h], Ap[h]) for h in hs]
    Tb = [P[h].astype(bf16) for h in hs]
    uv = [_dot(Tb[h], vh[h] * bc[h]) for h in hs]
    wk = [_dot(Tb[h], kh[h] * (bc[h] * jnp.exp(Gc[h]))) for h in hs]
    qk = [_dot_nt(qh[h], kb[h]) * decay[h] for h in hs]
    kdec = [kh[h] * jnp.exp(gl[h] - Gc[h]) for h in hs]
    S = [s_ref[h] for h in hs]
    Sb = [S[h].astype(bf16) for h in hs]
    u = [uv[h] - _dot(wk[h], Sb[h]) for h in hs]
    o = [_dot(qh[h] * jnp.exp(Gc[h]), Sb[h]) + _dot(qk[h], u[h]) for h in hs]
    Sn = [S[h] * jnp.exp(gl[h]) + _dot_tn(kdec[h], u[h]) for h in hs]
    for h in hs:
        o_ref[:, cols[h]] = o[h]
        s_ref[h] = Sn[h]


def _gdn_chunk(q, k, v, slab, s0, *, seq_len):
    T = q.shape[0]
    B = T // seq_len
    N = seq_len // GDN_CHUNK
    row = lambda b, n: (b * N + n, 0)
    st = pl.BlockSpec((None, GDN_HEADS, GDN_DK, GDN_DK), lambda b, n: (b, 0, 0, 0))
    wide = pl.BlockSpec((GDN_CHUNK, GDN_WIDTH), row)
    return pl.pallas_call(
        _gdn_chunk_kernel,
        grid=(B, N),
        in_specs=[wide, wide, wide, pl.BlockSpec((GDN_CHUNK, V7X_LANES), row), st],
        out_specs=[wide, st],
        out_shape=[jax.ShapeDtypeStruct((T, GDN_WIDTH), f32),
                   jax.ShapeDtypeStruct((B, GDN_HEADS, GDN_DK, GDN_DK), f32)],
        compiler_params=_cparams("parallel", "arbitrary"),
        name="gdn_chunk",
    )(q, k, v, slab, s0)


def _gdn_out_kernel(y_ref, o_ref, z_ref, nw_ref, wout_ref, out_ref, g_ref):
    for h in range(GDN_HEADS):
        lo = h * GDN_DK
        o = _rms_rows(o_ref[:, lo:lo + GDN_DK], nw_ref[...])
        g_ref[:, lo:lo + GDN_DK] = (o * _silu(z_ref[:, lo:lo + GDN_DK])).astype(bf16)
    out_ref[...] = y_ref[...] + jnp.dot(g_ref[...], wout_ref[...], preferred_element_type=f32)


def _gdn_out(y, o, z, nw, wout, *, tm):
    T = y.shape[0]
    row = lambda i: (i, 0)
    const = lambda i: (0, 0)
    wide = pl.BlockSpec((tm, GDN_WIDTH), row)
    return pl.pallas_call(
        _gdn_out_kernel,
        grid=(T // tm,),
        in_specs=[pl.BlockSpec((tm, D_MODEL), row), wide, wide, pl.BlockSpec((1, GDN_DK), const),
                  pl.BlockSpec((GDN_WIDTH, D_MODEL), const)],
        out_specs=pl.BlockSpec((tm, D_MODEL), row),
        out_shape=jax.ShapeDtypeStruct((T, D_MODEL), f32),
        scratch_shapes=[pltpu.VMEM((tm, GDN_WIDTH), bf16)],
        compiler_params=_cparams("parallel"),
        name="gdn_out",
    )(y, o, z, nw, wout)


NSA_IN_W = Q_WIDTH + 6 * KV_WIDTH + 3 * NSA_HEADS
NSA_IN_WP = _round_up(NSA_IN_W, V7X_LANES)
KV0 = Q_WIDTH
GATE0 = Q_WIDTH + 6 * KV_WIDTH


def _lane_iota(shape):
    return lax.broadcasted_iota(jnp.int32, shape, 1)


def _rope_pair(x, cos, sin):
    half = HEAD_DIM // 2
    lane = _lane_iota(x.shape)
    rot = jnp.where(lane % HEAD_DIM < half, pltpu.roll(x, V7X_LANES - half, 1), pltpu.roll(x, half, 1))
    return x * cos + rot * sin


def _split_pair(x, fill):
    lane = _lane_iota(x.shape)
    return jnp.where(lane < HEAD_DIM, x, fill), jnp.where(lane < HEAD_DIM, pltpu.roll(x, HEAD_DIM, 1), fill)


def _nsa_in_kernel(y_ref, nw_ref, w_ref, cos_ref, sin_ref, oh_ref,
                   qraw_ref, qrot_ref, cmp_ref, sel_ref, win_ref, ksel_ref, vsel_ref, kwin_ref, vwin_ref, gate_ref,
                   proj_ref):
    h = _rms_rows(y_ref[...], nw_ref[...]).astype(bf16)
    nc = NSA_IN_WP // 3
    for c in range(3):
        proj_ref[:, c * nc:(c + 1) * nc] = jnp.dot(h, w_ref[:, c * nc:(c + 1) * nc], preferred_element_type=f32)
    cos = cos_ref[...]
    sin = sin_ref[...]
    oh = oh_ref[...]
    L2 = 2 * V7X_LANES
    for p in range(Q_WIDTH // V7X_LANES):
        x = proj_ref[:, p * V7X_LANES:(p + 1) * V7X_LANES]
        qraw_ref[:, p * V7X_LANES:(p + 1) * V7X_LANES] = x.astype(bf16)
        a, b = _split_pair(_rope_pair(x, cos, sin) * ATTN_SCALE, 0.0)
        qrot_ref[:, p * L2:p * L2 + V7X_LANES] = a.astype(bf16)
        qrot_ref[:, p * L2 + V7X_LANES:(p + 1) * L2] = b.astype(bf16)
    cmp_ref[...] = proj_ref[:, KV0:KV0 + 2 * KV_WIDTH]
    for br, (o_ref, k_ref, v_ref, fill) in enumerate(((sel_ref, ksel_ref, vsel_ref, oh), (win_ref, kwin_ref, vwin_ref, 0.0))):
        base = KV0 + (br + 1) * 2 * KV_WIDTH
        for p in range(KV_WIDTH // V7X_LANES):
            lo = p * V7X_LANES
            kx = _rope_pair(proj_ref[:, base + lo:base + lo + V7X_LANES], cos, sin)
            o_ref[:, lo:lo + V7X_LANES] = kx
            a, b = _split_pair(kx, fill)
            k_ref[:, p * L2:p * L2 + V7X_LANES] = a.astype(bf16)
            k_ref[:, p * L2 + V7X_LANES:(p + 1) * L2] = b.astype(bf16)
            vx = proj_ref[:, base + KV_WIDTH + lo:base + KV_WIDTH + lo + V7X_LANES]
            o_ref[:, KV_WIDTH + lo:KV_WIDTH + lo + V7X_LANES] = vx
            a, b = _split_pair(vx, 1.0)
            v_ref[:, p * L2:p * L2 + V7X_LANES] = a.astype(bf16)
            v_ref[:, p * L2 + V7X_LANES:(p + 1) * L2] = b.astype(bf16)
    gate_ref[...] = _sigmoid(proj_ref[:, GATE0:GATE0 + V7X_LANES])


def _nsa_in(y, nw, w, cos, sin, onehot, *, tm, tiles_per_seq):
    T = y.shape[0]
    row = lambda i: (i, 0)
    const = lambda i: (0, 0)
    tab = lambda i: (i % tiles_per_seq, 0)
    P4 = NSA_KV_HEADS * V7X_LANES
    blk = lambda w_: pl.BlockSpec((tm, w_), row)
    shp = lambda w_, dt: jax.ShapeDtypeStruct((T, w_), dt)
    return pl.pallas_call(
        _nsa_in_kernel,
        grid=(T // tm,),
        in_specs=[blk(D_MODEL), pl.BlockSpec((1, D_MODEL), const),
                  pl.BlockSpec((D_MODEL, NSA_IN_WP), const, pipeline_mode=pl.Buffered(1)),
                  pl.BlockSpec((tm, V7X_LANES), tab), pl.BlockSpec((tm, V7X_LANES), tab),
                  pl.BlockSpec((tm, V7X_LANES), tab)],
        out_specs=[blk(Q_WIDTH), blk(2 * Q_WIDTH), blk(2 * KV_WIDTH), blk(2 * KV_WIDTH), blk(2 * KV_WIDTH),
                   blk(P4), blk(P4), blk(P4), blk(P4), blk(V7X_LANES)],
        out_shape=[shp(Q_WIDTH, bf16), shp(2 * Q_WIDTH, bf16), shp(2 * KV_WIDTH, f32), shp(2 * KV_WIDTH, f32),
                   shp(2 * KV_WIDTH, f32), shp(P4, bf16), shp(P4, bf16), shp(P4, bf16), shp(P4, bf16),
                   shp(V7X_LANES, f32)],
        scratch_shapes=[pltpu.VMEM((tm, NSA_IN_WP), f32)],
        compiler_params=_cparams("parallel"),
        name="nsa_in",
    )(y, nw, w, cos, sin, onehot)


CMP_ROW = NSA_BLOCK * 2 * KV_WIDTH


def _compress_kernel(x_ref, pe_ref, w_ref, o_ref, acc_ref):
    kk = pl.program_id(1)

    @pl.when(kk == 0)
    def _():
        acc_ref[...] = jnp.zeros_like(acc_ref)

    acc_ref[...] += jnp.dot((x_ref[...] + pe_ref[...]).astype(bf16), w_ref[...], preferred_element_type=f32)

    @pl.when(kk == pl.num_programs(1) - 1)
    def _():
        o_ref[...] = acc_ref[...]


def _compress(x, pe_row, wbig, *, tm, tk):
    M = x.shape[0]
    N = 2 * KV_WIDTH
    return pl.pallas_call(
        _compress_kernel,
        grid=(M // tm, CMP_ROW // tk),
        in_specs=[pl.BlockSpec((tm, tk), lambda i, k: (i, k)), pl.BlockSpec((1, tk), lambda i, k: (0, k)),
                  pl.BlockSpec((tk, N), lambda i, k: (k, 0))],
        out_specs=pl.BlockSpec((tm, N), lambda i, k: (i, 0)),
        out_shape=jax.ShapeDtypeStruct((M, N), f32),
        scratch_shapes=[pltpu.VMEM((tm, N), f32)],
        compiler_params=_cparams("parallel", "arbitrary"),
        name="nsa_compress",
    )(x, pe_row, wbig)


def _cmp_select_kernel(q_ref, kbd_ref, vbd_ref, oc_ref, bias_ref, *, tq, nb, n_top):
    i = pl.program_id(1)
    G = NSA_GROUP
    gw = G * HEAD_DIM
    blk = lax.broadcasted_iota(jnp.int32, (nb, tq), 0)
    pos = i * tq + lax.broadcasted_iota(jnp.int32, (nb, tq), 1)
    complete = (blk + 1) * NSA_BLOCK - 1 <= pos
    cur = lax.shift_right_logical(pos, int(math.log2(NSA_BLOCK)))
    cand = blk < cur
    for g in range(NSA_KV_HEADS):
        sT = _dot_nt(kbd_ref[g], q_ref[:, g * gw:(g + 1) * gw]) * ATTN_SCALE
        ps = []
        imp = jnp.zeros((nb, tq), f32)
        for n in range(G):
            sm = jnp.where(complete, sT[n * nb:(n + 1) * nb], NEG)
            mx = jnp.max(sm, axis=0, keepdims=True)
            e = jnp.where(complete, jnp.exp(sm - mx), 0.0)
            den = jnp.sum(e, axis=0, keepdims=True)
            p = e / jnp.where(den > 0.0, den, 1.0)
            ps.append(p)
            imp = imp + p
        oc_ref[:, g * gw:(g + 1) * gw] = _dot_tn(jnp.concatenate(ps, axis=0), vbd_ref[g])
        vals = jnp.where(cand, imp, -1.0)
        cnt = jnp.zeros((nb, tq), jnp.int32)
        for j in range(nb):
            vj = vals[j:j + 1, :]
            cnt = cnt + ((vj > vals) | ((vj == vals) & (blk > j))).astype(jnp.int32)
        sel = (cand & (cnt < n_top)) | (blk == cur)
        biasT = jnp.where(sel, 0.0, NEG)
        full = jnp.concatenate([jnp.zeros((HEAD_DIM, tq), f32), biasT,
                                jnp.zeros((V7X_LANES - HEAD_DIM - nb, tq), f32)], axis=0)
        bias_ref[:, g * V7X_LANES:(g + 1) * V7X_LANES] = full.T.astype(bf16)


def _cmp_select(q_raw, kbd, vbd, *, seq_len, tq):
    T = q_raw.shape[0]
    nq = seq_len // tq
    nb = seq_len // NSA_BLOCK
    row = lambda b, i: (b * nq + i, 0)
    kern = functools.partial(_cmp_select_kernel, tq=tq, nb=nb, n_top=min(N_SEL - 1, nb))
    return pl.pallas_call(
        kern,
        grid=(T // seq_len, nq),
        in_specs=[pl.BlockSpec((tq, Q_WIDTH), row),
                  pl.BlockSpec((None,) + kbd.shape[1:], lambda b, i: (b, 0, 0, 0)),
                  pl.BlockSpec((None,) + vbd.shape[1:], lambda b, i: (b, 0, 0, 0))],
        out_specs=[pl.BlockSpec((tq, Q_WIDTH), row), pl.BlockSpec((tq, NSA_KV_HEADS * V7X_LANES), row)],
        out_shape=[jax.ShapeDtypeStruct((T, Q_WIDTH), f32),
                   jax.ShapeDtypeStruct((T, NSA_KV_HEADS * V7X_LANES), bf16)],
        compiler_params=_cparams("parallel", "parallel"),
        name="nsa_cmp_select",
    )(q_raw, kbd, vbd)


def _attn_kernel(*refs, tq, mode):
    if mode == "sel":
        q_ref, bias_ref, k_ref, v_ref, o_ref, mx_ref, acc_ref = refs
    else:
        q_ref, k_ref, v_ref, o_ref, mx_ref, acc_ref = refs
    i = pl.program_id(2)
    G = NSA_GROUP
    qs = []
    for n in range(G):
        qn = q_ref[:, n * V7X_LANES:(n + 1) * V7X_LANES]
        if mode == "sel":
            qn = qn + bias_ref[...]
        qs.append(qn)
    qa = jnp.concatenate(qs, axis=0)
    qpos = i * tq + lax.broadcasted_iota(jnp.int32, (G * tq, tq), 0) % tq
    kofs = lax.broadcasted_iota(jnp.int32, (G * tq, tq), 1)

    def scores(t, mask_fn):
        start = pl.multiple_of(t * tq, tq)
        s = _dot_nt(qa, k_ref[pl.ds(start, tq), :])
        if mask_fn is not None:
            s = jnp.where(mask_fn(t * tq + kofs), s, NEG)
        return s

    causal = lambda kpos: kpos <= qpos
    in_window = lambda kpos: qpos - kpos < WINDOW
    if mode == "sel":
        tiles = None
    else:
        nback = WINDOW // tq
        tiles = tuple((b, in_window if b == nback else None) for b in range(1, nback + 1))

    mx_ref[...] = scores(i, causal)
    if mode == "sel":
        def max_body(t, c):
            mx_ref[...] = jnp.maximum(mx_ref[...], scores(t, None))
            return c
        lax.fori_loop(0, i, max_body, 0)
    else:
        for back, mask_fn in tiles:
            @pl.when(i >= back)
            def _(back=back, mask_fn=mask_fn):
                mx_ref[...] = jnp.maximum(mx_ref[...], scores(i - back, mask_fn))
    mx_ref[...] = jnp.broadcast_to(jnp.max(mx_ref[...], axis=-1, keepdims=True), mx_ref.shape)

    def accumulate(t, mask_fn, first=False):
        start = pl.multiple_of(t * tq, tq)
        p = jnp.exp(scores(t, mask_fn) - mx_ref[...]).astype(bf16)
        pv = jnp.dot(p, v_ref[pl.ds(start, tq), :], preferred_element_type=f32)
        acc_ref[...] = pv if first else acc_ref[...] + pv

    accumulate(i, causal, first=True)
    if mode == "sel":
        def acc_body(t, c):
            accumulate(t, None)
            return c
        lax.fori_loop(0, i, acc_body, 0)
    else:
        for back, mask_fn in tiles:
            @pl.when(i >= back)
            def _(back=back, mask_fn=mask_fn):
                accumulate(i - back, mask_fn)

    acc = acc_ref[...]
    o = acc / pltpu.roll(acc, HEAD_DIM, 1)
    lane = _lane_iota((tq, V7X_LANES))
    for p in range(G // 2):
        a = o[(2 * p) * tq:(2 * p + 1) * tq]
        b = o[(2 * p + 1) * tq:(2 * p + 2) * tq]
        o_ref[:, p * V7X_LANES:(p + 1) * V7X_LANES] = jnp.where(lane < HEAD_DIM, a, pltpu.roll(b, HEAD_DIM, 1))


def _attn(q_pad, bias, k_pad, v_pad, *, seq_len, tq, mode):
    assert mode in ("sel", "win") and WINDOW % tq == 0
    T = q_pad.shape[0]
    B = T // seq_len
    nq = seq_len // tq
    G = NSA_GROUP
    qrow = lambda b, g, i: (b * nq + i, g)
    kv = lambda b, g, i: (b, g)
    in_specs = [pl.BlockSpec((tq, G * V7X_LANES), qrow)]
    args = [q_pad]
    if mode == "sel":
        in_specs.append(pl.BlockSpec((tq, V7X_LANES), qrow))
        args.append(bias)
    in_specs += [pl.BlockSpec((seq_len, V7X_LANES), kv), pl.BlockSpec((seq_len, V7X_LANES), kv)]
    args += [k_pad, v_pad]
    return pl.pallas_call(
        functools.partial(_attn_kernel, tq=tq, mode=mode),
        grid=(B, NSA_KV_HEADS, nq),
        in_specs=in_specs,
        out_specs=pl.BlockSpec((tq, G * HEAD_DIM), qrow),
        out_shape=jax.ShapeDtypeStruct((T, Q_WIDTH), f32),
        scratch_shapes=[pltpu.VMEM((G * tq, tq), f32), pltpu.VMEM((G * tq, V7X_LANES), f32)],
        compiler_params=_cparams("parallel", "parallel", "parallel"),
        name="nsa_attn_" + mode,
    )(*args)


def _nsa_out_kernel(y_ref, oc_ref, os_ref, ow_ref, gate_ref, e_ref, wout_ref, out_ref):
    g = gate_ref[...]
    ghi = g.astype(bf16)
    glo = (g - ghi.astype(f32)).astype(bf16)
    o = jnp.zeros(oc_ref.shape, f32)
    for j, br in enumerate((oc_ref, os_ref, ow_ref)):
        ge = (jnp.dot(ghi, e_ref[j], preferred_element_type=f32) + jnp.dot(glo, e_ref[j], preferred_element_type=f32))
        o = o + ge * br[...]
    out_ref[...] = y_ref[...] + jnp.dot(o.astype(bf16), wout_ref[...], preferred_element_type=f32)


def _nsa_out(y, oc, osel, ow, gates, expand, wout, *, tm):
    T = y.shape[0]
    row = lambda i: (i, 0)
    wide = pl.BlockSpec((tm, Q_WIDTH), row)
    return pl.pallas_call(
        _nsa_out_kernel,
        grid=(T // tm,),
        in_specs=[pl.BlockSpec((tm, D_MODEL), row), wide, wide, wide, pl.BlockSpec((tm, V7X_LANES), row),
                  pl.BlockSpec((3, V7X_LANES, Q_WIDTH), lambda i: (0, 0, 0)),
                  pl.BlockSpec((Q_WIDTH, D_MODEL), lambda i: (0, 0))],
        out_specs=pl.BlockSpec((tm, D_MODEL), row),
        out_shape=jax.ShapeDtypeStruct((T, D_MODEL), f32),
        compiler_params=_cparams("parallel"),
        name="nsa_out",
    )(y, oc, osel, ow, gates, expand, wout)


def _final_norm_kernel(y_ref, nw_ref, o_ref):
    o_ref[...] = _rms_rows(y_ref[...], nw_ref[...])


def _final_norm(y, nw, *, tm):
    T = y.shape[0]
    return pl.pallas_call(
        _final_norm_kernel,
        grid=(T // tm,),
        in_specs=[pl.BlockSpec((tm, D_MODEL), lambda i: (i, 0)), pl.BlockSpec((1, D_MODEL), lambda i: (0, 0))],
        out_specs=pl.BlockSpec((tm, D_MODEL), lambda i: (i, 0)),
        out_shape=jax.ShapeDtypeStruct((T, D_MODEL), f32),
        compiler_params=_cparams("parallel"),
        name="final_norm",
    )(y, nw)


DEC_ROWS = V7X_SUBLANES


def _page_gather_kernel(pt_ref, tbl_ref, o_ref, *, n_pages):
    b = pl.program_id(0)
    for p in range(n_pages):
        o_ref[p:p + 1, :] = tbl_ref[pl.ds(pt_ref[b, p], 1), :]


def _page_gather(page_table, tbl):
    DB, n_pages = page_table.shape
    n_phys, W = tbl.shape
    return pl.pallas_call(
        functools.partial(_page_gather_kernel, n_pages=n_pages),
        grid_spec=pltpu.PrefetchScalarGridSpec(
            num_scalar_prefetch=1,
            grid=(DB,),
            in_specs=[pl.BlockSpec((n_phys, W), lambda b, pt: (0, 0), pipeline_mode=pl.Buffered(1))],
            out_specs=pl.BlockSpec((None, n_pages, W), lambda b, pt: (b, 0, 0)),
        ),
        out_shape=jax.ShapeDtypeStruct((DB, n_pages, W), tbl.dtype),
        compiler_params=_cparams("arbitrary"),
        name="nsa_page_gather",
    )(page_table, tbl)


PAGE_BLOCKS = PAGE_SIZE // NSA_BLOCK
PAGE_ROWS = 2 * KV_WIDTH


def _compress_pages_kernel(x_ref, pe_ref, w_ref, o_ref, *, pages):
    M = pages * PAGE_ROWS // HEAD_DIM
    acc = jnp.zeros((M, 2 * V7X_LANES), f32)
    for d in range(HEAD_DIM):
        a = x_ref[pl.ds(d, M, stride=HEAD_DIM), :].reshape(pages, M // pages, V7X_LANES) + pe_ref[d]
        acc = acc + jnp.dot(a.reshape(M, V7X_LANES).astype(bf16), w_ref[d], preferred_element_type=f32)
    is_k = lax.broadcasted_iota(jnp.int32, (M, V7X_LANES), 0) % (2 * NSA_KV_HEADS) < NSA_KV_HEADS
    o_ref[...] = jnp.where(is_k, acc[:, :V7X_LANES], acc[:, V7X_LANES:])


def _compress_pages(cache2d, pe8, w2, *, layer, n_phys, pages):
    assert n_phys % pages == 0
    steps = n_phys // pages
    M = pages * PAGE_ROWS // HEAD_DIM
    return pl.pallas_call(
        functools.partial(_compress_pages_kernel, pages=pages),
        grid=(steps,),
        in_specs=[pl.BlockSpec((pages * PAGE_ROWS, PAGE_SIZE), lambda i: (layer * steps + i, 0)),
                  pl.BlockSpec(pe8.shape, lambda i: (0, 0, 0)),
                  pl.BlockSpec(w2.shape, lambda i: (0, 0, 0), pipeline_mode=pl.Buffered(1))],
        out_specs=pl.BlockSpec((M, V7X_LANES), lambda i: (i, 0)),
        out_shape=jax.ShapeDtypeStruct((steps * M, V7X_LANES), f32),
        compiler_params=_cparams("parallel"),
        name="nsa_compress_pages",
    )(cache2d, pe8, w2)


def _cmp_decode_kernel(q_ref, ckv_ref, oc_ref, bias_ref, *, past, n_top):
    R = DEC_ROWS
    n_pages = ckv_ref.shape[0]
    nb = PAGE_BLOCKS * n_pages
    KVH, G = NSA_KV_HEADS, NSA_GROUP
    lane = lax.broadcasted_iota(jnp.int32, (R, nb), 1)
    blk = PAGE_BLOCKS * (lane % n_pages) + lane // n_pages
    pos = past + lax.broadcasted_iota(jnp.int32, (R, nb), 0)
    complete = (blk + 1) * NSA_BLOCK - 1 <= pos
    cand = blk < lax.shift_right_logical(pos, int(math.log2(NSA_BLOCK)))

    def blocks(c, g):
        lo = (c * KVH + g) * PAGE_BLOCKS * HEAD_DIM
        return jnp.concatenate([ckv_ref[:, lo + h * HEAD_DIM:lo + (h + 1) * HEAD_DIM] for h in range(PAGE_BLOCKS)],
                               axis=0).astype(bf16)

    ck = [blocks(0, g) for g in range(KVH)]
    cv = [blocks(1, g) for g in range(KVH)]
    heads = [(g, n) for g in range(KVH) for n in range(G)]
    qh = [q_ref[:, (g * G + n) * HEAD_DIM:(g * G + n + 1) * HEAD_DIM] for g, n in heads]
    sm = [jnp.where(complete, _dot_nt(qh[i], ck[g]) * ATTN_SCALE, NEG) for i, (g, n) in enumerate(heads)]
    e = [jnp.where(complete, jnp.exp(s - jnp.max(s, axis=-1, keepdims=True)), 0.0) for s in sm]
    den = [jnp.sum(x, axis=-1, keepdims=True) for x in e]
    p = [x / jnp.where(d > 0.0, d, 1.0) for x, d in zip(e, den)]
    for i, (g, n) in enumerate(heads):
        lo = (g * G + n) * HEAD_DIM
        oc_ref[:, lo:lo + HEAD_DIM] = _dot(p[i], cv[g])
    imp = jnp.concatenate([sum(p[g * G:(g + 1) * G]) for g in range(KVH)], axis=0)
    cand4 = jnp.concatenate([cand] * KVH, axis=0)
    blk4 = jnp.concatenate([blk] * KVH, axis=0)
    vals = jnp.where(cand4, imp, -1.0)
    cnts = [jnp.zeros(vals.shape, jnp.int32) for _ in range(4)]
    for s in range(1, nb):
        other = pltpu.roll(vals, s, 1)
        other_blk = pltpu.roll(blk4, s, 1)
        cnts[s % 4] = cnts[s % 4] + ((other > vals) | ((other == vals) & (other_blk < blk4))).astype(jnp.int32)
    cnt = (cnts[0] + cnts[1]) + (cnts[2] + cnts[3])
    bias = jnp.where(cand4 & (cnt < n_top), 0.0, NEG)
    for g in range(KVH):
        bias_ref[g] = bias[g * R:(g + 1) * R]


def _cmp_decode(q_raw, ckv, *, past):
    DB, n_pages = ckv.shape[:2]
    nb = PAGE_BLOCKS * n_pages
    assert nb == V7X_LANES
    return pl.pallas_call(
        functools.partial(_cmp_decode_kernel, past=past, n_top=min(N_SEL - 1, nb)),
        grid=(DB,),
        in_specs=[pl.BlockSpec((None, DEC_ROWS, Q_WIDTH), lambda b: (b, 0, 0)),
                  pl.BlockSpec((None,) + ckv.shape[1:], lambda b: (b, 0, 0))],
        out_specs=[pl.BlockSpec((None, DEC_ROWS, Q_WIDTH), lambda b: (b, 0, 0)),
                   pl.BlockSpec((None, NSA_KV_HEADS, DEC_ROWS, nb), lambda b: (b, 0, 0, 0))],
        out_shape=[jax.ShapeDtypeStruct((DB, DEC_ROWS, Q_WIDTH), f32),
                   jax.ShapeDtypeStruct((DB, NSA_KV_HEADS, DEC_ROWS, nb), f32)],
        compiler_params=_cparams("parallel"),
        name="nsa_cmp_decode",
    )(q_raw, ckv)


SEL_PAGES_PER_STEP = 8


def _sel_decode_kernel(pt_ref, q_ref, bias_ref, new_ref, pool_ref, o_ref, buf_ref, sem_ref, m_ref, l_ref, acc_ref,
                       *, layer, n_seq, n_pages, n_new):
    PG = SEL_PAGES_PER_STEP
    NG = n_pages // PG
    R = NSA_GROUP * DEC_ROWS
    keys = PG * PAGE_SIZE

    def copies(step, slot):
        b = step // NG
        j = step % NG
        return [pltpu.make_async_copy(pool_ref.at[layer, pt_ref[b, j * PG + p]], buf_ref.at[slot, p],
                                      sem_ref.at[slot]) for p in range(PG)]

    KVH = NSA_KV_HEADS

    def online(s, pv):
        m_old = [m_ref[g] for g in range(KVH)]
        m_new = [jnp.maximum(m_old[g], jnp.max(s[g], axis=-1, keepdims=True)) for g in range(KVH)]
        alpha = [jnp.exp(m_old[g] - m_new[g]) for g in range(KVH)]
        p = [jnp.exp(s[g] - m_new[g]) for g in range(KVH)]
        psum = [jnp.sum(p[g], axis=-1, keepdims=True) for g in range(KVH)]
        pvs = [pv[g](p[g]) for g in range(KVH)]
        for g in range(KVH):
            l_ref[g] = alpha[g] * l_ref[g] + psum[g]
            acc_ref[g] = alpha[g] * acc_ref[g] + pvs[g]
            m_ref[g] = m_new[g]

    for c in copies(0, 0):
        c.start()

    def body(step, carry):
        slot = step % 2
        b = step // NG
        j = step % NG

        @pl.when(step + 1 < n_seq * NG)
        def _():
            for c in copies(step + 1, 1 - slot):
                c.start()

        @pl.when(j == 0)
        def _():
            m_ref[...] = jnp.full(m_ref.shape, NEG, f32)
            l_ref[...] = jnp.zeros(l_ref.shape, f32)
            acc_ref[...] = jnp.zeros(acc_ref.shape, f32)
            tok = lax.broadcasted_iota(jnp.int32, (R, DEC_ROWS), 0) % DEC_ROWS
            key = lax.broadcasted_iota(jnp.int32, (R, DEC_ROWS), 1)
            causal = (key <= tok) & (key < n_new)
            qn = [q_ref[b, g][:, :HEAD_DIM] for g in range(KVH)]
            kn = [new_ref[b, :, g * HEAD_DIM:(g + 1) * HEAD_DIM] for g in range(KVH)]
            vn = [new_ref[b, :, KV_WIDTH + g * HEAD_DIM:KV_WIDTH + (g + 1) * HEAD_DIM] for g in range(KVH)]
            online([jnp.where(causal, _dot_nt(qn[g], kn[g]), NEG) for g in range(KVH)],
                   [functools.partial(lambda p, v: _dot(p, v), v=vn[g]) for g in range(KVH)])

        for c in copies(step, slot):
            c.wait()
        lane = lax.broadcasted_iota(jnp.int32, (V7X_LANES, keys), 0)
        key = lax.broadcasted_iota(jnp.int32, (V7X_LANES, keys), 1)
        kpage = j * PG + lax.shift_right_logical(key, int(math.log2(PAGE_SIZE)))
        khalf = lax.shift_right_logical(key, int(math.log2(NSA_BLOCK))) % PAGE_BLOCKS
        expand = (lane == khalf * n_pages + kpage).astype(bf16)
        qg = [q_ref[b, g][:, :HEAD_DIM] for g in range(KVH)]
        kt = [jnp.concatenate([buf_ref[slot, p, 0, g] for p in range(PG)], axis=1).astype(bf16) for g in range(KVH)]
        vt = [jnp.concatenate([buf_ref[slot, p, 1, g] for p in range(PG)], axis=1).astype(bf16) for g in range(KVH)]
        bias = [jnp.concatenate([bias_ref[b, g]] * NSA_GROUP, axis=0).astype(bf16) for g in range(KVH)]
        s = [_dot(qg[g], kt[g]) + jnp.dot(bias[g], expand, preferred_element_type=f32) for g in range(KVH)]
        online(s, [functools.partial(lambda p, v: _dot_nt(p, v), v=vt[g]) for g in range(KVH)])

        @pl.when(j == NG - 1)
        def _():
            o = [acc_ref[g] / l_ref[g] for g in range(KVH)]
            for g in range(KVH):
                for n in range(NSA_GROUP):
                    lo = (g * NSA_GROUP + n) * HEAD_DIM
                    o_ref[b, :, lo:lo + HEAD_DIM] = o[g][n * DEC_ROWS:(n + 1) * DEC_ROWS]

        return carry

    lax.fori_loop(0, n_seq * NG, body, 0)


def _sel_decode(page_table, q, bias, new_kv, pool, *, layer, n_new):
    DB, n_pages = page_table.shape
    assert n_pages % SEL_PAGES_PER_STEP == 0 and n_pages * PAGE_BLOCKS == V7X_LANES
    R = NSA_GROUP * DEC_ROWS
    whole = lambda shape: pl.BlockSpec(shape, lambda i, pt, _n=len(shape): (0,) * _n)
    return pl.pallas_call(
        functools.partial(_sel_decode_kernel, layer=layer, n_seq=DB, n_pages=n_pages, n_new=n_new),
        grid_spec=pltpu.PrefetchScalarGridSpec(
            num_scalar_prefetch=1,
            grid=(1,),
            in_specs=[whole(q.shape), whole(bias.shape), whole(new_kv.shape), pl.BlockSpec(memory_space=pl.ANY)],
            out_specs=whole((DB, DEC_ROWS, Q_WIDTH)),
            scratch_shapes=[pltpu.VMEM((2, SEL_PAGES_PER_STEP) + pool.shape[2:], f32),
                            pltpu.SemaphoreType.DMA((2,)),
                            pltpu.VMEM((NSA_KV_HEADS, R, 1), f32), pltpu.VMEM((NSA_KV_HEADS, R, 1), f32),
                            pltpu.VMEM((NSA_KV_HEADS, R, HEAD_DIM), f32)],
        ),
        out_shape=jax.ShapeDtypeStruct((DB, DEC_ROWS, Q_WIDTH), f32),
        compiler_params=_cparams("arbitrary"),
        name="nsa_sel_decode",
    )(page_table, q, bias, new_kv, pool)


def _win_decode_kernel(q_ref, buf_ref, new_ref, o_ref, *, n_new):
    wbuf = buf_ref.shape[-1]
    R = NSA_GROUP * DEC_ROWS
    KVH = NSA_KV_HEADS
    tok = lax.broadcasted_iota(jnp.int32, (R, wbuf), 0) % DEC_ROWS
    dist = wbuf + tok - lax.broadcasted_iota(jnp.int32, (R, wbuf), 1)
    in_buf = (dist >= 0) & (dist < WINDOW)
    tokn = lax.broadcasted_iota(jnp.int32, (R, DEC_ROWS), 0) % DEC_ROWS
    keyn = lax.broadcasted_iota(jnp.int32, (R, DEC_ROWS), 1)
    in_new = (keyn <= tokn) & (keyn < n_new)
    hs = range(KVH)
    q = [q_ref[g][:, :HEAD_DIM] for g in hs]
    kn = [new_ref[:, g * HEAD_DIM:(g + 1) * HEAD_DIM] for g in hs]
    vn = [new_ref[:, KV_WIDTH + g * HEAD_DIM:KV_WIDTH + (g + 1) * HEAD_DIM] for g in hs]
    s1 = [jnp.where(in_buf, _dot(q[g], buf_ref[0, g]), NEG) for g in hs]
    s2 = [jnp.where(in_new, _dot_nt(q[g], kn[g]), NEG) for g in hs]
    m = [jnp.maximum(jnp.max(s1[g], axis=-1, keepdims=True), jnp.max(s2[g], axis=-1, keepdims=True)) for g in hs]
    p1 = [jnp.exp(s1[g] - m[g]) for g in hs]
    p2 = [jnp.exp(s2[g] - m[g]) for g in hs]
    den = [jnp.sum(p1[g], axis=-1, keepdims=True) + jnp.sum(p2[g], axis=-1, keepdims=True) for g in hs]
    o = [(_dot_nt(p1[g], buf_ref[1, g]) + _dot(p2[g], vn[g])) / den[g] for g in hs]
    for g in hs:
        for n in range(NSA_GROUP):
            lo = (g * NSA_GROUP + n) * HEAD_DIM
            o_ref[:, lo:lo + HEAD_DIM] = o[g][n * DEC_ROWS:(n + 1) * DEC_ROWS]


def _win_decode(q, win_t, new_kv, *, layer, n_new):
    DB = q.shape[0]
    R = NSA_GROUP * DEC_ROWS
    return pl.pallas_call(
        functools.partial(_win_decode_kernel, n_new=n_new),
        grid=(DB,),
        in_specs=[pl.BlockSpec((None, NSA_KV_HEADS, R, V7X_LANES), lambda b: (b, 0, 0, 0)),
                  pl.BlockSpec((None, None) + win_t.shape[2:], lambda b: (layer, b, 0, 0, 0, 0)),
                  pl.BlockSpec((None, DEC_ROWS, 2 * KV_WIDTH), lambda b: (b, 0, 0))],
        out_specs=pl.BlockSpec((None, DEC_ROWS, Q_WIDTH), lambda b: (b, 0, 0)),
        out_shape=jax.ShapeDtypeStruct((DB, DEC_ROWS, Q_WIDTH), f32),
        compiler_params=_cparams("parallel"),
        name="nsa_win_decode",
    )(q, win_t, new_kv)


def _rope_tables(pos):
    half = HEAD_DIM // 2
    inv = ROPE_THETA ** (-jnp.arange(half, dtype=f32) / half)
    ang = pos.astype(f32)[:, None] * inv[None, :]
    cos, sin = jnp.cos(ang), jnp.sin(ang)
    reps = V7X_LANES // HEAD_DIM
    return jnp.tile(cos, (1, 2 * reps)), jnp.tile(jnp.concatenate([-sin, sin], axis=1), (1, reps))


def _block_onehot(pos):
    return (jnp.arange(V7X_LANES, dtype=jnp.int32)[None, :] == HEAD_DIM + pos[:, None] // NSA_BLOCK).astype(f32)


def _gate_expand():
    src = jnp.arange(V7X_LANES)[None, :, None]
    head = jnp.arange(Q_WIDTH)[None, None, :] // HEAD_DIM
    j = jnp.arange(3)[:, None, None]
    return (src == head * 3 + j).astype(bf16)


def _compress_weights(pe, w):
    wb = jnp.einsum("lcde,cx,gy->lcgdxye", w, jnp.eye(2, dtype=f32), jnp.eye(NSA_KV_HEADS, dtype=f32))
    wb = wb.reshape(CMP_ROW, 2 * KV_WIDTH).astype(bf16)
    pe_row = jnp.broadcast_to(pe[:, :, None, :], (NSA_BLOCK, 2, NSA_KV_HEADS, HEAD_DIM)).reshape(1, CMP_ROW)
    return pe_row, wb


def _compress_page_weights(pe, w):
    w2 = jnp.einsum("lcde,hx->dhlcxe", w, jnp.eye(PAGE_BLOCKS, dtype=f32))
    w2 = w2.reshape(HEAD_DIM, PAGE_SIZE, 2 * PAGE_BLOCKS * HEAD_DIM).astype(bf16)
    pe8 = jnp.broadcast_to(pe.transpose(2, 1, 0)[:, :, None, None, :],
                           (HEAD_DIM, 2, NSA_KV_HEADS, PAGE_BLOCKS, NSA_BLOCK))
    return pe8.reshape(HEAD_DIM, 2 * NSA_KV_HEADS, PAGE_SIZE), w2


def _token_minor(cache):
    n = cache.ndim
    return jnp.transpose(cache, tuple(range(n - 4)) + (n - 3, n - 2, n - 1, n - 4))


def _time_major(x):
    return jnp.swapaxes(x, 0, 1).reshape((x.shape[0] * x.shape[1],) + x.shape[2:])


def _batch_major(x, db):
    return jnp.swapaxes(x.reshape((x.shape[0] // db, db) + x.shape[1:]), 0, 1)


def _gdn_weights(w_in, a_log, dt_bias):
    W3 = 3 * GDN_WIDTH
    pad = V7X_LANES - 2 * GDN_HEADS
    return (w_in[:, :W3].astype(bf16), w_in[:, W3:W3 + GDN_WIDTH].astype(bf16),
            jnp.pad(w_in[:, W3 + GDN_WIDTH:], ((0, 0), (0, pad))).astype(bf16),
            jnp.pad(a_log, (0, V7X_LANES - GDN_HEADS))[None], jnp.pad(dt_bias, (0, V7X_LANES - GDN_HEADS))[None])


def _gdn_layer(yp, ys, S0s, conv_s, nw, w_in, conv_w, a_log, dt_bias, norm_w, w_out, *, B, L, DB, DS):
    wqkv, wz, wab, alog, dtb = _gdn_weights(w_in, a_log, dt_bias)
    wout = w_out.astype(bf16)
    nw = nw[None]
    norm_w = norm_w[None]
    q, k, v, z, slab, conv_p = _gdn_in(yp, None, nw, wqkv, wz, wab, conv_w, alog, dtb, tm=256, shift=1, rows_per_seq=L)
    o, S_p = _gdn_chunk(q, k, v, slab, jnp.zeros((B, GDN_HEADS, GDN_DK, GDN_DK), f32), seq_len=L)
    yp = _gdn_out(yp, o, z, norm_w, wout, tm=512)
    Ts = DS * DB
    hist = _time_major(conv_s).reshape((GDN_CONV - 1) * DB, 3 * GDN_WIDTH)
    q, k, v, z, slab, conv_tm = _gdn_in(ys, hist, nw, wqkv, wz, wab, conv_w, alog, dtb, tm=Ts, shift=DB, rows_per_seq=Ts)

    def chunked(x):
        x = _batch_major(x, DB)
        return jnp.pad(x, ((0, 0), (0, GDN_CHUNK - DS), (0, 0))).reshape(DB * GDN_CHUNK, x.shape[-1])

    o, S_s = _gdn_chunk(chunked(q), chunked(k), chunked(v), chunked(slab), S0s, seq_len=GDN_CHUNK)
    o = _time_major(o.reshape(DB, GDN_CHUNK, GDN_WIDTH)[:, :DS])
    ys = _gdn_out(ys, o, z, norm_w, wout, tm=Ts)
    conv_s_new = _batch_major(conv_tm.reshape((GDN_CONV - 1) * DB, 3 * GDN_WIDTH), DB)
    return yp, ys, S_p, S_s, conv_p, conv_s_new


def _ffn_layer(yp, ys, conv_s, nw, w_up, conv_w, conv_b, w_down, *, B, L, DB, DS):
    wup = w_up.astype(bf16)
    wdn = w_down.astype(bf16)
    nw = nw[None]
    cb = conv_b[None]
    yp, st_p = _ffn(yp, None, nw, wup, conv_w, cb, wdn, tm=256, shift=1, rows_per_seq=L)
    Ts = DS * DB
    hist = _time_major(conv_s).reshape((FFN_CONV - 1) * DB, 2 * D_FF)
    ys, st_tm = _ffn(ys, hist, nw, wup, conv_w, cb, wdn, tm=Ts, shift=DB, rows_per_seq=Ts)
    st_s = _batch_major(st_tm.reshape((FFN_CONV - 1) * DB, 2 * D_FF), DB)
    return yp, ys, st_p, st_s


def _nsa_layer(yp, ys, j, cmp_t, sel_t, win_t, win_buf, page_table, nw, w_in, cmp_pe, cmp_w, w_out, *, B, L, DB, DS):
    past = page_table.shape[1] * PAGE_SIZE
    w = jnp.pad(w_in, ((0, 0), (0, NSA_IN_WP - NSA_IN_W))).astype(bf16)
    wout = w_out.astype(bf16)
    nw = nw[None]
    expand = _gate_expand()
    pe_row, wbig = _compress_weights(cmp_pe, cmp_w)
    tq = WINDOW // 2
    tq_attn = WINDOW
    pos_p = jnp.arange(L, dtype=jnp.int32)
    cos, sin = _rope_tables(pos_p)
    (q_raw, q_rot, kv_cmp, kv_sel, kv_win, ksel, vsel, kwin, vwin, gates) = _nsa_in(
        yp, nw, w, cos, sin, _block_onehot(pos_p), tm=256, tiles_per_seq=L // 256)
    nb = L // NSA_BLOCK
    ckv = _compress(kv_cmp.reshape(B * nb, CMP_ROW), pe_row, wbig, tm=B * nb, tk=2048)
    ck = ckv.reshape(B, nb, 2, NSA_KV_HEADS, HEAD_DIM)
    eye = jnp.eye(NSA_GROUP, dtype=f32)
    kbd = jnp.einsum("bkgd,nm->bgnkmd", ck[:, :, 0], eye).reshape(B, NSA_KV_HEADS, NSA_GROUP * nb, NSA_GROUP * HEAD_DIM)
    vbd = jnp.einsum("bkgd,nm->bgnkmd", ck[:, :, 1], eye).reshape(B, NSA_KV_HEADS, NSA_GROUP * nb, NSA_GROUP * HEAD_DIM)
    o_c, bias = _cmp_select(q_raw, kbd.astype(bf16), vbd.astype(bf16), seq_len=L, tq=tq)
    o_s = _attn(q_rot, bias, ksel, vsel, seq_len=L, tq=tq_attn, mode="sel")
    o_w = _attn(q_rot, None, kwin, vwin, seq_len=L, tq=tq_attn, mode="win")
    yp = _nsa_out(yp, o_c, o_s, o_w, gates, expand, wout, tm=512)
    shp = (B, L, 2, NSA_KV_HEADS, HEAD_DIM)
    cmp_p, sel_p, win_p = kv_cmp.reshape(shp), kv_sel.reshape(shp), kv_win.reshape(shp)[:, -min(WINDOW, L):]
    Ts = DS * DB
    pos_s = past + jnp.arange(DS, dtype=jnp.int32)
    pos_rows = jnp.repeat(pos_s, DB)
    cos, sin = _rope_tables(pos_rows)
    (q_raw, q_rot, kv_cmp, kv_sel, kv_win, _, _, _, _, gates) = _nsa_in(
        ys, nw, w, cos, sin, jnp.zeros((Ts, V7X_LANES), f32), tm=Ts, tiles_per_seq=1)
    n_layers, n_phys = cmp_t.shape[:2]
    pe8, w2 = _compress_page_weights(cmp_pe, cmp_w)
    ckv_all = _compress_pages(cmp_t.reshape(n_layers * n_phys * PAGE_ROWS, PAGE_SIZE), pe8, w2, layer=j,
                              n_phys=n_phys, pages=_pick_tile(n_phys, 16))
    ckv = _page_gather(page_table, ckv_all.reshape(n_phys, 2 * KV_WIDTH * PAGE_BLOCKS))

    def rows8(x):
        return jnp.pad(_batch_major(x, DB), ((0, 0), (0, DEC_ROWS - DS), (0, 0)))

    o_c, bias = _cmp_decode(rows8(q_raw), ckv, past=past)
    q4 = rows8(q_rot).reshape(DB, DEC_ROWS, NSA_KV_HEADS, NSA_GROUP, V7X_LANES)
    q4 = q4.transpose(0, 2, 3, 1, 4).reshape(DB, NSA_KV_HEADS, NSA_GROUP * DEC_ROWS, V7X_LANES)
    o_s = _sel_decode(page_table, q4, bias, rows8(kv_sel), sel_t, layer=j, n_new=DS)
    o_w = _win_decode(q4, win_t, rows8(kv_win), layer=j, n_new=DS)
    ks = (DB, DS, 2, NSA_KV_HEADS, HEAD_DIM)
    cmp_s, sel_s, win_new = (_batch_major(t, DB).reshape(ks) for t in (kv_cmp, kv_sel, kv_win))
    win_s = jnp.concatenate([win_buf[:, DS:], win_new], axis=1)
    tm_rows = lambda o: _time_major(o[:, :DS])
    ys = _nsa_out(ys, tm_rows(o_c), tm_rows(o_s), tm_rows(o_w), gates, expand, wout, tm=Ts)
    return yp, ys, cmp_p, cmp_s, sel_p, sel_s, win_p, win_s


def kernel(x_prompt, x_sample, state_gdn, state_gdn_conv, cache_cmp, cache_sel, state_win, state_ffn_conv, page_table,
           norm_mix, norm_ffn, norm_final, gdn_w_in, gdn_conv_w, gdn_a_log, gdn_dt_bias, gdn_norm_w, gdn_w_out,
           nsa_w_in, nsa_cmp_pe, nsa_cmp_w, nsa_w_out, ffn_w_up, ffn_conv_w, ffn_conv_b, ffn_w_down):
    B, L, _ = x_prompt.shape
    DB, DS, _ = x_sample.shape
    depth = norm_mix.shape[0]
    dims = dict(B=B, L=L, DB=DB, DS=DS)
    yp = x_prompt.reshape(B * L, D_MODEL)
    ys = _time_major(x_sample)
    outs = [[] for _ in range(12)]
    cmp_t, sel_t, win_t = _token_minor(cache_cmp), _token_minor(cache_sel), _token_minor(state_win)
    for i in range(depth):
        j = i // 2
        if i % 2 == 0:
            yp, ys, S_p, S_s, c_p, c_s = _gdn_layer(yp, ys, state_gdn[j], state_gdn_conv[j], norm_mix[i], gdn_w_in[j],
                                                    gdn_conv_w[j], gdn_a_log[j], gdn_dt_bias[j], gdn_norm_w[j],
                                                    gdn_w_out[j], **dims)
            for lst, val in zip(outs[0:4], (S_p, S_s, c_p, c_s)):
                lst.append(val)
        else:
            yp, ys, *kv = _nsa_layer(yp, ys, j, cmp_t, sel_t, win_t, state_win[j], page_table, norm_mix[i],
                                     nsa_w_in[j], nsa_cmp_pe[j], nsa_cmp_w[j], nsa_w_out[j], **dims)
            for lst, val in zip(outs[4:10], kv):
                lst.append(val)
        yp, ys, f_p, f_s = _ffn_layer(yp, ys, state_ffn_conv[i], norm_ffn[i], ffn_w_up[i], ffn_conv_w[i],
                                      ffn_conv_b[i], ffn_w_down[i], **dims)
        outs[10].append(f_p)
        outs[11].append(f_s)
    y_prompt = _final_norm(yp, norm_final[None], tm=512).reshape(B, L, D_MODEL)
    y_sample = _batch_major(_final_norm(ys, norm_final[None], tm=DS * DB), DB)
    return (y_prompt, y_sample) + tuple(jnp.stack(o) for o in outs)
```

```python
import functools
import math

import jax
import jax.numpy as jnp
from jax import lax
from jax.experimental import pallas as pl
from jax.experimental.pallas import tpu as pltpu

f32 = jnp.float32
bf16 = jnp.bfloat16

D_MODEL = 1024
GDN_HEADS = 8
GDN_DK = 128
GDN_WIDTH = GDN_HEADS * GDN_DK
GDN_CONV = 4
GDN_CHUNK = 64
NSA_HEADS = 16
NSA_KV_HEADS = 4
NSA_GROUP = 4
HEAD_DIM = 64
Q_WIDTH = NSA_HEADS * HEAD_DIM
KV_WIDTH = NSA_KV_HEADS * HEAD_DIM
NSA_BLOCK = 64
N_SEL = 16
WINDOW = 512
PAGE_SIZE = 128
ROPE_THETA = 10000.0
ATTN_SCALE = HEAD_DIM ** -0.5
LOG2E = math.log2(math.e)
D_FF = 2816
FFN_CONV = 3
RMS_EPS = 1e-6
NEG = -1e30

V7X_LANES = 128
V7X_SUBLANES = 8
V7X_VMEM_BYTES = 64 * 1024 * 1024
VMEM_LIMIT = 56 * 1024 * 1024


def _cparams(*sem):
    return pltpu.CompilerParams(dimension_semantics=sem, vmem_limit_bytes=VMEM_LIMIT)


def _round_up(x, m):
    return (x + m - 1) // m * m


def _pick_tile(n, cap):
    best = None
    for t in range(V7X_SUBLANES, min(n, cap) + 1, V7X_SUBLANES):
        if n % t == 0:
            best = t
    assert best is not None, (n, cap)
    return best


def _dot(a, b):
    return jnp.dot(a.astype(bf16), b.astype(bf16), preferred_element_type=f32)


def _dot_nt(a, b):
    return lax.dot_general(a.astype(bf16), b.astype(bf16), (((1,), (1,)), ((), ())), preferred_element_type=f32)


def _dot_tn(a, b):
    return lax.dot_general(a.astype(bf16), b.astype(bf16), (((0,), (0,)), ((), ())), preferred_element_type=f32)


def _split3(x):
    hi = x.astype(bf16)
    r = x - hi.astype(f32)
    mid = r.astype(bf16)
    lo = (r - mid.astype(f32)).astype(bf16)
    return hi, mid, lo


def _dot_exact_lhs(a01, x):
    a = a01.astype(bf16)
    hi, mid, lo = _split3(x)
    return (jnp.dot(a, hi, preferred_element_type=f32) + jnp.dot(a, mid, preferred_element_type=f32)
            + jnp.dot(a, lo, preferred_element_type=f32))


def _silu(x):
    return x / (1.0 + jnp.exp(-x))


def _sigmoid(x):
    return 1.0 / (1.0 + jnp.exp(-x))


def _rms_rows(x, w):
    return x * lax.rsqrt(jnp.mean(x * x, axis=-1, keepdims=True) + RMS_EPS) * w


def _ffn_kernel(*refs, tm, shift, hist_rows, has_hist, tiles_per_seq, n_chunk):
    if has_hist:
        y_ref, hist_ref, nw_ref, wup_ref, cw_ref, cb_ref, wdn_ref, out_ref, st_ref, ext_ref, g_ref = refs
    else:
        y_ref, nw_ref, wup_ref, cw_ref, cb_ref, wdn_ref, out_ref, st_ref, ext_ref, g_ref = refs
        hist_ref = None
    H = hist_rows
    s = shift
    t = pl.program_id(0) % tiles_per_seq

    @pl.when(t == 0)
    def _():
        if has_hist:
            ext_ref[H - 2 * s:H, :] = hist_ref[...]
        else:
            ext_ref[0:H, :] = jnp.zeros((H, 2 * D_FF), f32)

    @pl.when(t != 0)
    def _():
        ext_ref[H - 2 * s:H, :] = ext_ref[H + tm - 2 * s:H + tm, :]

    y = y_ref[...]
    h = _rms_rows(y, nw_ref[...]).astype(bf16)
    for c in range(2 * D_FF // n_chunk):
        ext_ref[H:H + tm, c * n_chunk:(c + 1) * n_chunk] = jnp.dot(
            h, wup_ref[:, c * n_chunk:(c + 1) * n_chunk], preferred_element_type=f32)
    st_ref[...] = ext_ref[H + tm - 2 * s:H + tm, :]

    gc = 256
    for j in range(D_FF // gc):
        def conv(col):
            w = cw_ref[:, col:col + gc]
            return (ext_ref[H - 2 * s:H - 2 * s + tm, col:col + gc] * w[0:1, :]
                    + ext_ref[H - s:H - s + tm, col:col + gc] * w[1:2, :]
                    + ext_ref[H:H + tm, col:col + gc] * w[2:3, :]
                    + cb_ref[:, col:col + gc])
        a = conv(j * gc)
        b = conv(D_FF + j * gc)
        g_ref[:, j * gc:(j + 1) * gc] = (_silu(a) * b).astype(bf16)
    out_ref[...] = y + jnp.dot(g_ref[...], wdn_ref[...], preferred_element_type=f32)


def _ffn(y, hist, nw, wup, cw, cb, wdn, *, tm, shift, rows_per_seq):
    T = y.shape[0]
    n_seq = T // rows_per_seq
    tiles_per_seq = rows_per_seq // tm
    H = _round_up(2 * shift, V7X_SUBLANES)
    kern = functools.partial(_ffn_kernel, tm=tm, shift=shift, hist_rows=H, has_hist=hist is not None,
                             tiles_per_seq=tiles_per_seq, n_chunk=512)
    const = lambda i: (0, 0)
    in_specs = [pl.BlockSpec((tm, D_MODEL), lambda i: (i, 0))]
    args = [y]
    if hist is not None:
        in_specs.append(pl.BlockSpec((2 * shift, 2 * D_FF), lambda i: (i // tiles_per_seq, 0)))
        args.append(hist)
    in_specs += [pl.BlockSpec((1, D_MODEL), const),
                 pl.BlockSpec((D_MODEL, 2 * D_FF), const, pipeline_mode=pl.Buffered(1)),
                 pl.BlockSpec((FFN_CONV, 2 * D_FF), const),
                 pl.BlockSpec((1, 2 * D_FF), const),
                 pl.BlockSpec((D_FF, D_MODEL), const, pipeline_mode=pl.Buffered(1))]
    args += [nw, wup, cw, cb, wdn]
    out, st = pl.pallas_call(
        kern,
        grid=(T // tm,),
        in_specs=in_specs,
        out_specs=[pl.BlockSpec((tm, D_MODEL), lambda i: (i, 0)),
                   pl.BlockSpec((None, 2 * shift, 2 * D_FF), lambda i: (i // tiles_per_seq, 0, 0))],
        out_shape=[jax.ShapeDtypeStruct((T, D_MODEL), f32),
                   jax.ShapeDtypeStruct((n_seq, 2 * shift, 2 * D_FF), f32)],
        scratch_shapes=[pltpu.VMEM((H + tm, 2 * D_FF), f32), pltpu.VMEM((tm, D_FF), bf16)],
        compiler_params=_cparams("arbitrary"),
        name="conv_ffn",
    )(*args)
    return out, st


def _gdn_in_kernel(*refs, tm, shift, hist_rows, has_hist, tiles_per_seq):
    if has_hist:
        (y_ref, hist_ref, nw_ref, wqkv_ref, wz_ref, wab_ref, cw_ref, alog_ref, dtb_ref,
         q_ref, k_ref, v_ref, z_ref, slab_ref, st_ref, ext_ref) = refs
    else:
        (y_ref, nw_ref, wqkv_ref, wz_ref, wab_ref, cw_ref, alog_ref, dtb_ref,
         q_ref, k_ref, v_ref, z_ref, slab_ref, st_ref, ext_ref) = refs
        hist_ref = None
    H = hist_rows
    s = shift
    nh = GDN_CONV - 1
    W3 = 3 * GDN_WIDTH
    t = pl.program_id(0) % tiles_per_seq

    @pl.when(t == 0)
    def _():
        if has_hist:
            ext_ref[H - nh * s:H, :] = hist_ref[...]
        else:
            ext_ref[0:H, :] = jnp.zeros((H, W3), f32)

    @pl.when(t != 0)
    def _():
        ext_ref[H - nh * s:H, :] = ext_ref[H + tm - nh * s:H + tm, :]

    h = _rms_rows(y_ref[...], nw_ref[...]).astype(bf16)
    nc = 512
    for c in range(W3 // nc):
        ext_ref[H:H + tm, c * nc:(c + 1) * nc] = jnp.dot(h, wqkv_ref[:, c * nc:(c + 1) * nc], preferred_element_type=f32)
    for c in range(GDN_WIDTH // nc):
        z_ref[:, c * nc:(c + 1) * nc] = jnp.dot(h, wz_ref[:, c * nc:(c + 1) * nc], preferred_element_type=f32)
    ab = jnp.dot(h, wab_ref[...], preferred_element_type=f32)
    x = ab + dtb_ref[...]
    softplus = jnp.maximum(x, 0.0) + jnp.log(1.0 + jnp.exp(-jnp.abs(x)))
    g = -jnp.exp(alog_ref[...]) * softplus
    lane = lax.broadcasted_iota(jnp.int32, ab.shape, 1)
    slab_ref[...] = jnp.where(lane < GDN_HEADS, g, _sigmoid(ab))
    st_ref[...] = ext_ref[H + tm - nh * s:H + tm, :]

    hw = GDN_DK
    for j in range(W3 // hw):
        col = j * hw
        w = cw_ref[:, col:col + hw]
        c = ext_ref[H:H + tm, col:col + hw] * w[nh:nh + 1, :]
        for i in range(nh):
            c = c + ext_ref[H - (nh - i) * s:H - (nh - i) * s + tm, col:col + hw] * w[i:i + 1, :]
        c = _silu(c)
        if j < 2 * GDN_HEADS:
            c = c * lax.rsqrt(jnp.sum(c * c, axis=-1, keepdims=True) + RMS_EPS)
        if j < GDN_HEADS:
            q_ref[:, col:col + hw] = c * (GDN_DK ** -0.5)
        elif j < 2 * GDN_HEADS:
            k_ref[:, col - GDN_WIDTH:col - GDN_WIDTH + hw] = c
        else:
            v_ref[:, col - 2 * GDN_WIDTH:col - 2 * GDN_WIDTH + hw] = c


def _gdn_in(y, hist, nw, wqkv, wz, wab, cw, alog, dtb, *, tm, shift, rows_per_seq):
    T = y.shape[0]
    n_seq = T // rows_per_seq
    tiles_per_seq = rows_per_seq // tm
    nh = GDN_CONV - 1
    H = _round_up(nh * shift, V7X_SUBLANES)
    W3 = 3 * GDN_WIDTH
    kern = functools.partial(_gdn_in_kernel, tm=tm, shift=shift, hist_rows=H, has_hist=hist is not None,
                             tiles_per_seq=tiles_per_seq)
    const = lambda i: (0, 0)
    row = lambda i: (i, 0)
    in_specs = [pl.BlockSpec((tm, D_MODEL), row)]
    args = [y]
    if hist is not None:
        in_specs.append(pl.BlockSpec((nh * shift, W3), lambda i: (i // tiles_per_seq, 0)))
        args.append(hist)
    in_specs += [pl.BlockSpec((1, D_MODEL), const),
                 pl.BlockSpec((D_MODEL, W3), const, pipeline_mode=pl.Buffered(1)),
                 pl.BlockSpec((D_MODEL, GDN_WIDTH), const, pipeline_mode=pl.Buffered(1)),
                 pl.BlockSpec((D_MODEL, V7X_LANES), const),
                 pl.BlockSpec((GDN_CONV, W3), const),
                 pl.BlockSpec((1, V7X_LANES), const),
                 pl.BlockSpec((1, V7X_LANES), const)]
    args += [nw, wqkv, wz, wab, cw, alog, dtb]
    wide = pl.BlockSpec((tm, GDN_WIDTH), row)
    return pl.pallas_call(
        kern,
        grid=(T // tm,),
        in_specs=in_specs,
        out_specs=[wide, wide, wide, wide, pl.BlockSpec((tm, V7X_LANES), row),
                   pl.BlockSpec((None, nh * shift, W3), lambda i: (i // tiles_per_seq, 0, 0))],
        out_shape=[jax.ShapeDtypeStruct((T, GDN_WIDTH), f32)] * 4
        + [jax.ShapeDtypeStruct((T, V7X_LANES), f32), jax.ShapeDtypeStruct((n_seq, nh * shift, W3), f32)],
        scratch_shapes=[pltpu.VMEM((H + tm, W3), f32)],
        compiler_params=_cparams("arbitrary"),
        name="gdn_in",
    )(*args)


def _gdn_chunk_kernel(q_ref, k_ref, v_ref, slab_ref, s0_ref, o_ref, s_ref, *, nseq):
    C = GDN_CHUNK
    H = GDN_HEADS
    n = pl.program_id(1)

    @pl.when(n == 0)
    def _():
        s_ref[...] = s0_ref[...]

    ri = lax.broadcasted_iota(jnp.int32, (C, C), 0)
    ci = lax.broadcasted_iota(jnp.int32, (C, C), 1)
    tril = ri >= ci
    strict = ri > ci
    eye = (ri == ci).astype(f32)
    slab = [slab_ref[s] for s in range(nseq)]
    G = [_dot_exact_lhs(tril.astype(f32), slab[s]) for s in range(nseq)]
    GT = [G[s].T for s in range(nseq)]
    ch = [(s, h) for s in range(nseq) for h in range(H)]
    idx = range(len(ch))
    col = lambda h: slice(h * GDN_DK, (h + 1) * GDN_DK)
    qh = [q_ref[s, :, col(h)] for s, h in ch]
    kh = [k_ref[s, :, col(h)] for s, h in ch]
    vh = [v_ref[s, :, col(h)] for s, h in ch]
    Gc = [G[s][:, h:h + 1] for s, h in ch]
    bc = [slab[s][:, H + h:H + h + 1] for s, h in ch]
    gl = [G[s][C - 1:C, h:h + 1] for s, h in ch]
    decay = [jnp.exp(jnp.where(tril, Gc[i] - GT[s][h:h + 1, :], NEG)) for i, (s, h) in enumerate(ch)]
    kb = [kh[i].astype(bf16) for i in idx]
    A = [jnp.where(strict, bc[i] * _dot_nt(kb[i], kb[i]) * decay[i], 0.0) for i in idx]
    P = [eye - A[i] for i in idx]
    Ap = A
    for _ in range(int(math.log2(C)) - 1):
        Ap = [_dot(Ap[i], Ap[i]) for i in idx]
        P = [P[i] + _dot(P[i], Ap[i]) for i in idx]
    Tb = [P[i].astype(bf16) for i in idx]
    uv = [_dot(Tb[i], vh[i] * bc[i]) for i in idx]
    wk = [_dot(Tb[i], kh[i] * (bc[i] * jnp.exp(Gc[i]))) for i in idx]
    qk = [_dot_nt(qh[i], kb[i]) * decay[i] for i in idx]
    kdec = [kh[i] * jnp.exp(gl[i] - Gc[i]) for i in idx]
    S = [s_ref[s, h] for s, h in ch]
    Sb = [S[i].astype(bf16) for i in idx]
    u = [uv[i] - _dot(wk[i], Sb[i]) for i in idx]
    o = [_dot(qh[i] * jnp.exp(Gc[i]), Sb[i]) + _dot(qk[i], u[i]) for i in idx]
    Sn = [S[i] * jnp.exp(gl[i]) + _dot_tn(kdec[i], u[i]) for i in idx]
    for i, (s, h) in enumerate(ch):
        o_ref[s, :, col(h)] = o[i]
        s_ref[s, h] = Sn[i]


GDN_SEQS_PER_STEP = 4


def _gdn_chunk(q, k, v, slab, s0, *, seq_len):
    T = q.shape[0]
    B = T // seq_len
    N = seq_len // GDN_CHUNK
    nseq = math.gcd(B, GDN_SEQS_PER_STEP)
    blk = lambda w: pl.BlockSpec((nseq, GDN_CHUNK, w), lambda b, n: (b, n, 0))
    st = pl.BlockSpec((nseq, GDN_HEADS, GDN_DK, GDN_DK), lambda b, n: (b, 0, 0, 0))
    seqs = lambda x: x.reshape(B, seq_len, x.shape[-1])
    o, s_out = pl.pallas_call(
        functools.partial(_gdn_chunk_kernel, nseq=nseq),
        grid=(B // nseq, N),
        in_specs=[blk(GDN_WIDTH), blk(GDN_WIDTH), blk(GDN_WIDTH), blk(V7X_LANES), st],
        out_specs=[blk(GDN_WIDTH), st],
        out_shape=[jax.ShapeDtypeStruct((B, seq_len, GDN_WIDTH), f32),
                   jax.ShapeDtypeStruct((B, GDN_HEADS, GDN_DK, GDN_DK), f32)],
        compiler_params=_cparams("parallel", "arbitrary"),
        name="gdn_chunk",
    )(seqs(q), seqs(k), seqs(v), seqs(slab), s0)
    return o.reshape(T, GDN_WIDTH), s_out


def _gdn_out_kernel(y_ref, o_ref, z_ref, nw_ref, wout_ref, out_ref, g_ref):
    for h in range(GDN_HEADS):
        lo = h * GDN_DK
        o = _rms_rows(o_ref[:, lo:lo + GDN_DK], nw_ref[...])
        g_ref[:, lo:lo + GDN_DK] = (o * _silu(z_ref[:, lo:lo + GDN_DK])).astype(bf16)
    out_ref[...] = y_ref[...] + jnp.dot(g_ref[...], wout_ref[...], preferred_element_type=f32)


def _gdn_out(y, o, z, nw, wout, *, tm):
    T = y.shape[0]
    row = lambda i: (i, 0)
    const = lambda i: (0, 0)
    wide = pl.BlockSpec((tm, GDN_WIDTH), row)
    return pl.pallas_call(
        _gdn_out_kernel,
        grid=(T // tm,),
        in_specs=[pl.BlockSpec((tm, D_MODEL), row), wide, wide, pl.BlockSpec((1, GDN_DK), const),
                  pl.BlockSpec((GDN_WIDTH, D_MODEL), const)],
        out_specs=pl.BlockSpec((tm, D_MODEL), row),
        out_shape=jax.ShapeDtypeStruct((T, D_MODEL), f32),
        scratch_shapes=[pltpu.VMEM((tm, GDN_WIDTH), bf16)],
        compiler_params=_cparams("parallel"),
        name="gdn_out",
    )(y, o, z, nw, wout)


NSA_IN_W = Q_WIDTH + 6 * KV_WIDTH + 3 * NSA_HEADS
NSA_IN_WP = _round_up(NSA_IN_W, V7X_LANES)
KV0 = Q_WIDTH
GATE0 = Q_WIDTH + 6 * KV_WIDTH


def _lane_iota(shape):
    return lax.broadcasted_iota(jnp.int32, shape, 1)


def _rope_pair(x, cos, sin):
    half = HEAD_DIM // 2
    lane = _lane_iota(x.shape)
    rot = jnp.where(lane % HEAD_DIM < half, pltpu.roll(x, V7X_LANES - half, 1), pltpu.roll(x, half, 1))
    return x * cos + rot * sin


def _split_pair(x, fill):
    lane = _lane_iota(x.shape)
    return jnp.where(lane < HEAD_DIM, x, fill), jnp.where(lane < HEAD_DIM, pltpu.roll(x, HEAD_DIM, 1), fill)


def _nsa_in_kernel(y_ref, nw_ref, w_ref, cos_ref, sin_ref, oh_ref,
                   qraw_ref, qrot_ref, cmp_ref, sel_ref, win_ref, ksel_ref, vsel_ref, kwin_ref, vwin_ref, gate_ref,
                   proj_ref):
    h = _rms_rows(y_ref[...], nw_ref[...]).astype(bf16)
    nc = NSA_IN_WP // 3
    for c in range(3):
        proj_ref[:, c * nc:(c + 1) * nc] = jnp.dot(h, w_ref[:, c * nc:(c + 1) * nc], preferred_element_type=f32)
    cos = cos_ref[...]
    sin = sin_ref[...]
    oh = oh_ref[...]
    L2 = 2 * V7X_LANES
    for p in range(Q_WIDTH // V7X_LANES):
        x = proj_ref[:, p * V7X_LANES:(p + 1) * V7X_LANES]
        qraw_ref[:, p * V7X_LANES:(p + 1) * V7X_LANES] = x.astype(bf16)
        a, b = _split_pair(_rope_pair(x, cos, sin) * (ATTN_SCALE * LOG2E), 0.0)
        qrot_ref[:, p * L2:p * L2 + V7X_LANES] = a.astype(bf16)
        qrot_ref[:, p * L2 + V7X_LANES:(p + 1) * L2] = b.astype(bf16)
    cmp_ref[...] = proj_ref[:, KV0:KV0 + 2 * KV_WIDTH]
    for br, (o_ref, k_ref, v_ref, fill) in enumerate(((sel_ref, ksel_ref, vsel_ref, oh), (win_ref, kwin_ref, vwin_ref, 0.0))):
        base = KV0 + (br + 1) * 2 * KV_WIDTH
        for p in range(KV_WIDTH // V7X_LANES):
            lo = p * V7X_LANES
            kx = _rope_pair(proj_ref[:, base + lo:base + lo + V7X_LANES], cos, sin)
            o_ref[:, lo:lo + V7X_LANES] = kx
            a, b = _split_pair(kx, fill)
            k_ref[:, p * L2:p * L2 + V7X_LANES] = a.astype(bf16)
            k_ref[:, p * L2 + V7X_LANES:(p + 1) * L2] = b.astype(bf16)
            vx = proj_ref[:, base + KV_WIDTH + lo:base + KV_WIDTH + lo + V7X_LANES]
            o_ref[:, KV_WIDTH + lo:KV_WIDTH + lo + V7X_LANES] = vx
            a, b = _split_pair(vx, 1.0)
            v_ref[:, p * L2:p * L2 + V7X_LANES] = a.astype(bf16)
            v_ref[:, p * L2 + V7X_LANES:(p + 1) * L2] = b.astype(bf16)
    gate_ref[...] = _sigmoid(proj_ref[:, GATE0:GATE0 + V7X_LANES])


def _nsa_in(y, nw, w, cos, sin, onehot, *, tm, tiles_per_seq):
    T = y.shape[0]
    row = lambda i: (i, 0)
    const = lambda i: (0, 0)
    tab = lambda i: (i % tiles_per_seq, 0)
    P4 = NSA_KV_HEADS * V7X_LANES
    blk = lambda w_: pl.BlockSpec((tm, w_), row)
    shp = lambda w_, dt: jax.ShapeDtypeStruct((T, w_), dt)
    return pl.pallas_call(
        _nsa_in_kernel,
        grid=(T // tm,),
        in_specs=[blk(D_MODEL), pl.BlockSpec((1, D_MODEL), const),
                  pl.BlockSpec((D_MODEL, NSA_IN_WP), const, pipeline_mode=pl.Buffered(1)),
                  pl.BlockSpec((tm, V7X_LANES), tab), pl.BlockSpec((tm, V7X_LANES), tab),
                  pl.BlockSpec((tm, V7X_LANES), tab)],
        out_specs=[blk(Q_WIDTH), blk(2 * Q_WIDTH), blk(2 * KV_WIDTH), blk(2 * KV_WIDTH), blk(2 * KV_WIDTH),
                   blk(P4), blk(P4), blk(P4), blk(P4), blk(V7X_LANES)],
        out_shape=[shp(Q_WIDTH, bf16), shp(2 * Q_WIDTH, bf16), shp(2 * KV_WIDTH, f32), shp(2 * KV_WIDTH, f32),
                   shp(2 * KV_WIDTH, f32), shp(P4, bf16), shp(P4, bf16), shp(P4, bf16), shp(P4, bf16),
                   shp(V7X_LANES, f32)],
        scratch_shapes=[pltpu.VMEM((tm, NSA_IN_WP), f32)],
        compiler_params=_cparams("parallel"),
        name="nsa_in",
    )(y, nw, w, cos, sin, onehot)


def _cmp_select_kernel(q_ref, kbd_ref, vbd_ref, oc_ref, bias_ref, *, tq, nb, n_top):
    i = pl.program_id(1)
    G = NSA_GROUP
    gw = G * HEAD_DIM
    blk = lax.broadcasted_iota(jnp.int32, (nb, tq), 0)
    pos = i * tq + lax.broadcasted_iota(jnp.int32, (nb, tq), 1)
    complete = (blk + 1) * NSA_BLOCK - 1 <= pos
    cur = lax.shift_right_logical(pos, int(math.log2(NSA_BLOCK)))
    cand = blk < cur
    for g in range(NSA_KV_HEADS):
        sT = _dot_nt(kbd_ref[g], q_ref[:, g * gw:(g + 1) * gw]) * ATTN_SCALE
        ps = []
        imp = jnp.zeros((nb, tq), f32)
        for n in range(G):
            sm = jnp.where(complete, sT[n * nb:(n + 1) * nb], NEG)
            mx = jnp.max(sm, axis=0, keepdims=True)
            e = jnp.where(complete, jnp.exp(sm - mx), 0.0)
            den = jnp.sum(e, axis=0, keepdims=True)
            p = e / jnp.where(den > 0.0, den, 1.0)
            ps.append(p)
            imp = imp + p
        oc_ref[:, g * gw:(g + 1) * gw] = _dot_tn(jnp.concatenate(ps, axis=0), vbd_ref[g])
        vals = jnp.where(cand, imp, -1.0)
        cnt = jnp.zeros((nb, tq), jnp.int32)
        for j in range(nb):
            vj = vals[j:j + 1, :]
            cnt = cnt + ((vj > vals) | ((vj == vals) & (blk > j))).astype(jnp.int32)
        sel = (cand & (cnt < n_top)) | (blk == cur)
        biasT = jnp.where(sel, 0.0, NEG)
        full = jnp.concatenate([jnp.zeros((HEAD_DIM, tq), f32), biasT,
                                jnp.zeros((V7X_LANES - HEAD_DIM - nb, tq), f32)], axis=0)
        bias_ref[:, g * V7X_LANES:(g + 1) * V7X_LANES] = full.T.astype(bf16)


def _cmp_select(q_raw, kbd, vbd, *, seq_len, tq):
    T = q_raw.shape[0]
    nq = seq_len // tq
    nb = seq_len // NSA_BLOCK
    row = lambda b, i: (b * nq + i, 0)
    kern = functools.partial(_cmp_select_kernel, tq=tq, nb=nb, n_top=min(N_SEL - 1, nb))
    return pl.pallas_call(
        kern,
        grid=(T // seq_len, nq),
        in_specs=[pl.BlockSpec((tq, Q_WIDTH), row),
                  pl.BlockSpec((None,) + kbd.shape[1:], lambda b, i: (b, 0, 0, 0)),
                  pl.BlockSpec((None,) + vbd.shape[1:], lambda b, i: (b, 0, 0, 0))],
        out_specs=[pl.BlockSpec((tq, Q_WIDTH), row), pl.BlockSpec((tq, NSA_KV_HEADS * V7X_LANES), row)],
        out_shape=[jax.ShapeDtypeStruct((T, Q_WIDTH), f32),
                   jax.ShapeDtypeStruct((T, NSA_KV_HEADS * V7X_LANES), bf16)],
        compiler_params=_cparams("parallel", "parallel"),
        name="nsa_cmp_select",
    )(q_raw, kbd, vbd)


def _attn_kernel(*refs, tq, mode):
    if mode == "sel":
        q_ref, bias_ref, k_ref, v_ref, o_ref, mx_ref, acc_ref, s_ref = refs
    else:
        q_ref, k_ref, v_ref, o_ref, mx_ref, acc_ref, s_ref = refs
    i = pl.program_id(2)
    G = NSA_GROUP
    qs = []
    for n in range(G):
        qn = q_ref[:, n * V7X_LANES:(n + 1) * V7X_LANES]
        if mode == "sel":
            qn = qn + bias_ref[...]
        qs.append(qn)
    qa = jnp.concatenate(qs, axis=0)
    qpos = i * tq + lax.broadcasted_iota(jnp.int32, (G * tq, tq), 0) % tq
    kofs = lax.broadcasted_iota(jnp.int32, (G * tq, tq), 1)

    def scores(t, mask_fn):
        start = pl.multiple_of(t * tq, tq)
        s = _dot_nt(qa, k_ref[pl.ds(start, tq), :])
        if mask_fn is not None:
            s = jnp.where(mask_fn(t * tq + kofs), s, NEG)
        return s

    causal = lambda kpos: kpos <= qpos
    in_window = lambda kpos: qpos - kpos < WINDOW
    if mode == "sel":
        tiles = None
    else:
        nback = WINDOW // tq
        tiles = tuple((b, in_window if b == nback else None) for b in range(1, nback + 1))

    def slot_of(t):
        return t if mode == "sel" else i - t

    def first_pass(t, mask_fn, first=False):
        s = scores(t, mask_fn)
        s_ref[:, pl.ds(pl.multiple_of(slot_of(t) * tq, tq), tq)] = s
        mx_ref[...] = s if first else jnp.maximum(mx_ref[...], s)

    first_pass(i, causal, first=True)
    if mode == "sel":
        def max_body(t, c):
            first_pass(t, None)
            return c
        lax.fori_loop(0, i, max_body, 0)
    else:
        for back, mask_fn in tiles:
            @pl.when(i >= back)
            def _(back=back, mask_fn=mask_fn):
                first_pass(i - back, mask_fn)
    mx_ref[...] = jnp.broadcast_to(jnp.max(mx_ref[...], axis=-1, keepdims=True), mx_ref.shape)

    def accumulate(t, first=False):
        start = pl.multiple_of(t * tq, tq)
        s = s_ref[:, pl.ds(pl.multiple_of(slot_of(t) * tq, tq), tq)]
        p = jnp.exp2(s - mx_ref[...]).astype(bf16)
        pv = jnp.dot(p, v_ref[pl.ds(start, tq), :], preferred_element_type=f32)
        acc_ref[...] = pv if first else acc_ref[...] + pv

    accumulate(i, first=True)
    if mode == "sel":
        def acc_body(t, c):
            accumulate(t)
            return c
        lax.fori_loop(0, i, acc_body, 0)
    else:
        for back, mask_fn in tiles:
            @pl.when(i >= back)
            def _(back=back):
                accumulate(i - back)

    acc = acc_ref[...]
    o = acc / pltpu.roll(acc, HEAD_DIM, 1)
    lane = _lane_iota((tq, V7X_LANES))
    for p in range(G // 2):
        a = o[(2 * p) * tq:(2 * p + 1) * tq]
        b = o[(2 * p + 1) * tq:(2 * p + 2) * tq]
        o_ref[:, p * V7X_LANES:(p + 1) * V7X_LANES] = jnp.where(lane < HEAD_DIM, a, pltpu.roll(b, HEAD_DIM, 1))


def _attn(q_pad, bias, k_pad, v_pad, *, seq_len, tq, mode):
    assert mode in ("sel", "win") and WINDOW % tq == 0
    T = q_pad.shape[0]
    B = T // seq_len
    nq = seq_len // tq
    G = NSA_GROUP
    qrow = lambda b, g, i: (b * nq + i, g)
    kv = lambda b, g, i: (b, g)
    in_specs = [pl.BlockSpec((tq, G * V7X_LANES), qrow)]
    args = [q_pad]
    if mode == "sel":
        in_specs.append(pl.BlockSpec((tq, V7X_LANES), qrow))
        args.append(bias)
    in_specs += [pl.BlockSpec((seq_len, V7X_LANES), kv), pl.BlockSpec((seq_len, V7X_LANES), kv)]
    args += [k_pad, v_pad]
    return pl.pallas_call(
        functools.partial(_attn_kernel, tq=tq, mode=mode),
        grid=(B, NSA_KV_HEADS, nq),
        in_specs=in_specs,
        out_specs=pl.BlockSpec((tq, G * HEAD_DIM), qrow),
        out_shape=jax.ShapeDtypeStruct((T, Q_WIDTH), f32),
        scratch_shapes=[pltpu.VMEM((G * tq, tq), f32), pltpu.VMEM((G * tq, V7X_LANES), f32),
                        pltpu.VMEM((G * tq, seq_len if mode == "sel" else WINDOW + tq), f32)],
        compiler_params=_cparams("parallel", "parallel", "parallel"),
        name="nsa_attn_" + mode,
    )(*args)


def _nsa_out_kernel(y_ref, oc_ref, os_ref, ow_ref, gate_ref, e_ref, wout_ref, out_ref):
    g = gate_ref[...]
    ghi = g.astype(bf16)
    glo = (g - ghi.astype(f32)).astype(bf16)
    o = jnp.zeros(oc_ref.shape, f32)
    for j, br in enumerate((oc_ref, os_ref, ow_ref)):
        ge = (jnp.dot(ghi, e_ref[j], preferred_element_type=f32) + jnp.dot(glo, e_ref[j], preferred_element_type=f32))
        o = o + ge * br[...]
    out_ref[...] = y_ref[...] + jnp.dot(o.astype(bf16), wout_ref[...], preferred_element_type=f32)


def _nsa_out(y, oc, osel, ow, gates, expand, wout, *, tm):
    T = y.shape[0]
    row = lambda i: (i, 0)
    wide = pl.BlockSpec((tm, Q_WIDTH), row)
    return pl.pallas_call(
        _nsa_out_kernel,
        grid=(T // tm,),
        in_specs=[pl.BlockSpec((tm, D_MODEL), row), wide, wide, wide, pl.BlockSpec((tm, V7X_LANES), row),
                  pl.BlockSpec((3, V7X_LANES, Q_WIDTH), lambda i: (0, 0, 0)),
                  pl.BlockSpec((Q_WIDTH, D_MODEL), lambda i: (0, 0))],
        out_specs=pl.BlockSpec((tm, D_MODEL), row),
        out_shape=jax.ShapeDtypeStruct((T, D_MODEL), f32),
        compiler_params=_cparams("parallel"),
        name="nsa_out",
    )(y, oc, osel, ow, gates, expand, wout)


def _final_norm_kernel(y_ref, nw_ref, o_ref):
    o_ref[...] = _rms_rows(y_ref[...], nw_ref[...])


def _final_norm(y, nw, *, tm):
    T = y.shape[0]
    return pl.pallas_call(
        _final_norm_kernel,
        grid=(T // tm,),
        in_specs=[pl.BlockSpec((tm, D_MODEL), lambda i: (i, 0)), pl.BlockSpec((1, D_MODEL), lambda i: (0, 0))],
        out_specs=pl.BlockSpec((tm, D_MODEL), lambda i: (i, 0)),
        out_shape=jax.ShapeDtypeStruct((T, D_MODEL), f32),
        compiler_params=_cparams("parallel"),
        name="final_norm",
    )(y, nw)


DEC_ROWS = V7X_SUBLANES


def _page_gather_kernel(pt_ref, tbl_ref, o_ref, *, n_pages):
    b = pl.program_id(0)
    for p in range(n_pages):
        o_ref[p:p + 1, :] = tbl_ref[pl.ds(pt_ref[b, p], 1), :]


def _page_gather(page_table, tbl):
    DB, n_pages = page_table.shape
    n_phys, W = tbl.shape
    return pl.pallas_call(
        functools.partial(_page_gather_kernel, n_pages=n_pages),
        grid_spec=pltpu.PrefetchScalarGridSpec(
            num_scalar_prefetch=1,
            grid=(DB,),
            in_specs=[pl.BlockSpec((n_phys, W), lambda b, pt: (0, 0), pipeline_mode=pl.Buffered(1))],
            out_specs=pl.BlockSpec((None, n_pages, W), lambda b, pt: (b, 0, 0)),
        ),
        out_shape=jax.ShapeDtypeStruct((DB, n_pages, W), tbl.dtype),
        compiler_params=_cparams("arbitrary"),
        name="nsa_page_gather",
    )(page_table, tbl)


PAGE_BLOCKS = PAGE_SIZE // NSA_BLOCK
PAGE_ROWS = 2 * KV_WIDTH


CMP_PITCH = HEAD_DIM + V7X_SUBLANES


def _compress_pages_kernel(x_hbm, pe_ref, w_ref, o_ref, buf_ref, sem_ref, *, first, steps, M):
    i = pl.program_id(0)

    def copies(step, slot):
        return [pltpu.make_async_copy(x_hbm.at[first + step * M + k], buf_ref.at[slot, pl.ds(k * CMP_PITCH, HEAD_DIM), :],
                                      sem_ref.at[slot]) for k in range(M)]

    @pl.when(i == 0)
    def _():
        for c in copies(0, 0):
            c.start()

    @pl.when(i + 1 < steps)
    def _():
        for c in copies(i + 1, (i + 1) % 2):
            c.start()

    slot = i % 2
    for c in copies(i, slot):
        c.wait()
    buf = buf_ref.at[slot]
    acc = jnp.zeros((M, 2 * V7X_LANES), f32)
    for d in range(HEAD_DIM):
        a = buf[pl.ds(d, M, stride=CMP_PITCH), :].reshape(M // 8, 8, V7X_LANES) + pe_ref[d]
        acc = acc + jnp.dot(a.reshape(M, V7X_LANES).astype(bf16), w_ref[d], preferred_element_type=f32)
    is_k = lax.broadcasted_iota(jnp.int32, (M, V7X_LANES), 0) % (2 * NSA_KV_HEADS) < NSA_KV_HEADS
    o_ref[...] = jnp.where(is_k, acc[:, :V7X_LANES], acc[:, V7X_LANES:])


def _compress_pages(slabs, pe8, w2, *, layer, n_phys, pages):
    assert n_phys % pages == 0
    steps = n_phys // pages
    per_page = PAGE_ROWS // HEAD_DIM
    M = pages * per_page
    return pl.pallas_call(
        functools.partial(_compress_pages_kernel, first=layer * n_phys * per_page, steps=steps, M=M),
        grid=(steps,),
        in_specs=[pl.BlockSpec(memory_space=pl.ANY),
                  pl.BlockSpec(pe8.shape, lambda i: (0, 0, 0)),
                  pl.BlockSpec(w2.shape, lambda i: (0, 0, 0), pipeline_mode=pl.Buffered(1))],
        out_specs=pl.BlockSpec((M, V7X_LANES), lambda i: (i, 0)),
        out_shape=jax.ShapeDtypeStruct((steps * M, V7X_LANES), f32),
        scratch_shapes=[pltpu.VMEM((2, M * CMP_PITCH, PAGE_SIZE), f32), pltpu.SemaphoreType.DMA((2,))],
        compiler_params=_cparams("arbitrary"),
        name="nsa_compress_pages",
    )(slabs, pe8, w2)


def _cmp_decode_kernel(q_ref, ckv_ref, oc_ref, bias_ref, *, past, n_top):
    R = DEC_ROWS
    n_pages = ckv_ref.shape[0]
    nb = PAGE_BLOCKS * n_pages
    KVH, G = NSA_KV_HEADS, NSA_GROUP
    lane = lax.broadcasted_iota(jnp.int32, (R, nb), 1)
    blk = PAGE_BLOCKS * (lane % n_pages) + lane // n_pages
    pos = past + lax.broadcasted_iota(jnp.int32, (R, nb), 0)
    complete = (blk + 1) * NSA_BLOCK - 1 <= pos
    cand = blk < lax.shift_right_logical(pos, int(math.log2(NSA_BLOCK)))

    def blocks(c, g):
        lo = (c * KVH + g) * PAGE_BLOCKS * HEAD_DIM
        return jnp.concatenate([ckv_ref[:, lo + h * HEAD_DIM:lo + (h + 1) * HEAD_DIM] for h in range(PAGE_BLOCKS)],
                               axis=0).astype(bf16)

    ck = [blocks(0, g) for g in range(KVH)]
    cv = [blocks(1, g) for g in range(KVH)]
    heads = [(g, n) for g in range(KVH) for n in range(G)]
    qh = [q_ref[:, (g * G + n) * HEAD_DIM:(g * G + n + 1) * HEAD_DIM] for g, n in heads]
    sm = [jnp.where(complete, _dot_nt(qh[i], ck[g]) * ATTN_SCALE, NEG) for i, (g, n) in enumerate(heads)]
    e = [jnp.where(complete, jnp.exp(s - jnp.max(s, axis=-1, keepdims=True)), 0.0) for s in sm]
    den = [jnp.sum(x, axis=-1, keepdims=True) for x in e]
    p = [x / jnp.where(d > 0.0, d, 1.0) for x, d in zip(e, den)]
    for i, (g, n) in enumerate(heads):
        lo = (g * G + n) * HEAD_DIM
        oc_ref[:, lo:lo + HEAD_DIM] = _dot(p[i], cv[g])
    imp = jnp.concatenate([sum(p[g * G:(g + 1) * G]) for g in range(KVH)], axis=0)
    cand4 = jnp.concatenate([cand] * KVH, axis=0)
    blk4 = jnp.concatenate([blk] * KVH, axis=0)
    vals = jnp.where(cand4, imp, -1.0)
    cnts = [jnp.zeros(vals.shape, jnp.int32) for _ in range(4)]
    for s in range(1, nb):
        other = pltpu.roll(vals, s, 1)
        other_blk = pltpu.roll(blk4, s, 1)
        cnts[s % 4] = cnts[s % 4] + ((other > vals) | ((other == vals) & (other_blk < blk4))).astype(jnp.int32)
    cnt = (cnts[0] + cnts[1]) + (cnts[2] + cnts[3])
    bias = jnp.where(cand4 & (cnt < n_top), 0.0, NEG)
    for g in range(KVH):
        bias_ref[g] = bias[g * R:(g + 1) * R]


def _cmp_decode(q_raw, ckv, *, past):
    DB, n_pages = ckv.shape[:2]
    nb = PAGE_BLOCKS * n_pages
    assert nb == V7X_LANES
    return pl.pallas_call(
        functools.partial(_cmp_decode_kernel, past=past, n_top=min(N_SEL - 1, nb)),
        grid=(DB,),
        in_specs=[pl.BlockSpec((None, DEC_ROWS, Q_WIDTH), lambda b: (b, 0, 0)),
                  pl.BlockSpec((None,) + ckv.shape[1:], lambda b: (b, 0, 0))],
        out_specs=[pl.BlockSpec((None, DEC_ROWS, Q_WIDTH), lambda b: (b, 0, 0)),
                   pl.BlockSpec((None, NSA_KV_HEADS, DEC_ROWS, nb), lambda b: (b, 0, 0, 0))],
        out_shape=[jax.ShapeDtypeStruct((DB, DEC_ROWS, Q_WIDTH), f32),
                   jax.ShapeDtypeStruct((DB, NSA_KV_HEADS, DEC_ROWS, nb), f32)],
        compiler_params=_cparams("parallel"),
        name="nsa_cmp_decode",
    )(q_raw, ckv)


SEL_PAGES_PER_STEP = 16


def _sel_decode_kernel(pt_ref, q_ref, bias_ref, new_ref, pool_ref, o_ref, buf_ref, sem_ref, m_ref, l_ref, acc_ref,
                       *, layer, n_seq, n_pages, n_new):
    PG = SEL_PAGES_PER_STEP
    NG = n_pages // PG
    R = NSA_GROUP * DEC_ROWS
    keys = PG * PAGE_SIZE

    def copies(step, slot):
        b = step // NG
        j = step % NG
        return [pltpu.make_async_copy(pool_ref.at[layer, pt_ref[b, j * PG + p]], buf_ref.at[slot, p],
                                      sem_ref.at[slot]) for p in range(PG)]

    KVH = NSA_KV_HEADS

    def online(s, pv):
        m_old = [m_ref[g] for g in range(KVH)]
        m_new = [jnp.maximum(m_old[g], jnp.max(s[g], axis=-1, keepdims=True)) for g in range(KVH)]
        alpha = [jnp.exp2(m_old[g] - m_new[g]) for g in range(KVH)]
        p = [jnp.exp2(s[g] - m_new[g]) for g in range(KVH)]
        psum = [jnp.sum(p[g], axis=-1, keepdims=True) for g in range(KVH)]
        pvs = [pv[g](p[g]) for g in range(KVH)]
        for g in range(KVH):
            l_ref[g] = alpha[g] * l_ref[g] + psum[g]
            acc_ref[g] = alpha[g] * acc_ref[g] + pvs[g]
            m_ref[g] = m_new[g]

    for c in copies(0, 0):
        c.start()

    def body(step, carry):
        slot = step % 2
        b = step // NG
        j = step % NG

        @pl.when(step + 1 < n_seq * NG)
        def _():
            for c in copies(step + 1, 1 - slot):
                c.start()

        @pl.when(j == 0)
        def _():
            m_ref[...] = jnp.full(m_ref.shape, NEG, f32)
            l_ref[...] = jnp.zeros(l_ref.shape, f32)
            acc_ref[...] = jnp.zeros(acc_ref.shape, f32)
            tok = lax.broadcasted_iota(jnp.int32, (R, DEC_ROWS), 0) % DEC_ROWS
            key = lax.broadcasted_iota(jnp.int32, (R, DEC_ROWS), 1)
            causal = (key <= tok) & (key < n_new)
            qn = [q_ref[b, g][:, :HEAD_DIM] for g in range(KVH)]
            kn = [new_ref[b, :, g * HEAD_DIM:(g + 1) * HEAD_DIM] for g in range(KVH)]
            vn = [new_ref[b, :, KV_WIDTH + g * HEAD_DIM:KV_WIDTH + (g + 1) * HEAD_DIM] for g in range(KVH)]
            online([jnp.where(causal, _dot_nt(qn[g], kn[g]), NEG) for g in range(KVH)],
                   [functools.partial(lambda p, v: _dot(p, v), v=vn[g]) for g in range(KVH)])

        for c in copies(step, slot):
            c.wait()
        lane = lax.broadcasted_iota(jnp.int32, (V7X_LANES, keys), 0)
        key = lax.broadcasted_iota(jnp.int32, (V7X_LANES, keys), 1)
        kpage = j * PG + lax.shift_right_logical(key, int(math.log2(PAGE_SIZE)))
        khalf = lax.shift_right_logical(key, int(math.log2(NSA_BLOCK))) % PAGE_BLOCKS
        expand = (lane == khalf * n_pages + kpage).astype(bf16)
        qg = [q_ref[b, g][:, :HEAD_DIM] for g in range(KVH)]
        kt = [jnp.concatenate([buf_ref[slot, p, 0, g] for p in range(PG)], axis=1).astype(bf16) for g in range(KVH)]
        vt = [jnp.concatenate([buf_ref[slot, p, 1, g] for p in range(PG)], axis=1).astype(bf16) for g in range(KVH)]
        bias = [jnp.concatenate([bias_ref[b, g]] * NSA_GROUP, axis=0).astype(bf16) for g in range(KVH)]
        s = [_dot(qg[g], kt[g]) + jnp.dot(bias[g], expand, preferred_element_type=f32) for g in range(KVH)]
        online(s, [functools.partial(lambda p, v: _dot_nt(p, v), v=vt[g]) for g in range(KVH)])

        @pl.when(j == NG - 1)
        def _():
            o = [acc_ref[g] / l_ref[g] for g in range(KVH)]
            for g in range(KVH):
                for n in range(NSA_GROUP):
                    lo = (g * NSA_GROUP + n) * HEAD_DIM
                    o_ref[b, :, lo:lo + HEAD_DIM] = o[g][n * DEC_ROWS:(n + 1) * DEC_ROWS]

        return carry

    lax.fori_loop(0, n_seq * NG, body, 0)


def _sel_decode(page_table, q, bias, new_kv, pool, *, layer, n_new):
    DB, n_pages = page_table.shape
    assert n_pages % SEL_PAGES_PER_STEP == 0 and n_pages * PAGE_BLOCKS == V7X_LANES
    R = NSA_GROUP * DEC_ROWS
    whole = lambda shape: pl.BlockSpec(shape, lambda i, pt, _n=len(shape): (0,) * _n)
    return pl.pallas_call(
        functools.partial(_sel_decode_kernel, layer=layer, n_seq=DB, n_pages=n_pages, n_new=n_new),
        grid_spec=pltpu.PrefetchScalarGridSpec(
            num_scalar_prefetch=1,
            grid=(1,),
            in_specs=[whole(q.shape), whole(bias.shape), whole(new_kv.shape), pl.BlockSpec(memory_space=pl.ANY)],
            out_specs=whole((DB, DEC_ROWS, Q_WIDTH)),
            scratch_shapes=[pltpu.VMEM((2, SEL_PAGES_PER_STEP) + pool.shape[2:], f32),
                            pltpu.SemaphoreType.DMA((2,)),
                            pltpu.VMEM((NSA_KV_HEADS, R, 1), f32), pltpu.VMEM((NSA_KV_HEADS, R, 1), f32),
                            pltpu.VMEM((NSA_KV_HEADS, R, HEAD_DIM), f32)],
        ),
        out_shape=jax.ShapeDtypeStruct((DB, DEC_ROWS, Q_WIDTH), f32),
        compiler_params=_cparams("arbitrary"),
        name="nsa_sel_decode",
    )(page_table, q, bias, new_kv, pool)


def _win_decode_kernel(q_ref, buf_ref, new_ref, o_ref, *, n_new):
    wbuf = buf_ref.shape[-1]
    R = NSA_GROUP * DEC_ROWS
    KVH = NSA_KV_HEADS
    tok = lax.broadcasted_iota(jnp.int32, (R, wbuf), 0) % DEC_ROWS
    dist = wbuf + tok - lax.broadcasted_iota(jnp.int32, (R, wbuf), 1)
    in_buf = (dist >= 0) & (dist < WINDOW)
    tokn = lax.broadcasted_iota(jnp.int32, (R, DEC_ROWS), 0) % DEC_ROWS
    keyn = lax.broadcasted_iota(jnp.int32, (R, DEC_ROWS), 1)
    in_new = (keyn <= tokn) & (keyn < n_new)
    hs = range(KVH)
    q = [q_ref[g][:, :HEAD_DIM] for g in hs]
    kn = [new_ref[:, g * HEAD_DIM:(g + 1) * HEAD_DIM] for g in hs]
    vn = [new_ref[:, KV_WIDTH + g * HEAD_DIM:KV_WIDTH + (g + 1) * HEAD_DIM] for g in hs]
    s1 = [jnp.where(in_buf, _dot(q[g], buf_ref[0, g]), NEG) for g in hs]
    s2 = [jnp.where(in_new, _dot_nt(q[g], kn[g]), NEG) for g in hs]
    m = [jnp.maximum(jnp.max(s1[g], axis=-1, keepdims=True), jnp.max(s2[g], axis=-1, keepdims=True)) for g in hs]
    p1 = [jnp.exp2(s1[g] - m[g]) for g in hs]
    p2 = [jnp.exp2(s2[g] - m[g]) for g in hs]
    den = [jnp.sum(p1[g], axis=-1, keepdims=True) + jnp.sum(p2[g], axis=-1, keepdims=True) for g in hs]
    o = [(_dot_nt(p1[g], buf_ref[1, g]) + _dot(p2[g], vn[g])) / den[g] for g in hs]
    for g in hs:
        for n in range(NSA_GROUP):
            lo = (g * NSA_GROUP + n) * HEAD_DIM
            o_ref[:, lo:lo + HEAD_DIM] = o[g][n * DEC_ROWS:(n + 1) * DEC_ROWS]


def _win_decode(q, win_t, new_kv, *, layer, n_new):
    DB = q.shape[0]
    R = NSA_GROUP * DEC_ROWS
    return pl.pallas_call(
        functools.partial(_win_decode_kernel, n_new=n_new),
        grid=(DB,),
        in_specs=[pl.BlockSpec((None, NSA_KV_HEADS, R, V7X_LANES), lambda b: (b, 0, 0, 0)),
                  pl.BlockSpec((None, None) + win_t.shape[2:], lambda b: (layer, b, 0, 0, 0, 0)),
                  pl.BlockSpec((None, DEC_ROWS, 2 * KV_WIDTH), lambda b: (b, 0, 0))],
        out_specs=pl.BlockSpec((None, DEC_ROWS, Q_WIDTH), lambda b: (b, 0, 0)),
        out_shape=jax.ShapeDtypeStruct((DB, DEC_ROWS, Q_WIDTH), f32),
        compiler_params=_cparams("parallel"),
        name="nsa_win_decode",
    )(q, win_t, new_kv)


def _rope_tables(pos):
    half = HEAD_DIM // 2
    inv = ROPE_THETA ** (-jnp.arange(half, dtype=f32) / half)
    ang = pos.astype(f32)[:, None] * inv[None, :]
    cos, sin = jnp.cos(ang), jnp.sin(ang)
    reps = V7X_LANES // HEAD_DIM
    return jnp.tile(cos, (1, 2 * reps)), jnp.tile(jnp.concatenate([-sin, sin], axis=1), (1, reps))


def _block_onehot(pos):
    return (jnp.arange(V7X_LANES, dtype=jnp.int32)[None, :] == HEAD_DIM + pos[:, None] // NSA_BLOCK).astype(f32)


def _gate_expand():
    src = jnp.arange(V7X_LANES)[None, :, None]
    head = jnp.arange(Q_WIDTH)[None, None, :] // HEAD_DIM
    j = jnp.arange(3)[:, None, None]
    return (src == head * 3 + j).astype(bf16)


def _compress_page_weights(pe, w):
    w2 = jnp.einsum("lcde,hx->dhlcxe", w, jnp.eye(PAGE_BLOCKS, dtype=f32))
    w2 = w2.reshape(HEAD_DIM, PAGE_SIZE, 2 * PAGE_BLOCKS * HEAD_DIM).astype(bf16)
    pe8 = jnp.broadcast_to(pe.transpose(2, 1, 0)[:, :, None, None, :],
                           (HEAD_DIM, 2, NSA_KV_HEADS, PAGE_BLOCKS, NSA_BLOCK))
    return pe8.reshape(HEAD_DIM, 2 * NSA_KV_HEADS, PAGE_SIZE), w2


def _token_minor(cache):
    n = cache.ndim
    return jnp.transpose(cache, tuple(range(n - 4)) + (n - 3, n - 2, n - 1, n - 4))


def _time_major(x):
    return jnp.swapaxes(x, 0, 1).reshape((x.shape[0] * x.shape[1],) + x.shape[2:])


def _batch_major(x, db):
    return jnp.swapaxes(x.reshape((x.shape[0] // db, db) + x.shape[1:]), 0, 1)


def _gdn_weights(w_in, a_log, dt_bias):
    W3 = 3 * GDN_WIDTH
    pad = V7X_LANES - 2 * GDN_HEADS
    return (w_in[:, :W3].astype(bf16), w_in[:, W3:W3 + GDN_WIDTH].astype(bf16),
            jnp.pad(w_in[:, W3 + GDN_WIDTH:], ((0, 0), (0, pad))).astype(bf16),
            jnp.pad(a_log, (0, V7X_LANES - GDN_HEADS))[None], jnp.pad(dt_bias, (0, V7X_LANES - GDN_HEADS))[None])


def _gdn_layer(yp, ys, S0s, conv_s, nw, w_in, conv_w, a_log, dt_bias, norm_w, w_out, *, B, L, DB, DS):
    wqkv, wz, wab, alog, dtb = _gdn_weights(w_in, a_log, dt_bias)
    wout = w_out.astype(bf16)
    nw = nw[None]
    norm_w = norm_w[None]
    q, k, v, z, slab, conv_p = _gdn_in(yp, None, nw, wqkv, wz, wab, conv_w, alog, dtb, tm=256, shift=1, rows_per_seq=L)
    o, S_p = _gdn_chunk(q, k, v, slab, jnp.zeros((B, GDN_HEADS, GDN_DK, GDN_DK), f32), seq_len=L)
    yp = _gdn_out(yp, o, z, norm_w, wout, tm=512)
    Ts = DS * DB
    hist = _time_major(conv_s).reshape((GDN_CONV - 1) * DB, 3 * GDN_WIDTH)
    q, k, v, z, slab, conv_tm = _gdn_in(ys, hist, nw, wqkv, wz, wab, conv_w, alog, dtb, tm=Ts, shift=DB, rows_per_seq=Ts)

    def chunked(x):
        x = _batch_major(x, DB)
        return jnp.pad(x, ((0, 0), (0, GDN_CHUNK - DS), (0, 0))).reshape(DB * GDN_CHUNK, x.shape[-1])

    o, S_s = _gdn_chunk(chunked(q), chunked(k), chunked(v), chunked(slab), S0s, seq_len=GDN_CHUNK)
    o = _time_major(o.reshape(DB, GDN_CHUNK, GDN_WIDTH)[:, :DS])
    ys = _gdn_out(ys, o, z, norm_w, wout, tm=Ts)
    conv_s_new = _batch_major(conv_tm.reshape((GDN_CONV - 1) * DB, 3 * GDN_WIDTH), DB)
    return yp, ys, S_p, S_s, conv_p, conv_s_new


def _ffn_layer(yp, ys, conv_s, nw, w_up, conv_w, conv_b, w_down, *, B, L, DB, DS):
    wup = w_up.astype(bf16)
    wdn = w_down.astype(bf16)
    nw = nw[None]
    cb = conv_b[None]
    yp, st_p = _ffn(yp, None, nw, wup, conv_w, cb, wdn, tm=512, shift=1, rows_per_seq=L)
    Ts = DS * DB
    hist = _time_major(conv_s).reshape((FFN_CONV - 1) * DB, 2 * D_FF)
    ys, st_tm = _ffn(ys, hist, nw, wup, conv_w, cb, wdn, tm=Ts, shift=DB, rows_per_seq=Ts)
    st_s = _batch_major(st_tm.reshape((FFN_CONV - 1) * DB, 2 * D_FF), DB)
    return yp, ys, st_p, st_s


def _nsa_layer(yp, ys, j, cmp_t, sel_t, win_t, win_buf, page_table, nw, w_in, cmp_pe, cmp_w, w_out, *, B, L, DB, DS):
    past = page_table.shape[1] * PAGE_SIZE
    w = jnp.pad(w_in, ((0, 0), (0, NSA_IN_WP - NSA_IN_W))).astype(bf16)
    wout = w_out.astype(bf16)
    nw = nw[None]
    expand = _gate_expand()
    pe8, w2 = _compress_page_weights(cmp_pe, cmp_w)
    tq = WINDOW // 2
    tq_attn = WINDOW
    pos_p = jnp.arange(L, dtype=jnp.int32)
    cos, sin = _rope_tables(pos_p)
    (q_raw, q_rot, kv_cmp, kv_sel, kv_win, ksel, vsel, kwin, vwin, gates) = _nsa_in(
        yp, nw, w, cos, sin, _block_onehot(pos_p), tm=256, tiles_per_seq=L // 256)
    nb = L // NSA_BLOCK
    npg = B * L // PAGE_SIZE
    per_page = PAGE_ROWS // HEAD_DIM
    slabs = kv_cmp.reshape(npg, PAGE_SIZE, per_page, HEAD_DIM).transpose(0, 2, 3, 1)
    ckv = _compress_pages(slabs.reshape(npg * per_page, HEAD_DIM, PAGE_SIZE), pe8, w2, layer=0, n_phys=npg,
                          pages=_pick_tile(npg, 16))
    ck = ckv.reshape(B, L // PAGE_SIZE, 2, NSA_KV_HEADS, PAGE_BLOCKS, HEAD_DIM)
    ck = ck.transpose(0, 1, 4, 2, 3, 5).reshape(B, nb, 2, NSA_KV_HEADS, HEAD_DIM)
    eye = jnp.eye(NSA_GROUP, dtype=f32)
    kbd = jnp.einsum("bkgd,nm->bgnkmd", ck[:, :, 0], eye).reshape(B, NSA_KV_HEADS, NSA_GROUP * nb, NSA_GROUP * HEAD_DIM)
    vbd = jnp.einsum("bkgd,nm->bgnkmd", ck[:, :, 1], eye).reshape(B, NSA_KV_HEADS, NSA_GROUP * nb, NSA_GROUP * HEAD_DIM)
    o_c, bias = _cmp_select(q_raw, kbd.astype(bf16), vbd.astype(bf16), seq_len=L, tq=tq)
    o_s = _attn(q_rot, bias, ksel, vsel, seq_len=L, tq=tq_attn, mode="sel")
    o_w = _attn(q_rot, None, kwin, vwin, seq_len=L, tq=tq_attn, mode="win")
    yp = _nsa_out(yp, o_c, o_s, o_w, gates, expand, wout, tm=512)
    shp = (B, L, 2, NSA_KV_HEADS, HEAD_DIM)
    cmp_p, sel_p, win_p = kv_cmp.reshape(shp), kv_sel.reshape(shp), kv_win.reshape(shp)[:, -min(WINDOW, L):]
    Ts = DS * DB
    pos_s = past + jnp.arange(DS, dtype=jnp.int32)
    pos_rows = jnp.repeat(pos_s, DB)
    cos, sin = _rope_tables(pos_rows)
    (q_raw, q_rot, kv_cmp, kv_sel, kv_win, _, _, _, _, gates) = _nsa_in(
        ys, nw, w, cos, sin, jnp.zeros((Ts, V7X_LANES), f32), tm=Ts, tiles_per_seq=1)
    n_layers, n_phys = cmp_t.shape[:2]
    ckv_all = _compress_pages(cmp_t.reshape(n_layers * n_phys * PAGE_ROWS // HEAD_DIM, HEAD_DIM, PAGE_SIZE), pe8, w2,
                              layer=j, n_phys=n_phys, pages=_pick_tile(n_phys, 16))
    ckv = _page_gather(page_table, ckv_all.reshape(n_phys, 2 * KV_WIDTH * PAGE_BLOCKS))

    def rows8(x):
        return jnp.pad(_batch_major(x, DB), ((0, 0), (0, DEC_ROWS - DS), (0, 0)))

    o_c, bias = _cmp_decode(rows8(q_raw), ckv, past=past)
    q4 = rows8(q_rot).reshape(DB, DEC_ROWS, NSA_KV_HEADS, NSA_GROUP, V7X_LANES)
    q4 = q4.transpose(0, 2, 3, 1, 4).reshape(DB, NSA_KV_HEADS, NSA_GROUP * DEC_ROWS, V7X_LANES)
    o_s = _sel_decode(page_table, q4, bias, rows8(kv_sel), sel_t, layer=j, n_new=DS)
    o_w = _win_decode(q4, win_t, rows8(kv_win), layer=j, n_new=DS)
    ks = (DB, DS, 2, NSA_KV_HEADS, HEAD_DIM)
    cmp_s, sel_s, win_new = (_batch_major(t, DB).reshape(ks) for t in (kv_cmp, kv_sel, kv_win))
    win_s = jnp.concatenate([win_buf[:, DS:], win_new], axis=1)
    tm_rows = lambda o: _time_major(o[:, :DS])
    ys = _nsa_out(ys, tm_rows(o_c), tm_rows(o_s), tm_rows(o_w), gates, expand, wout, tm=Ts)
    return yp, ys, cmp_p, cmp_s, sel_p, sel_s, win_p, win_s


def kernel(x_prompt, x_sample, state_gdn, state_gdn_conv, cache_cmp, cache_sel, state_win, state_ffn_conv, page_table,
           norm_mix, norm_ffn, norm_final, gdn_w_in, gdn_conv_w, gdn_a_log, gdn_dt_bias, gdn_norm_w, gdn_w_out,
           nsa_w_in, nsa_cmp_pe, nsa_cmp_w, nsa_w_out, ffn_w_up, ffn_conv_w, ffn_conv_b, ffn_w_down):
    B, L, _ = x_prompt.shape
    DB, DS, _ = x_sample.shape
    depth = norm_mix.shape[0]
    dims = dict(B=B, L=L, DB=DB, DS=DS)
    yp = x_prompt.reshape(B * L, D_MODEL)
    ys = _time_major(x_sample)
    outs = [[] for _ in range(12)]
    cmp_t, sel_t, win_t = _token_minor(cache_cmp), _token_minor(cache_sel), _token_minor(state_win)
    for i in range(depth):
        j = i // 2
        if i % 2 == 0:
            yp, ys, S_p, S_s, c_p, c_s = _gdn_layer(yp, ys, state_gdn[j], state_gdn_conv[j], norm_mix[i], gdn_w_in[j],
                                                    gdn_conv_w[j], gdn_a_log[j], gdn_dt_bias[j], gdn_norm_w[j],
                                                    gdn_w_out[j], **dims)
            for lst, val in zip(outs[0:4], (S_p, S_s, c_p, c_s)):
                lst.append(val)
        else:
            yp, ys, *kv = _nsa_layer(yp, ys, j, cmp_t, sel_t, win_t, state_win[j], page_table, norm_mix[i],
                                     nsa_w_in[j], nsa_cmp_pe[j], nsa_cmp_w[j], nsa_w_out[j], **dims)
            for lst, val in zip(outs[4:10], kv):
                lst.append(val)
        yp, ys, f_p, f_s = _ffn_layer(yp, ys, state_ffn_conv[i], norm_ffn[i], ffn_w_up[i], ffn_conv_w[i],
                                      ffn_conv_b[i], ffn_w_down[i], **dims)
        outs[10].append(f_p)
        outs[11].append(f_s)
    y_prompt = _final_norm(yp, norm_final[None], tm=512).reshape(B, L, D_MODEL)
    y_sample = _batch_major(_final_norm(ys, norm_final[None], tm=DS * DB), DB)
    return (y_prompt, y_sample) + tuple(jnp.stack(o) for o in outs)
```

```python
import functools
import math

import jax
import jax.numpy as jnp
from jax import lax
from jax.experimental import pallas as pl
from jax.experimental.pallas import tpu as pltpu

f32 = jnp.float32
bf16 = jnp.bfloat16

D_MODEL = 1024
GDN_HEADS = 8
GDN_DK = 128
GDN_WIDTH = GDN_HEADS * GDN_DK
GDN_CONV = 4
GDN_CHUNK = 64
NSA_HEADS = 16
NSA_KV_HEADS = 4
NSA_GROUP = 4
HEAD_DIM = 64
Q_WIDTH = NSA_HEADS * HEAD_DIM
KV_WIDTH = NSA_KV_HEADS * HEAD_DIM
NSA_BLOCK = 64
N_SEL = 16
WINDOW = 512
PAGE_SIZE = 128
ROPE_THETA = 10000.0
ATTN_SCALE = HEAD_DIM ** -0.5
LOG2E = math.log2(math.e)
D_FF = 2816
FFN_CONV = 3
RMS_EPS = 1e-6
NEG = -1e30

V7X_LANES = 128
V7X_SUBLANES = 8
V7X_VMEM_BYTES = 64 * 1024 * 1024
VMEM_LIMIT = 56 * 1024 * 1024


def _cparams(*sem):
    return pltpu.CompilerParams(dimension_semantics=sem, vmem_limit_bytes=VMEM_LIMIT)


def _round_up(x, m):
    return (x + m - 1) // m * m


def _pick_tile(n, cap):
    best = None
    for t in range(V7X_SUBLANES, min(n, cap) + 1, V7X_SUBLANES):
        if n % t == 0:
            best = t
    assert best is not None, (n, cap)
    return best


def _dot(a, b):
    return jnp.dot(a.astype(bf16), b.astype(bf16), preferred_element_type=f32)


def _dot_nt(a, b):
    return lax.dot_general(a.astype(bf16), b.astype(bf16), (((1,), (1,)), ((), ())), preferred_element_type=f32)


def _dot_tn(a, b):
    return lax.dot_general(a.astype(bf16), b.astype(bf16), (((0,), (0,)), ((), ())), preferred_element_type=f32)


def _split3(x):
    hi = x.astype(bf16)
    r = x - hi.astype(f32)
    mid = r.astype(bf16)
    lo = (r - mid.astype(f32)).astype(bf16)
    return hi, mid, lo


def _dot_exact_lhs(a01, x):
    a = a01.astype(bf16)
    hi, mid, lo = _split3(x)
    return (jnp.dot(a, hi, preferred_element_type=f32) + jnp.dot(a, mid, preferred_element_type=f32)
            + jnp.dot(a, lo, preferred_element_type=f32))


def _silu(x):
    return x / (1.0 + jnp.exp(-x))


def _sigmoid(x):
    return 1.0 / (1.0 + jnp.exp(-x))


def _rms_rows(x, w):
    return x * lax.rsqrt(jnp.mean(x * x, axis=-1, keepdims=True) + RMS_EPS) * w


def _ffn_kernel(*refs, tm, shift, hist_rows, has_hist, tiles_per_seq, n_chunk):
    if has_hist:
        y_ref, hist_ref, nw_ref, wup_ref, cw_ref, cb_ref, wdn_ref, out_ref, st_ref, ext_ref, g_ref = refs
    else:
        y_ref, nw_ref, wup_ref, cw_ref, cb_ref, wdn_ref, out_ref, st_ref, ext_ref, g_ref = refs
        hist_ref = None
    H = hist_rows
    s = shift
    LN = V7X_LANES
    nblk = 2 * D_FF // LN
    t = pl.program_id(0) % tiles_per_seq

    @pl.when(t == 0)
    def _():
        if has_hist:
            for c in range(nblk):
                ext_ref[c, H - 2 * s:H, :] = hist_ref[:, c * LN:(c + 1) * LN]
        else:
            ext_ref[:, 0:H, :] = jnp.zeros((nblk, H, LN), f32)

    @pl.when(t != 0)
    def _():
        ext_ref[:, H - 2 * s:H, :] = ext_ref[:, H + tm - 2 * s:H + tm, :]

    y = y_ref[...]
    h = _rms_rows(y, nw_ref[...]).astype(bf16)
    per = n_chunk // LN
    for c in range(2 * D_FF // n_chunk):
        u = jnp.dot(h, wup_ref[:, c * n_chunk:(c + 1) * n_chunk], preferred_element_type=f32)
        for k in range(per):
            ext_ref[c * per + k, H:H + tm, :] = u[:, k * LN:(k + 1) * LN]
    for c in range(nblk):
        st_ref[:, c * LN:(c + 1) * LN] = ext_ref[c, H + tm - 2 * s:H + tm, :]

    def conv(c):
        w = cw_ref[:, c * LN:(c + 1) * LN]
        return (ext_ref[c, H - 2 * s:H - 2 * s + tm, :] * w[0:1, :] + ext_ref[c, H - s:H - s + tm, :] * w[1:2, :]
                + ext_ref[c, H:H + tm, :] * w[2:3, :] + cb_ref[:, c * LN:(c + 1) * LN])

    for j in range(D_FF // LN):
        g_ref[:, j * LN:(j + 1) * LN] = (_silu(conv(j)) * conv(D_FF // LN + j)).astype(bf16)
    out_ref[...] = y + jnp.dot(g_ref[...], wdn_ref[...], preferred_element_type=f32)


def _ffn(y, hist, nw, wup, cw, cb, wdn, *, tm, shift, rows_per_seq):
    T = y.shape[0]
    n_seq = T // rows_per_seq
    tiles_per_seq = rows_per_seq // tm
    H = _round_up(2 * shift, V7X_SUBLANES)
    kern = functools.partial(_ffn_kernel, tm=tm, shift=shift, hist_rows=H, has_hist=hist is not None,
                             tiles_per_seq=tiles_per_seq, n_chunk=512)
    const = lambda i: (0, 0)
    in_specs = [pl.BlockSpec((tm, D_MODEL), lambda i: (i, 0))]
    args = [y]
    if hist is not None:
        in_specs.append(pl.BlockSpec((2 * shift, 2 * D_FF), lambda i: (i // tiles_per_seq, 0)))
        args.append(hist)
    in_specs += [pl.BlockSpec((1, D_MODEL), const),
                 pl.BlockSpec((D_MODEL, 2 * D_FF), const, pipeline_mode=pl.Buffered(1)),
                 pl.BlockSpec((FFN_CONV, 2 * D_FF), const),
                 pl.BlockSpec((1, 2 * D_FF), const),
                 pl.BlockSpec((D_FF, D_MODEL), const, pipeline_mode=pl.Buffered(1))]
    args += [nw, wup, cw, cb, wdn]
    out, st = pl.pallas_call(
        kern,
        grid=(T // tm,),
        in_specs=in_specs,
        out_specs=[pl.BlockSpec((tm, D_MODEL), lambda i: (i, 0)),
                   pl.BlockSpec((None, 2 * shift, 2 * D_FF), lambda i: (i // tiles_per_seq, 0, 0))],
        out_shape=[jax.ShapeDtypeStruct((T, D_MODEL), f32),
                   jax.ShapeDtypeStruct((n_seq, 2 * shift, 2 * D_FF), f32)],
        scratch_shapes=[pltpu.VMEM((2 * D_FF // V7X_LANES, H + tm, V7X_LANES), f32), pltpu.VMEM((tm, D_FF), bf16)],
        compiler_params=_cparams("arbitrary"),
        name="conv_ffn",
    )(*args)
    return out, st


def _gdn_in_kernel(*refs, tm, shift, hist_rows, has_hist, tiles_per_seq):
    if has_hist:
        (y_ref, hist_ref, nw_ref, wqkv_ref, wz_ref, wab_ref, cw_ref, alog_ref, dtb_ref,
         q_ref, k_ref, v_ref, z_ref, slab_ref, st_ref, ext_ref) = refs
    else:
        (y_ref, nw_ref, wqkv_ref, wz_ref, wab_ref, cw_ref, alog_ref, dtb_ref,
         q_ref, k_ref, v_ref, z_ref, slab_ref, st_ref, ext_ref) = refs
        hist_ref = None
    H = hist_rows
    s = shift
    nh = GDN_CONV - 1
    W3 = 3 * GDN_WIDTH
    t = pl.program_id(0) % tiles_per_seq
    hw = GDN_DK
    nblk = W3 // hw

    @pl.when(t == 0)
    def _():
        if has_hist:
            for c in range(nblk):
                ext_ref[c, H - nh * s:H, :] = hist_ref[:, c * hw:(c + 1) * hw]
        else:
            ext_ref[:, 0:H, :] = jnp.zeros((nblk, H, hw), f32)

    @pl.when(t != 0)
    def _():
        ext_ref[:, H - nh * s:H, :] = ext_ref[:, H + tm - nh * s:H + tm, :]

    h = _rms_rows(y_ref[...], nw_ref[...]).astype(bf16)
    nc = 512
    for c in range(W3 // nc):
        u = jnp.dot(h, wqkv_ref[:, c * nc:(c + 1) * nc], preferred_element_type=f32)
        for k in range(nc // hw):
            ext_ref[c * (nc // hw) + k, H:H + tm, :] = u[:, k * hw:(k + 1) * hw]
    for c in range(GDN_WIDTH // nc):
        z_ref[:, c * nc:(c + 1) * nc] = jnp.dot(h, wz_ref[:, c * nc:(c + 1) * nc], preferred_element_type=f32)
    ab = jnp.dot(h, wab_ref[...], preferred_element_type=f32)
    x = ab + dtb_ref[...]
    softplus = jnp.maximum(x, 0.0) + jnp.log(1.0 + jnp.exp(-jnp.abs(x)))
    g = -jnp.exp(alog_ref[...]) * softplus
    lane = lax.broadcasted_iota(jnp.int32, ab.shape, 1)
    slab_ref[...] = jnp.where(lane < GDN_HEADS, g, _sigmoid(ab))
    for c in range(nblk):
        st_ref[:, c * hw:(c + 1) * hw] = ext_ref[c, H + tm - nh * s:H + tm, :]

    for j in range(nblk):
        col = j * hw
        w = cw_ref[:, col:col + hw]
        c = ext_ref[j, H:H + tm, :] * w[nh:nh + 1, :]
        for i in range(nh):
            c = c + ext_ref[j, H - (nh - i) * s:H - (nh - i) * s + tm, :] * w[i:i + 1, :]
        c = _silu(c)
        if j < 2 * GDN_HEADS:
            c = c * lax.rsqrt(jnp.sum(c * c, axis=-1, keepdims=True) + RMS_EPS)
        if j < GDN_HEADS:
            q_ref[:, col:col + hw] = c * (GDN_DK ** -0.5)
        elif j < 2 * GDN_HEADS:
            k_ref[:, col - GDN_WIDTH:col - GDN_WIDTH + hw] = c
        else:
            v_ref[:, col - 2 * GDN_WIDTH:col - 2 * GDN_WIDTH + hw] = c


def _gdn_in(y, hist, nw, wqkv, wz, wab, cw, alog, dtb, *, tm, shift, rows_per_seq):
    T = y.shape[0]
    n_seq = T // rows_per_seq
    tiles_per_seq = rows_per_seq // tm
    nh = GDN_CONV - 1
    H = _round_up(nh * shift, V7X_SUBLANES)
    W3 = 3 * GDN_WIDTH
    kern = functools.partial(_gdn_in_kernel, tm=tm, shift=shift, hist_rows=H, has_hist=hist is not None,
                             tiles_per_seq=tiles_per_seq)
    const = lambda i: (0, 0)
    row = lambda i: (i, 0)
    in_specs = [pl.BlockSpec((tm, D_MODEL), row)]
    args = [y]
    if hist is not None:
        in_specs.append(pl.BlockSpec((nh * shift, W3), lambda i: (i // tiles_per_seq, 0)))
        args.append(hist)
    in_specs += [pl.BlockSpec((1, D_MODEL), const),
                 pl.BlockSpec((D_MODEL, W3), const, pipeline_mode=pl.Buffered(1)),
                 pl.BlockSpec((D_MODEL, GDN_WIDTH), const, pipeline_mode=pl.Buffered(1)),
                 pl.BlockSpec((D_MODEL, V7X_LANES), const),
                 pl.BlockSpec((GDN_CONV, W3), const),
                 pl.BlockSpec((1, V7X_LANES), const),
                 pl.BlockSpec((1, V7X_LANES), const)]
    args += [nw, wqkv, wz, wab, cw, alog, dtb]
    wide = pl.BlockSpec((tm, GDN_WIDTH), row)
    return pl.pallas_call(
        kern,
        grid=(T // tm,),
        in_specs=in_specs,
        out_specs=[wide, wide, wide, wide, pl.BlockSpec((tm, V7X_LANES), row),
                   pl.BlockSpec((None, nh * shift, W3), lambda i: (i // tiles_per_seq, 0, 0))],
        out_shape=[jax.ShapeDtypeStruct((T, GDN_WIDTH), f32)] * 4
        + [jax.ShapeDtypeStruct((T, V7X_LANES), f32), jax.ShapeDtypeStruct((n_seq, nh * shift, W3), f32)],
        scratch_shapes=[pltpu.VMEM((W3 // GDN_DK, H + tm, GDN_DK), f32)],
        compiler_params=_cparams("arbitrary"),
        name="gdn_in",
    )(*args)


def _gdn_chunk_kernel(q_ref, k_ref, v_ref, slab_ref, s0_ref, o_ref, s_ref, *, nseq):
    C = GDN_CHUNK
    H = GDN_HEADS
    n = pl.program_id(1)

    @pl.when(n == 0)
    def _():
        s_ref[...] = s0_ref[...]

    ri = lax.broadcasted_iota(jnp.int32, (C, C), 0)
    ci = lax.broadcasted_iota(jnp.int32, (C, C), 1)
    tril = ri >= ci
    strict = ri > ci
    eye = (ri == ci).astype(f32)
    slab = [slab_ref[s] for s in range(nseq)]
    G = [_dot_exact_lhs(tril.astype(f32), slab[s]) for s in range(nseq)]
    GT = [G[s].T for s in range(nseq)]
    ch = [(s, h) for s in range(nseq) for h in range(H)]
    idx = range(len(ch))
    col = lambda h: slice(h * GDN_DK, (h + 1) * GDN_DK)
    qh = [q_ref[s, :, col(h)] for s, h in ch]
    kh = [k_ref[s, :, col(h)] for s, h in ch]
    vh = [v_ref[s, :, col(h)] for s, h in ch]
    Gc = [G[s][:, h:h + 1] for s, h in ch]
    bc = [slab[s][:, H + h:H + h + 1] for s, h in ch]
    gl = [G[s][C - 1:C, h:h + 1] for s, h in ch]
    decay = [jnp.exp(jnp.where(tril, Gc[i] - GT[s][h:h + 1, :], NEG)) for i, (s, h) in enumerate(ch)]
    kb = [kh[i].astype(bf16) for i in idx]
    A = [jnp.where(strict, bc[i] * _dot_nt(kb[i], kb[i]) * decay[i], 0.0) for i in idx]
    P = [eye - A[i] for i in idx]
    Ap = A
    for _ in range(int(math.log2(C)) - 1):
        Ap = [_dot(Ap[i], Ap[i]) for i in idx]
        P = [P[i] + _dot(P[i], Ap[i]) for i in idx]
    Tb = [P[i].astype(bf16) for i in idx]
    uv = [_dot(Tb[i], vh[i] * bc[i]) for i in idx]
    wk = [_dot(Tb[i], kh[i] * (bc[i] * jnp.exp(Gc[i]))) for i in idx]
    qk = [_dot_nt(qh[i], kb[i]) * decay[i] for i in idx]
    kdec = [kh[i] * jnp.exp(gl[i] - Gc[i]) for i in idx]
    S = [s_ref[s, h] for s, h in ch]
    Sb = [S[i].astype(bf16) for i in idx]
    u = [uv[i] - _dot(wk[i], Sb[i]) for i in idx]
    o = [_dot(qh[i] * jnp.exp(Gc[i]), Sb[i]) + _dot(qk[i], u[i]) for i in idx]
    Sn = [S[i] * jnp.exp(gl[i]) + _dot_tn(kdec[i], u[i]) for i in idx]
    for i, (s, h) in enumerate(ch):
        o_ref[s, :, col(h)] = o[i]
        s_ref[s, h] = Sn[i]


GDN_SEQS_PER_STEP = 4


def _gdn_chunk(q, k, v, slab, s0, *, seq_len):
    T = q.shape[0]
    B = T // seq_len
    N = seq_len // GDN_CHUNK
    nseq = math.gcd(B, GDN_SEQS_PER_STEP)
    blk = lambda w: pl.BlockSpec((nseq, GDN_CHUNK, w), lambda b, n: (b, n, 0))
    st = pl.BlockSpec((nseq, GDN_HEADS, GDN_DK, GDN_DK), lambda b, n: (b, 0, 0, 0))
    seqs = lambda x: x.reshape(B, seq_len, x.shape[-1])
    o, s_out = pl.pallas_call(
        functools.partial(_gdn_chunk_kernel, nseq=nseq),
        grid=(B // nseq, N),
        in_specs=[blk(GDN_WIDTH), blk(GDN_WIDTH), blk(GDN_WIDTH), blk(V7X_LANES), st],
        out_specs=[blk(GDN_WIDTH), st],
        out_shape=[jax.ShapeDtypeStruct((B, seq_len, GDN_WIDTH), f32),
                   jax.ShapeDtypeStruct((B, GDN_HEADS, GDN_DK, GDN_DK), f32)],
        compiler_params=_cparams("parallel", "arbitrary"),
        name="gdn_chunk",
    )(seqs(q), seqs(k), seqs(v), seqs(slab), s0)
    return o.reshape(T, GDN_WIDTH), s_out


def _gdn_out_kernel(y_ref, o_ref, z_ref, nw_ref, wout_ref, out_ref, g_ref):
    for h in range(GDN_HEADS):
        lo = h * GDN_DK
        o = _rms_rows(o_ref[:, lo:lo + GDN_DK], nw_ref[...])
        g_ref[:, lo:lo + GDN_DK] = (o * _silu(z_ref[:, lo:lo + GDN_DK])).astype(bf16)
    out_ref[...] = y_ref[...] + jnp.dot(g_ref[...], wout_ref[...], preferred_element_type=f32)


def _gdn_out(y, o, z, nw, wout, *, tm):
    T = y.shape[0]
    row = lambda i: (i, 0)
    const = lambda i: (0, 0)
    wide = pl.BlockSpec((tm, GDN_WIDTH), row)
    return pl.pallas_call(
        _gdn_out_kernel,
        grid=(T // tm,),
        in_specs=[pl.BlockSpec((tm, D_MODEL), row), wide, wide, pl.BlockSpec((1, GDN_DK), const),
                  pl.BlockSpec((GDN_WIDTH, D_MODEL), const)],
        out_specs=pl.BlockSpec((tm, D_MODEL), row),
        out_shape=jax.ShapeDtypeStruct((T, D_MODEL), f32),
        scratch_shapes=[pltpu.VMEM((tm, GDN_WIDTH), bf16)],
        compiler_params=_cparams("parallel"),
        name="gdn_out",
    )(y, o, z, nw, wout)


NSA_IN_W = Q_WIDTH + 6 * KV_WIDTH + 3 * NSA_HEADS
NSA_IN_WP = _round_up(NSA_IN_W, V7X_LANES)
KV0 = Q_WIDTH
GATE0 = Q_WIDTH + 6 * KV_WIDTH


def _lane_iota(shape):
    return lax.broadcasted_iota(jnp.int32, shape, 1)


def _rope_pair(x, cos, sin):
    half = HEAD_DIM // 2
    lane = _lane_iota(x.shape)
    rot = jnp.where(lane % HEAD_DIM < half, pltpu.roll(x, V7X_LANES - half, 1), pltpu.roll(x, half, 1))
    return x * cos + rot * sin


def _split_pair(x, fill):
    lane = _lane_iota(x.shape)
    return jnp.where(lane < HEAD_DIM, x, fill), jnp.where(lane < HEAD_DIM, pltpu.roll(x, HEAD_DIM, 1), fill)


def _nsa_in_kernel(*refs, token_minor, n_alias):
    y_ref, nw_ref, w_ref, cos_ref, sin_ref, oh_ref = refs[:6]
    (qraw_ref, qrot_ref, cmp_ref, sel_ref, win_ref, ksel_ref, vsel_ref, kwin_ref, vwin_ref, gate_ref,
     proj_ref) = refs[6 + n_alias:]

    def put(o_ref, lo, x):
        if token_minor:
            o_ref[lo:lo + V7X_LANES, :] = x.T
        else:
            o_ref[:, lo:lo + V7X_LANES] = x

    h = _rms_rows(y_ref[...], nw_ref[...]).astype(bf16)
    nc = NSA_IN_WP // 3
    for c in range(3):
        proj_ref[:, c * nc:(c + 1) * nc] = jnp.dot(h, w_ref[:, c * nc:(c + 1) * nc], preferred_element_type=f32)
    cos = cos_ref[...]
    sin = sin_ref[...]
    oh = oh_ref[...]
    L2 = 2 * V7X_LANES
    for p in range(Q_WIDTH // V7X_LANES):
        x = proj_ref[:, p * V7X_LANES:(p + 1) * V7X_LANES]
        qraw_ref[:, p * V7X_LANES:(p + 1) * V7X_LANES] = x.astype(bf16)
        a, b = _split_pair(_rope_pair(x, cos, sin) * (ATTN_SCALE * LOG2E), 0.0)
        qrot_ref[:, p * L2:p * L2 + V7X_LANES] = a.astype(bf16)
        qrot_ref[:, p * L2 + V7X_LANES:(p + 1) * L2] = b.astype(bf16)
    for p in range(2 * KV_WIDTH // V7X_LANES):
        put(cmp_ref, p * V7X_LANES, proj_ref[:, KV0 + p * V7X_LANES:KV0 + (p + 1) * V7X_LANES])
    for br, (o_ref, k_ref, v_ref, fill) in enumerate(((sel_ref, ksel_ref, vsel_ref, oh), (win_ref, kwin_ref, vwin_ref, 0.0))):
        base = KV0 + (br + 1) * 2 * KV_WIDTH
        for p in range(KV_WIDTH // V7X_LANES):
            lo = p * V7X_LANES
            kx = _rope_pair(proj_ref[:, base + lo:base + lo + V7X_LANES], cos, sin)
            put(o_ref, lo, kx)
            a, b = _split_pair(kx, fill)
            k_ref[:, p * L2:p * L2 + V7X_LANES] = a.astype(bf16)
            k_ref[:, p * L2 + V7X_LANES:(p + 1) * L2] = b.astype(bf16)
            vx = proj_ref[:, base + KV_WIDTH + lo:base + KV_WIDTH + lo + V7X_LANES]
            put(o_ref, KV_WIDTH + lo, vx)
            a, b = _split_pair(vx, 1.0)
            v_ref[:, p * L2:p * L2 + V7X_LANES] = a.astype(bf16)
            v_ref[:, p * L2 + V7X_LANES:(p + 1) * L2] = b.astype(bf16)
    gate_ref[...] = _sigmoid(proj_ref[:, GATE0:GATE0 + V7X_LANES])


def _nsa_in(y, nw, w, cos, sin, onehot, *, tm, tiles_per_seq, kv_layers=None):
    T = y.shape[0]
    row = lambda i: (i, 0)
    const = lambda i: (0, 0)
    tab = lambda i: (i % tiles_per_seq, 0)
    P4 = NSA_KV_HEADS * V7X_LANES
    W = 2 * KV_WIDTH
    blk = lambda w_: pl.BlockSpec((tm, w_), row)
    shp = lambda w_, dt: jax.ShapeDtypeStruct((T, w_), dt)
    args = [y, nw, w, cos, sin, onehot]
    in_specs = [blk(D_MODEL), pl.BlockSpec((1, D_MODEL), const),
                pl.BlockSpec((D_MODEL, NSA_IN_WP), const, pipeline_mode=pl.Buffered(1)),
                pl.BlockSpec((tm, V7X_LANES), tab), pl.BlockSpec((tm, V7X_LANES), tab),
                pl.BlockSpec((tm, V7X_LANES), tab)]
    aliases = {}
    if kv_layers is None:
        kv_specs = [blk(W)] * 3
        kv_shapes = [shp(W, f32)] * 3
    else:
        layer, n_layers, prev = kv_layers
        n_seq = T // (tm * tiles_per_seq)
        kv_specs = [pl.BlockSpec((None, None, W, tm), lambda i: (layer, i // tiles_per_seq, 0, i % tiles_per_seq))] * 3
        kv_shapes = [jax.ShapeDtypeStruct((n_layers, n_seq, W, tm * tiles_per_seq), f32)] * 3
        if prev is not None:
            aliases = {len(args) + k: 2 + k for k in range(3)}
            args += list(prev)
            in_specs += [pl.BlockSpec(memory_space=pl.ANY)] * 3
    return pl.pallas_call(
        functools.partial(_nsa_in_kernel, token_minor=kv_layers is not None, n_alias=len(aliases)),
        grid=(T // tm,),
        in_specs=in_specs,
        out_specs=[blk(Q_WIDTH), blk(2 * Q_WIDTH)] + kv_specs + [blk(P4), blk(P4), blk(P4), blk(P4), blk(V7X_LANES)],
        out_shape=[shp(Q_WIDTH, bf16), shp(2 * Q_WIDTH, bf16)] + kv_shapes
        + [shp(P4, bf16), shp(P4, bf16), shp(P4, bf16), shp(P4, bf16), shp(V7X_LANES, f32)],
        scratch_shapes=[pltpu.VMEM((tm, NSA_IN_WP), f32)],
        input_output_aliases=aliases,
        compiler_params=_cparams("parallel"),
        name="nsa_in",
    )(*args)


def _cmp_select_kernel(q_ref, kbd_ref, vbd_ref, oc_ref, bias_ref, *, tq, nb, n_top):
    i = pl.program_id(1)
    G = NSA_GROUP
    gw = G * HEAD_DIM
    blk = lax.broadcasted_iota(jnp.int32, (nb, tq), 0)
    pos = i * tq + lax.broadcasted_iota(jnp.int32, (nb, tq), 1)
    complete = (blk + 1) * NSA_BLOCK - 1 <= pos
    cur = lax.shift_right_logical(pos, int(math.log2(NSA_BLOCK)))
    cand = blk < cur
    for g in range(NSA_KV_HEADS):
        sT = _dot_nt(kbd_ref[g], q_ref[:, g * gw:(g + 1) * gw]) * ATTN_SCALE
        ps = []
        imp = jnp.zeros((nb, tq), f32)
        for n in range(G):
            sm = jnp.where(complete, sT[n * nb:(n + 1) * nb], NEG)
            mx = jnp.max(sm, axis=0, keepdims=True)
            e = jnp.where(complete, jnp.exp(sm - mx), 0.0)
            den = jnp.sum(e, axis=0, keepdims=True)
            p = e / jnp.where(den > 0.0, den, 1.0)
            ps.append(p)
            imp = imp + p
        oc_ref[:, g * gw:(g + 1) * gw] = _dot_tn(jnp.concatenate(ps, axis=0), vbd_ref[g])
        vals = jnp.where(cand, imp, -1.0)
        cnt = jnp.zeros((nb, tq), jnp.int32)
        for j in range(nb):
            vj = vals[j:j + 1, :]
            cnt = cnt + ((vj > vals) | ((vj == vals) & (blk > j))).astype(jnp.int32)
        sel = (cand & (cnt < n_top)) | (blk == cur)
        biasT = jnp.where(sel, 0.0, NEG)
        full = jnp.concatenate([jnp.zeros((HEAD_DIM, tq), f32), biasT,
                                jnp.zeros((V7X_LANES - HEAD_DIM - nb, tq), f32)], axis=0)
        bias_ref[:, g * V7X_LANES:(g + 1) * V7X_LANES] = full.T.astype(bf16)


def _cmp_select(q_raw, kbd, vbd, *, seq_len, tq):
    T = q_raw.shape[0]
    nq = seq_len // tq
    nb = seq_len // NSA_BLOCK
    row = lambda b, i: (b * nq + i, 0)
    kern = functools.partial(_cmp_select_kernel, tq=tq, nb=nb, n_top=min(N_SEL - 1, nb))
    return pl.pallas_call(
        kern,
        grid=(T // seq_len, nq),
        in_specs=[pl.BlockSpec((tq, Q_WIDTH), row),
                  pl.BlockSpec((None,) + kbd.shape[1:], lambda b, i: (b, 0, 0, 0)),
                  pl.BlockSpec((None,) + vbd.shape[1:], lambda b, i: (b, 0, 0, 0))],
        out_specs=[pl.BlockSpec((tq, Q_WIDTH), row), pl.BlockSpec((tq, NSA_KV_HEADS * V7X_LANES), row)],
        out_shape=[jax.ShapeDtypeStruct((T, Q_WIDTH), f32),
                   jax.ShapeDtypeStruct((T, NSA_KV_HEADS * V7X_LANES), bf16)],
        compiler_params=_cparams("parallel", "parallel"),
        name="nsa_cmp_select",
    )(q_raw, kbd, vbd)


def _attn_kernel(*refs, tq, mode):
    if mode == "sel":
        q_ref, bias_ref, k_ref, v_ref, o_ref, mx_ref, acc_ref, s_ref = refs
    else:
        q_ref, k_ref, v_ref, o_ref, mx_ref, acc_ref, s_ref = refs
    i = pl.program_id(2)
    G = NSA_GROUP
    qs = []
    for n in range(G):
        qn = q_ref[:, n * V7X_LANES:(n + 1) * V7X_LANES]
        if mode == "sel":
            qn = qn + bias_ref[...]
        qs.append(qn)
    qa = jnp.concatenate(qs, axis=0)
    qpos = i * tq + lax.broadcasted_iota(jnp.int32, (G * tq, tq), 0) % tq
    kofs = lax.broadcasted_iota(jnp.int32, (G * tq, tq), 1)

    def scores(t, mask_fn):
        start = pl.multiple_of(t * tq, tq)
        s = _dot_nt(qa, k_ref[pl.ds(start, tq), :])
        if mask_fn is not None:
            s = jnp.where(mask_fn(t * tq + kofs), s, NEG)
        return s

    causal = lambda kpos: kpos <= qpos
    in_window = lambda kpos: qpos - kpos < WINDOW
    if mode == "sel":
        tiles = None
    else:
        nback = WINDOW // tq
        tiles = tuple((b, in_window if b == nback else None) for b in range(1, nback + 1))

    def slot_of(t):
        return t if mode == "sel" else i - t

    def first_pass(t, mask_fn, first=False):
        s = scores(t, mask_fn)
        s_ref[:, pl.ds(pl.multiple_of(slot_of(t) * tq, tq), tq)] = s
        mx_ref[...] = s if first else jnp.maximum(mx_ref[...], s)

    first_pass(i, causal, first=True)
    if mode == "sel":
        def max_body(t, c):
            first_pass(t, None)
            return c
        lax.fori_loop(0, i, max_body, 0)
    else:
        for back, mask_fn in tiles:
            @pl.when(i >= back)
            def _(back=back, mask_fn=mask_fn):
                first_pass(i - back, mask_fn)
    mx_ref[...] = jnp.broadcast_to(jnp.max(mx_ref[...], axis=-1, keepdims=True), mx_ref.shape)

    def accumulate(t, first=False):
        start = pl.multiple_of(t * tq, tq)
        s = s_ref[:, pl.ds(pl.multiple_of(slot_of(t) * tq, tq), tq)]
        p = jnp.exp2(s - mx_ref[...]).astype(bf16)
        pv = jnp.dot(p, v_ref[pl.ds(start, tq), :], preferred_element_type=f32)
        acc_ref[...] = pv if first else acc_ref[...] + pv

    accumulate(i, first=True)
    if mode == "sel":
        def acc_body(t, c):
            accumulate(t)
            return c
        lax.fori_loop(0, i, acc_body, 0)
    else:
        for back, mask_fn in tiles:
            @pl.when(i >= back)
            def _(back=back):
                accumulate(i - back)

    acc = acc_ref[...]
    o = acc / pltpu.roll(acc, HEAD_DIM, 1)
    lane = _lane_iota((tq, V7X_LANES))
    for p in range(G // 2):
        a = o[(2 * p) * tq:(2 * p + 1) * tq]
        b = o[(2 * p + 1) * tq:(2 * p + 2) * tq]
        o_ref[:, p * V7X_LANES:(p + 1) * V7X_LANES] = jnp.where(lane < HEAD_DIM, a, pltpu.roll(b, HEAD_DIM, 1))


def _attn(q_pad, bias, k_pad, v_pad, *, seq_len, tq, mode):
    assert mode in ("sel", "win") and WINDOW % tq == 0
    T = q_pad.shape[0]
    B = T // seq_len
    nq = seq_len // tq
    G = NSA_GROUP
    qrow = lambda b, g, i: (b * nq + i, g)
    kv = lambda b, g, i: (b, g)
    in_specs = [pl.BlockSpec((tq, G * V7X_LANES), qrow)]
    args = [q_pad]
    if mode == "sel":
        in_specs.append(pl.BlockSpec((tq, V7X_LANES), qrow))
        args.append(bias)
    in_specs += [pl.BlockSpec((seq_len, V7X_LANES), kv), pl.BlockSpec((seq_len, V7X_LANES), kv)]
    args += [k_pad, v_pad]
    return pl.pallas_call(
        functools.partial(_attn_kernel, tq=tq, mode=mode),
        grid=(B, NSA_KV_HEADS, nq),
        in_specs=in_specs,
        out_specs=pl.BlockSpec((tq, G * HEAD_DIM), qrow),
        out_shape=jax.ShapeDtypeStruct((T, Q_WIDTH), f32),
        scratch_shapes=[pltpu.VMEM((G * tq, tq), f32), pltpu.VMEM((G * tq, V7X_LANES), f32),
                        pltpu.VMEM((G * tq, seq_len if mode == "sel" else WINDOW + tq), f32)],
        compiler_params=_cparams("parallel", "parallel", "parallel"),
        name="nsa_attn_" + mode,
    )(*args)


def _nsa_out_kernel(y_ref, oc_ref, os_ref, ow_ref, gate_ref, e_ref, wout_ref, out_ref):
    g = gate_ref[...]
    ghi = g.astype(bf16)
    glo = (g - ghi.astype(f32)).astype(bf16)
    o = jnp.zeros(oc_ref.shape, f32)
    for j, br in enumerate((oc_ref, os_ref, ow_ref)):
        ge = (jnp.dot(ghi, e_ref[j], preferred_element_type=f32) + jnp.dot(glo, e_ref[j], preferred_element_type=f32))
        o = o + ge * br[...]
    out_ref[...] = y_ref[...] + jnp.dot(o.astype(bf16), wout_ref[...], preferred_element_type=f32)


def _nsa_out(y, oc, osel, ow, gates, expand, wout, *, tm):
    T = y.shape[0]
    row = lambda i: (i, 0)
    wide = pl.BlockSpec((tm, Q_WIDTH), row)
    return pl.pallas_call(
        _nsa_out_kernel,
        grid=(T // tm,),
        in_specs=[pl.BlockSpec((tm, D_MODEL), row), wide, wide, wide, pl.BlockSpec((tm, V7X_LANES), row),
                  pl.BlockSpec((3, V7X_LANES, Q_WIDTH), lambda i: (0, 0, 0)),
                  pl.BlockSpec((Q_WIDTH, D_MODEL), lambda i: (0, 0))],
        out_specs=pl.BlockSpec((tm, D_MODEL), row),
        out_shape=jax.ShapeDtypeStruct((T, D_MODEL), f32),
        compiler_params=_cparams("parallel"),
        name="nsa_out",
    )(y, oc, osel, ow, gates, expand, wout)


def _final_norm_kernel(y_ref, nw_ref, o_ref):
    o_ref[...] = _rms_rows(y_ref[...], nw_ref[...])


def _final_norm(y, nw, *, tm):
    T = y.shape[0]
    return pl.pallas_call(
        _final_norm_kernel,
        grid=(T // tm,),
        in_specs=[pl.BlockSpec((tm, D_MODEL), lambda i: (i, 0)), pl.BlockSpec((1, D_MODEL), lambda i: (0, 0))],
        out_specs=pl.BlockSpec((tm, D_MODEL), lambda i: (i, 0)),
        out_shape=jax.ShapeDtypeStruct((T, D_MODEL), f32),
        compiler_params=_cparams("parallel"),
        name="final_norm",
    )(y, nw)


DEC_ROWS = V7X_SUBLANES


def _page_gather_kernel(pt_ref, tbl_ref, o_ref, *, n_pages):
    b = pl.program_id(0)
    for p in range(n_pages):
        o_ref[p:p + 1, :] = tbl_ref[pl.ds(pt_ref[b, p], 1), :]


def _page_gather(page_table, tbl):
    DB, n_pages = page_table.shape
    n_phys, W = tbl.shape
    return pl.pallas_call(
        functools.partial(_page_gather_kernel, n_pages=n_pages),
        grid_spec=pltpu.PrefetchScalarGridSpec(
            num_scalar_prefetch=1,
            grid=(DB,),
            in_specs=[pl.BlockSpec((n_phys, W), lambda b, pt: (0, 0), pipeline_mode=pl.Buffered(1))],
            out_specs=pl.BlockSpec((None, n_pages, W), lambda b, pt: (b, 0, 0)),
        ),
        out_shape=jax.ShapeDtypeStruct((DB, n_pages, W), tbl.dtype),
        compiler_params=_cparams("arbitrary"),
        name="nsa_page_gather",
    )(page_table, tbl)


PAGE_BLOCKS = PAGE_SIZE // NSA_BLOCK
PAGE_ROWS = 2 * KV_WIDTH


CMP_PITCH = HEAD_DIM + V7X_SUBLANES


def _compress_pages_kernel(x_hbm, pe_ref, w_ref, o_ref, buf_ref, sem_ref, *, first, steps, M):
    i = pl.program_id(0)

    def copies(step, slot):
        return [pltpu.make_async_copy(x_hbm.at[first + step * M + k], buf_ref.at[slot, pl.ds(k * CMP_PITCH, HEAD_DIM), :],
                                      sem_ref.at[slot]) for k in range(M)]

    @pl.when(i == 0)
    def _():
        for c in copies(0, 0):
            c.start()

    @pl.when(i + 1 < steps)
    def _():
        for c in copies(i + 1, (i + 1) % 2):
            c.start()

    slot = i % 2
    for c in copies(i, slot):
        c.wait()
    buf = buf_ref.at[slot]
    acc = jnp.zeros((M, 2 * V7X_LANES), f32)
    for d in range(HEAD_DIM):
        a = buf[pl.ds(d, M, stride=CMP_PITCH), :].reshape(M // 8, 8, V7X_LANES) + pe_ref[d]
        acc = acc + jnp.dot(a.reshape(M, V7X_LANES).astype(bf16), w_ref[d], preferred_element_type=f32)
    is_k = lax.broadcasted_iota(jnp.int32, (M, V7X_LANES), 0) % (2 * NSA_KV_HEADS) < NSA_KV_HEADS
    o_ref[...] = jnp.where(is_k, acc[:, :V7X_LANES], acc[:, V7X_LANES:])


def _compress_pages(slabs, pe8, w2, *, layer, n_phys, pages):
    assert n_phys % pages == 0
    steps = n_phys // pages
    per_page = PAGE_ROWS // HEAD_DIM
    M = pages * per_page
    return pl.pallas_call(
        functools.partial(_compress_pages_kernel, first=layer * n_phys * per_page, steps=steps, M=M),
        grid=(steps,),
        in_specs=[pl.BlockSpec(memory_space=pl.ANY),
                  pl.BlockSpec(pe8.shape, lambda i: (0, 0, 0)),
                  pl.BlockSpec(w2.shape, lambda i: (0, 0, 0), pipeline_mode=pl.Buffered(1))],
        out_specs=pl.BlockSpec((M, V7X_LANES), lambda i: (i, 0)),
        out_shape=jax.ShapeDtypeStruct((steps * M, V7X_LANES), f32),
        scratch_shapes=[pltpu.VMEM((2, M * CMP_PITCH, PAGE_SIZE), f32), pltpu.SemaphoreType.DMA((2,))],
        compiler_params=_cparams("arbitrary"),
        name="nsa_compress_pages",
    )(slabs, pe8, w2)


def _cmp_decode_kernel(q_ref, ckv_ref, oc_ref, bias_ref, *, past, n_top):
    R = DEC_ROWS
    n_pages = ckv_ref.shape[0]
    nb = PAGE_BLOCKS * n_pages
    KVH, G = NSA_KV_HEADS, NSA_GROUP
    lane = lax.broadcasted_iota(jnp.int32, (R, nb), 1)
    blk = PAGE_BLOCKS * (lane % n_pages) + lane // n_pages
    pos = past + lax.broadcasted_iota(jnp.int32, (R, nb), 0)
    complete = (blk + 1) * NSA_BLOCK - 1 <= pos
    cand = blk < lax.shift_right_logical(pos, int(math.log2(NSA_BLOCK)))

    def blocks(c, g):
        lo = (c * KVH + g) * PAGE_BLOCKS * HEAD_DIM
        return jnp.concatenate([ckv_ref[:, lo + h * HEAD_DIM:lo + (h + 1) * HEAD_DIM] for h in range(PAGE_BLOCKS)],
                               axis=0).astype(bf16)

    ck = [blocks(0, g) for g in range(KVH)]
    cv = [blocks(1, g) for g in range(KVH)]
    heads = [(g, n) for g in range(KVH) for n in range(G)]
    qh = [q_ref[:, (g * G + n) * HEAD_DIM:(g * G + n + 1) * HEAD_DIM] for g, n in heads]
    sm = [jnp.where(complete, _dot_nt(qh[i], ck[g]) * ATTN_SCALE, NEG) for i, (g, n) in enumerate(heads)]
    e = [jnp.where(complete, jnp.exp(s - jnp.max(s, axis=-1, keepdims=True)), 0.0) for s in sm]
    den = [jnp.sum(x, axis=-1, keepdims=True) for x in e]
    p = [x / jnp.where(d > 0.0, d, 1.0) for x, d in zip(e, den)]
    for i, (g, n) in enumerate(heads):
        lo = (g * G + n) * HEAD_DIM
        oc_ref[:, lo:lo + HEAD_DIM] = _dot(p[i], cv[g])
    imp = jnp.concatenate([sum(p[g * G:(g + 1) * G]) for g in range(KVH)], axis=0)
    cand4 = jnp.concatenate([cand] * KVH, axis=0)
    blk4 = jnp.concatenate([blk] * KVH, axis=0)
    vals = jnp.where(cand4, imp, -1.0)
    cnts = [jnp.zeros(vals.shape, jnp.int32) for _ in range(4)]
    for s in range(1, nb):
        other = pltpu.roll(vals, s, 1)
        other_blk = pltpu.roll(blk4, s, 1)
        cnts[s % 4] = cnts[s % 4] + ((other > vals) | ((other == vals) & (other_blk < blk4))).astype(jnp.int32)
    cnt = (cnts[0] + cnts[1]) + (cnts[2] + cnts[3])
    bias = jnp.where(cand4 & (cnt < n_top), 0.0, NEG)
    for g in range(KVH):
        bias_ref[g] = bias[g * R:(g + 1) * R]


def _cmp_decode(q_raw, ckv, *, past):
    DB, n_pages = ckv.shape[:2]
    nb = PAGE_BLOCKS * n_pages
    assert nb == V7X_LANES
    return pl.pallas_call(
        functools.partial(_cmp_decode_kernel, past=past, n_top=min(N_SEL - 1, nb)),
        grid=(DB,),
        in_specs=[pl.BlockSpec((None, DEC_ROWS, Q_WIDTH), lambda b: (b, 0, 0)),
                  pl.BlockSpec((None,) + ckv.shape[1:], lambda b: (b, 0, 0))],
        out_specs=[pl.BlockSpec((None, DEC_ROWS, Q_WIDTH), lambda b: (b, 0, 0)),
                   pl.BlockSpec((None, NSA_KV_HEADS, DEC_ROWS, nb), lambda b: (b, 0, 0, 0))],
        out_shape=[jax.ShapeDtypeStruct((DB, DEC_ROWS, Q_WIDTH), f32),
                   jax.ShapeDtypeStruct((DB, NSA_KV_HEADS, DEC_ROWS, nb), f32)],
        compiler_params=_cparams("parallel"),
        name="nsa_cmp_decode",
    )(q_raw, ckv)


SEL_PAGES_PER_STEP = 16


def _sel_decode_kernel(pt_ref, q_ref, bias_ref, new_ref, pool_ref, o_ref, buf_ref, sem_ref, m_ref, l_ref, acc_ref,
                       *, layer, n_seq, n_pages, n_new):
    PG = SEL_PAGES_PER_STEP
    NG = n_pages // PG
    R = NSA_GROUP * DEC_ROWS
    keys = PG * PAGE_SIZE

    def copies(step, slot):
        b = step // NG
        j = step % NG
        return [pltpu.make_async_copy(pool_ref.at[layer, pt_ref[b, j * PG + p]], buf_ref.at[slot, p],
                                      sem_ref.at[slot]) for p in range(PG)]

    KVH = NSA_KV_HEADS

    def online(s, pv):
        m_old = [m_ref[g] for g in range(KVH)]
        m_new = [jnp.maximum(m_old[g], jnp.max(s[g], axis=-1, keepdims=True)) for g in range(KVH)]
        alpha = [jnp.exp2(m_old[g] - m_new[g]) for g in range(KVH)]
        p = [jnp.exp2(s[g] - m_new[g]) for g in range(KVH)]
        psum = [jnp.sum(p[g], axis=-1, keepdims=True) for g in range(KVH)]
        pvs = [pv[g](p[g]) for g in range(KVH)]
        for g in range(KVH):
            l_ref[g] = alpha[g] * l_ref[g] + psum[g]
            acc_ref[g] = alpha[g] * acc_ref[g] + pvs[g]
            m_ref[g] = m_new[g]

    for c in copies(0, 0):
        c.start()

    def body(step, carry):
        slot = step % 2
        b = step // NG
        j = step % NG

        @pl.when(step + 1 < n_seq * NG)
        def _():
            for c in copies(step + 1, 1 - slot):
                c.start()

        @pl.when(j == 0)
        def _():
            m_ref[...] = jnp.full(m_ref.shape, NEG, f32)
            l_ref[...] = jnp.zeros(l_ref.shape, f32)
            acc_ref[...] = jnp.zeros(acc_ref.shape, f32)
            tok = lax.broadcasted_iota(jnp.int32, (R, DEC_ROWS), 0) % DEC_ROWS
            key = lax.broadcasted_iota(jnp.int32, (R, DEC_ROWS), 1)
            causal = (key <= tok) & (key < n_new)
            qn = [q_ref[b, g][:, :HEAD_DIM] for g in range(KVH)]
            kn = [new_ref[b, :, g * HEAD_DIM:(g + 1) * HEAD_DIM] for g in range(KVH)]
            vn = [new_ref[b, :, KV_WIDTH + g * HEAD_DIM:KV_WIDTH + (g + 1) * HEAD_DIM] for g in range(KVH)]
            online([jnp.where(causal, _dot_nt(qn[g], kn[g]), NEG) for g in range(KVH)],
                   [functools.partial(lambda p, v: _dot(p, v), v=vn[g]) for g in range(KVH)])

        for c in copies(step, slot):
            c.wait()
        lane = lax.broadcasted_iota(jnp.int32, (V7X_LANES, keys), 0)
        key = lax.broadcasted_iota(jnp.int32, (V7X_LANES, keys), 1)
        kpage = j * PG + lax.shift_right_logical(key, int(math.log2(PAGE_SIZE)))
        khalf = lax.shift_right_logical(key, int(math.log2(NSA_BLOCK))) % PAGE_BLOCKS
        expand = (lane == khalf * n_pages + kpage).astype(bf16)
        qg = [q_ref[b, g][:, :HEAD_DIM] for g in range(KVH)]
        kt = [jnp.concatenate([buf_ref[slot, p, 0, g] for p in range(PG)], axis=1).astype(bf16) for g in range(KVH)]
        vt = [jnp.concatenate([buf_ref[slot, p, 1, g] for p in range(PG)], axis=1).astype(bf16) for g in range(KVH)]
        bias = [jnp.concatenate([bias_ref[b, g]] * NSA_GROUP, axis=0).astype(bf16) for g in range(KVH)]
        s = [_dot(qg[g], kt[g]) + jnp.dot(bias[g], expand, preferred_element_type=f32) for g in range(KVH)]
        online(s, [functools.partial(lambda p, v: _dot_nt(p, v), v=vt[g]) for g in range(KVH)])

        @pl.when(j == NG - 1)
        def _():
            o = [acc_ref[g] / l_ref[g] for g in range(KVH)]
            for g in range(KVH):
                for n in range(NSA_GROUP):
                    lo = (g * NSA_GROUP + n) * HEAD_DIM
                    o_ref[b, :, lo:lo + HEAD_DIM] = o[g][n * DEC_ROWS:(n + 1) * DEC_ROWS]

        return carry

    lax.fori_loop(0, n_seq * NG, body, 0)


def _sel_decode(page_table, q, bias, new_kv, pool, *, layer, n_new):
    DB, n_pages = page_table.shape
    assert n_pages % SEL_PAGES_PER_STEP == 0 and n_pages * PAGE_BLOCKS == V7X_LANES
    R = NSA_GROUP * DEC_ROWS
    whole = lambda shape: pl.BlockSpec(shape, lambda i, pt, _n=len(shape): (0,) * _n)
    return pl.pallas_call(
        functools.partial(_sel_decode_kernel, layer=layer, n_seq=DB, n_pages=n_pages, n_new=n_new),
        grid_spec=pltpu.PrefetchScalarGridSpec(
            num_scalar_prefetch=1,
            grid=(1,),
            in_specs=[whole(q.shape), whole(bias.shape), whole(new_kv.shape), pl.BlockSpec(memory_space=pl.ANY)],
            out_specs=whole((DB, DEC_ROWS, Q_WIDTH)),
            scratch_shapes=[pltpu.VMEM((2, SEL_PAGES_PER_STEP) + pool.shape[2:], f32),
                            pltpu.SemaphoreType.DMA((2,)),
                            pltpu.VMEM((NSA_KV_HEADS, R, 1), f32), pltpu.VMEM((NSA_KV_HEADS, R, 1), f32),
                            pltpu.VMEM((NSA_KV_HEADS, R, HEAD_DIM), f32)],
        ),
        out_shape=jax.ShapeDtypeStruct((DB, DEC_ROWS, Q_WIDTH), f32),
        compiler_params=_cparams("arbitrary"),
        name="nsa_sel_decode",
    )(page_table, q, bias, new_kv, pool)


def _win_decode_kernel(q_ref, buf_ref, new_ref, o_ref, *, n_new):
    wbuf = buf_ref.shape[-1]
    R = NSA_GROUP * DEC_ROWS
    KVH = NSA_KV_HEADS
    tok = lax.broadcasted_iota(jnp.int32, (R, wbuf), 0) % DEC_ROWS
    dist = wbuf + tok - lax.broadcasted_iota(jnp.int32, (R, wbuf), 1)
    in_buf = (dist >= 0) & (dist < WINDOW)
    tokn = lax.broadcasted_iota(jnp.int32, (R, DEC_ROWS), 0) % DEC_ROWS
    keyn = lax.broadcasted_iota(jnp.int32, (R, DEC_ROWS), 1)
    in_new = (keyn <= tokn) & (keyn < n_new)
    hs = range(KVH)
    q = [q_ref[g][:, :HEAD_DIM] for g in hs]
    kn = [new_ref[:, g * HEAD_DIM:(g + 1) * HEAD_DIM] for g in hs]
    vn = [new_ref[:, KV_WIDTH + g * HEAD_DIM:KV_WIDTH + (g + 1) * HEAD_DIM] for g in hs]
    s1 = [jnp.where(in_buf, _dot(q[g], buf_ref[0, g]), NEG) for g in hs]
    s2 = [jnp.where(in_new, _dot_nt(q[g], kn[g]), NEG) for g in hs]
    m = [jnp.maximum(jnp.max(s1[g], axis=-1, keepdims=True), jnp.max(s2[g], axis=-1, keepdims=True)) for g in hs]
    p1 = [jnp.exp2(s1[g] - m[g]) for g in hs]
    p2 = [jnp.exp2(s2[g] - m[g]) for g in hs]
    den = [jnp.sum(p1[g], axis=-1, keepdims=True) + jnp.sum(p2[g], axis=-1, keepdims=True) for g in hs]
    o = [(_dot_nt(p1[g], buf_ref[1, g]) + _dot(p2[g], vn[g])) / den[g] for g in hs]
    for g in hs:
        for n in range(NSA_GROUP):
            lo = (g * NSA_GROUP + n) * HEAD_DIM
            o_ref[:, lo:lo + HEAD_DIM] = o[g][n * DEC_ROWS:(n + 1) * DEC_ROWS]


def _win_decode(q, win_t, new_kv, *, layer, n_new):
    DB = q.shape[0]
    R = NSA_GROUP * DEC_ROWS
    return pl.pallas_call(
        functools.partial(_win_decode_kernel, n_new=n_new),
        grid=(DB,),
        in_specs=[pl.BlockSpec((None, NSA_KV_HEADS, R, V7X_LANES), lambda b: (b, 0, 0, 0)),
                  pl.BlockSpec((None, None) + win_t.shape[2:], lambda b: (layer, b, 0, 0, 0, 0)),
                  pl.BlockSpec((None, DEC_ROWS, 2 * KV_WIDTH), lambda b: (b, 0, 0))],
        out_specs=pl.BlockSpec((None, DEC_ROWS, Q_WIDTH), lambda b: (b, 0, 0)),
        out_shape=jax.ShapeDtypeStruct((DB, DEC_ROWS, Q_WIDTH), f32),
        compiler_params=_cparams("parallel"),
        name="nsa_win_decode",
    )(q, win_t, new_kv)


def _rope_tables(pos):
    half = HEAD_DIM // 2
    inv = ROPE_THETA ** (-jnp.arange(half, dtype=f32) / half)
    ang = pos.astype(f32)[:, None] * inv[None, :]
    cos, sin = jnp.cos(ang), jnp.sin(ang)
    reps = V7X_LANES // HEAD_DIM
    return jnp.tile(cos, (1, 2 * reps)), jnp.tile(jnp.concatenate([-sin, sin], axis=1), (1, reps))


def _block_onehot(pos):
    return (jnp.arange(V7X_LANES, dtype=jnp.int32)[None, :] == HEAD_DIM + pos[:, None] // NSA_BLOCK).astype(f32)


def _gate_expand():
    src = jnp.arange(V7X_LANES)[None, :, None]
    head = jnp.arange(Q_WIDTH)[None, None, :] // HEAD_DIM
    j = jnp.arange(3)[:, None, None]
    return (src == head * 3 + j).astype(bf16)


def _compress_page_weights(pe, w):
    w2 = jnp.einsum("lcde,hx->dhlcxe", w, jnp.eye(PAGE_BLOCKS, dtype=f32))
    w2 = w2.reshape(HEAD_DIM, PAGE_SIZE, 2 * PAGE_BLOCKS * HEAD_DIM).astype(bf16)
    pe8 = jnp.broadcast_to(pe.transpose(2, 1, 0)[:, :, None, None, :],
                           (HEAD_DIM, 2, NSA_KV_HEADS, PAGE_BLOCKS, NSA_BLOCK))
    return pe8.reshape(HEAD_DIM, 2 * NSA_KV_HEADS, PAGE_SIZE), w2


def _token_minor(cache):
    n = cache.ndim
    return jnp.transpose(cache, tuple(range(n - 4)) + (n - 3, n - 2, n - 1, n - 4))


def _time_major(x):
    return jnp.swapaxes(x, 0, 1).reshape((x.shape[0] * x.shape[1],) + x.shape[2:])


def _batch_major(x, db):
    return jnp.swapaxes(x.reshape((x.shape[0] // db, db) + x.shape[1:]), 0, 1)


def _gdn_weights(w_in, a_log, dt_bias):
    W3 = 3 * GDN_WIDTH
    pad = V7X_LANES - 2 * GDN_HEADS
    return (w_in[:, :W3].astype(bf16), w_in[:, W3:W3 + GDN_WIDTH].astype(bf16),
            jnp.pad(w_in[:, W3 + GDN_WIDTH:], ((0, 0), (0, pad))).astype(bf16),
            jnp.pad(a_log, (0, V7X_LANES - GDN_HEADS))[None], jnp.pad(dt_bias, (0, V7X_LANES - GDN_HEADS))[None])


def _gdn_layer(yp, ys, S0s, conv_s, nw, w_in, conv_w, a_log, dt_bias, norm_w, w_out, *, B, L, DB, DS):
    wqkv, wz, wab, alog, dtb = _gdn_weights(w_in, a_log, dt_bias)
    wout = w_out.astype(bf16)
    nw = nw[None]
    norm_w = norm_w[None]
    q, k, v, z, slab, conv_p = _gdn_in(yp, None, nw, wqkv, wz, wab, conv_w, alog, dtb, tm=512, shift=1, rows_per_seq=L)
    o, S_p = _gdn_chunk(q, k, v, slab, jnp.zeros((B, GDN_HEADS, GDN_DK, GDN_DK), f32), seq_len=L)
    yp = _gdn_out(yp, o, z, norm_w, wout, tm=512)
    Ts = DS * DB
    hist = _time_major(conv_s).reshape((GDN_CONV - 1) * DB, 3 * GDN_WIDTH)
    q, k, v, z, slab, conv_tm = _gdn_in(ys, hist, nw, wqkv, wz, wab, conv_w, alog, dtb, tm=Ts, shift=DB, rows_per_seq=Ts)

    def chunked(x):
        x = _batch_major(x, DB)
        return jnp.pad(x, ((0, 0), (0, GDN_CHUNK - DS), (0, 0))).reshape(DB * GDN_CHUNK, x.shape[-1])

    o, S_s = _gdn_chunk(chunked(q), chunked(k), chunked(v), chunked(slab), S0s, seq_len=GDN_CHUNK)
    o = _time_major(o.reshape(DB, GDN_CHUNK, GDN_WIDTH)[:, :DS])
    ys = _gdn_out(ys, o, z, norm_w, wout, tm=Ts)
    conv_s_new = _batch_major(conv_tm.reshape((GDN_CONV - 1) * DB, 3 * GDN_WIDTH), DB)
    return yp, ys, S_p, S_s, conv_p, conv_s_new


def _ffn_layer(yp, ys, conv_s, nw, w_up, conv_w, conv_b, w_down, *, B, L, DB, DS):
    wup = w_up.astype(bf16)
    wdn = w_down.astype(bf16)
    nw = nw[None]
    cb = conv_b[None]
    yp, st_p = _ffn(yp, None, nw, wup, conv_w, cb, wdn, tm=512, shift=1, rows_per_seq=L)
    Ts = DS * DB
    hist = _time_major(conv_s).reshape((FFN_CONV - 1) * DB, 2 * D_FF)
    ys, st_tm = _ffn(ys, hist, nw, wup, conv_w, cb, wdn, tm=Ts, shift=DB, rows_per_seq=Ts)
    st_s = _batch_major(st_tm.reshape((FFN_CONV - 1) * DB, 2 * D_FF), DB)
    return yp, ys, st_p, st_s


def _nsa_layer(yp, ys, j, kv_prev, cmp_t, sel_t, win_t, win_buf, page_table, nw, w_in, cmp_pe, cmp_w, w_out,
               *, B, L, DB, DS):
    past = page_table.shape[1] * PAGE_SIZE
    n_nsa = cmp_t.shape[0]
    w = jnp.pad(w_in, ((0, 0), (0, NSA_IN_WP - NSA_IN_W))).astype(bf16)
    wout = w_out.astype(bf16)
    nw = nw[None]
    expand = _gate_expand()
    pe8, w2 = _compress_page_weights(cmp_pe, cmp_w)
    tq = WINDOW // 2
    tq_attn = WINDOW
    pos_p = jnp.arange(L, dtype=jnp.int32)
    cos, sin = _rope_tables(pos_p)
    if kv_prev is None:
        kv_prev = tuple(jnp.zeros((n_nsa, B, 2 * KV_WIDTH, L), f32) for _ in range(3))
    (q_raw, q_rot, cmp_l, sel_l, win_l, ksel, vsel, kwin, vwin, gates) = _nsa_in(
        yp, nw, w, cos, sin, _block_onehot(pos_p), tm=256, tiles_per_seq=L // 256, kv_layers=(j, n_nsa, kv_prev))
    kv_layers = (cmp_l, sel_l, win_l)
    nb = L // NSA_BLOCK
    npg = B * L // PAGE_SIZE
    per_page = PAGE_ROWS // HEAD_DIM
    slabs = cmp_l[j].reshape(B, per_page, HEAD_DIM, L // PAGE_SIZE, PAGE_SIZE).transpose(0, 3, 1, 2, 4)
    ckv = _compress_pages(slabs.reshape(npg * per_page, HEAD_DIM, PAGE_SIZE), pe8, w2, layer=0, n_phys=npg,
                          pages=_pick_tile(npg, 16))
    ck = ckv.reshape(B, L // PAGE_SIZE, 2, NSA_KV_HEADS, PAGE_BLOCKS, HEAD_DIM)
    ck = ck.transpose(0, 1, 4, 2, 3, 5).reshape(B, nb, 2, NSA_KV_HEADS, HEAD_DIM)
    eye = jnp.eye(NSA_GROUP, dtype=f32)
    kbd = jnp.einsum("bkgd,nm->bgnkmd", ck[:, :, 0], eye).reshape(B, NSA_KV_HEADS, NSA_GROUP * nb, NSA_GROUP * HEAD_DIM)
    vbd = jnp.einsum("bkgd,nm->bgnkmd", ck[:, :, 1], eye).reshape(B, NSA_KV_HEADS, NSA_GROUP * nb, NSA_GROUP * HEAD_DIM)
    o_c, bias = _cmp_select(q_raw, kbd.astype(bf16), vbd.astype(bf16), seq_len=L, tq=tq)
    o_s = _attn(q_rot, bias, ksel, vsel, seq_len=L, tq=tq_attn, mode="sel")
    o_w = _attn(q_rot, None, kwin, vwin, seq_len=L, tq=tq_attn, mode="win")
    yp = _nsa_out(yp, o_c, o_s, o_w, gates, expand, wout, tm=512)
    Ts = DS * DB
    pos_s = past + jnp.arange(DS, dtype=jnp.int32)
    pos_rows = jnp.repeat(pos_s, DB)
    cos, sin = _rope_tables(pos_rows)
    (q_raw, q_rot, kv_cmp, kv_sel, kv_win, _, _, _, _, gates) = _nsa_in(
        ys, nw, w, cos, sin, jnp.zeros((Ts, V7X_LANES), f32), tm=Ts, tiles_per_seq=1)
    n_layers, n_phys = cmp_t.shape[:2]
    ckv_all = _compress_pages(cmp_t.reshape(n_layers * n_phys * PAGE_ROWS // HEAD_DIM, HEAD_DIM, PAGE_SIZE), pe8, w2,
                              layer=j, n_phys=n_phys, pages=_pick_tile(n_phys, 16))
    ckv = _page_gather(page_table, ckv_all.reshape(n_phys, 2 * KV_WIDTH * PAGE_BLOCKS))

    def rows8(x):
        return jnp.pad(_batch_major(x, DB), ((0, 0), (0, DEC_ROWS - DS), (0, 0)))

    o_c, bias = _cmp_decode(rows8(q_raw), ckv, past=past)
    q4 = rows8(q_rot).reshape(DB, DEC_ROWS, NSA_KV_HEADS, NSA_GROUP, V7X_LANES)
    q4 = q4.transpose(0, 2, 3, 1, 4).reshape(DB, NSA_KV_HEADS, NSA_GROUP * DEC_ROWS, V7X_LANES)
    o_s = _sel_decode(page_table, q4, bias, rows8(kv_sel), sel_t, layer=j, n_new=DS)
    o_w = _win_decode(q4, win_t, rows8(kv_win), layer=j, n_new=DS)
    ks = (DB, DS, 2, NSA_KV_HEADS, HEAD_DIM)
    cmp_s, sel_s, win_new = (_batch_major(t, DB).reshape(ks) for t in (kv_cmp, kv_sel, kv_win))
    win_s = jnp.concatenate([win_buf[:, DS:], win_new], axis=1)
    tm_rows = lambda o: _time_major(o[:, :DS])
    ys = _nsa_out(ys, tm_rows(o_c), tm_rows(o_s), tm_rows(o_w), gates, expand, wout, tm=Ts)
    return yp, ys, kv_layers, cmp_s, sel_s, win_s


def kernel(x_prompt, x_sample, state_gdn, state_gdn_conv, cache_cmp, cache_sel, state_win, state_ffn_conv, page_table,
           norm_mix, norm_ffn, norm_final, gdn_w_in, gdn_conv_w, gdn_a_log, gdn_dt_bias, gdn_norm_w, gdn_w_out,
           nsa_w_in, nsa_cmp_pe, nsa_cmp_w, nsa_w_out, ffn_w_up, ffn_conv_w, ffn_conv_b, ffn_w_down):
    B, L, _ = x_prompt.shape
    DB, DS, _ = x_sample.shape
    depth = norm_mix.shape[0]
    dims = dict(B=B, L=L, DB=DB, DS=DS)
    yp = x_prompt.reshape(B * L, D_MODEL)
    ys = _time_major(x_sample)
    outs = [[] for _ in range(12)]
    kv_p = None
    cmp_t, sel_t, win_t = _token_minor(cache_cmp), _token_minor(cache_sel), _token_minor(state_win)
    for i in range(depth):
        j = i // 2
        if i % 2 == 0:
            yp, ys, S_p, S_s, c_p, c_s = _gdn_layer(yp, ys, state_gdn[j], state_gdn_conv[j], norm_mix[i], gdn_w_in[j],
                                                    gdn_conv_w[j], gdn_a_log[j], gdn_dt_bias[j], gdn_norm_w[j],
                                                    gdn_w_out[j], **dims)
            for lst, val in zip(outs[0:4], (S_p, S_s, c_p, c_s)):
                lst.append(val)
        else:
            yp, ys, kv_p, *kv_s = _nsa_layer(yp, ys, j, kv_p, cmp_t, sel_t, win_t, state_win[j], page_table,
                                             norm_mix[i], nsa_w_in[j], nsa_cmp_pe[j], nsa_cmp_w[j], nsa_w_out[j], **dims)
            for lst, val in zip((outs[5], outs[7], outs[9]), kv_s):
                lst.append(val)
        yp, ys, f_p, f_s = _ffn_layer(yp, ys, state_ffn_conv[i], norm_ffn[i], ffn_w_up[i], ffn_conv_w[i],
                                      ffn_conv_b[i], ffn_w_down[i], **dims)
        outs[10].append(f_p)
        outs[11].append(f_s)
    y_prompt = _final_norm(yp, norm_final[None], tm=512).reshape(B, L, D_MODEL)
    y_sample = _batch_major(_final_norm(ys, norm_final[None], tm=DS * DB), DB)
    rows_of = lambda t: jnp.transpose(t.reshape(t.shape[:2] + (2, NSA_KV_HEADS, HEAD_DIM, t.shape[-1])), (0, 1, 5, 2, 3, 4))
    cmp_p, sel_p, win_p = rows_of(kv_p[0]), rows_of(kv_p[1]), rows_of(kv_p[2][..., L - min(WINDOW, L):])
    stacked = [jnp.stack(o) if o else None for o in outs]
    stacked[4], stacked[6], stacked[8] = cmp_p, sel_p, win_p
    return (y_prompt, y_sample) + tuple(stacked)
```

```python
import functools
import math

import jax
import jax.numpy as jnp
from jax import lax
from jax.experimental import pallas as pl
from jax.experimental.pallas import tpu as pltpu

f32 = jnp.float32
bf16 = jnp.bfloat16

D_MODEL = 1024
GDN_HEADS = 8
GDN_DK = 128
GDN_WIDTH = GDN_HEADS * GDN_DK
GDN_CONV = 4
GDN_CHUNK = 64
NSA_HEADS = 16
NSA_KV_HEADS = 4
NSA_GROUP = 4
HEAD_DIM = 64
Q_WIDTH = NSA_HEADS * HEAD_DIM
KV_WIDTH = NSA_KV_HEADS * HEAD_DIM
NSA_BLOCK = 64
N_SEL = 16
WINDOW = 512
PAGE_SIZE = 128
ROPE_THETA = 10000.0
ATTN_SCALE = HEAD_DIM ** -0.5
LOG2E = math.log2(math.e)
D_FF = 2816
FFN_CONV = 3
RMS_EPS = 1e-6
NEG = -1e30

V7X_LANES = 128
V7X_SUBLANES = 8
V7X_VMEM_BYTES = 64 * 1024 * 1024
VMEM_LIMIT = 56 * 1024 * 1024


def _cparams(*sem):
    return pltpu.CompilerParams(dimension_semantics=sem, vmem_limit_bytes=VMEM_LIMIT)


def _round_up(x, m):
    return (x + m - 1) // m * m


def _pick_tile(n, cap):
    best = None
    for t in range(V7X_SUBLANES, min(n, cap) + 1, V7X_SUBLANES):
        if n % t == 0:
            best = t
    assert best is not None, (n, cap)
    return best


def _dot(a, b):
    return jnp.dot(a.astype(bf16), b.astype(bf16), preferred_element_type=f32)


def _dot_nt(a, b):
    return lax.dot_general(a.astype(bf16), b.astype(bf16), (((1,), (1,)), ((), ())), preferred_element_type=f32)


def _dot_tn(a, b):
    return lax.dot_general(a.astype(bf16), b.astype(bf16), (((0,), (0,)), ((), ())), preferred_element_type=f32)


def _split3(x):
    hi = x.astype(bf16)
    r = x - hi.astype(f32)
    mid = r.astype(bf16)
    lo = (r - mid.astype(f32)).astype(bf16)
    return hi, mid, lo


def _dot_exact_lhs(a01, x):
    a = a01.astype(bf16)
    hi, mid, lo = _split3(x)
    return (jnp.dot(a, hi, preferred_element_type=f32) + jnp.dot(a, mid, preferred_element_type=f32)
            + jnp.dot(a, lo, preferred_element_type=f32))


def _silu(x):
    return x / (1.0 + jnp.exp(-x))


def _sigmoid(x):
    return 1.0 / (1.0 + jnp.exp(-x))


def _rms_rows(x, w):
    return x * lax.rsqrt(jnp.mean(x * x, axis=-1, keepdims=True) + RMS_EPS) * w


def _ffn_kernel(*refs, tm, shift, hist_rows, has_hist, tiles_per_seq, n_chunk):
    if has_hist:
        y_ref, hist_ref, nw_ref, wup_ref, cw_ref, cb_ref, wdn_ref, out_ref, st_ref, ext_ref, g_ref = refs
    else:
        y_ref, nw_ref, wup_ref, cw_ref, cb_ref, wdn_ref, out_ref, st_ref, ext_ref, g_ref = refs
        hist_ref = None
    H = hist_rows
    s = shift
    LN = V7X_LANES
    nblk = 2 * D_FF // LN
    t = pl.program_id(0) % tiles_per_seq

    @pl.when(t == 0)
    def _():
        if has_hist:
            for c in range(nblk):
                ext_ref[c, H - 2 * s:H, :] = hist_ref[:, c * LN:(c + 1) * LN]
        else:
            ext_ref[:, 0:H, :] = jnp.zeros((nblk, H, LN), f32)

    @pl.when(t != 0)
    def _():
        ext_ref[:, H - 2 * s:H, :] = ext_ref[:, H + tm - 2 * s:H + tm, :]

    y = y_ref[...]
    h = _rms_rows(y, nw_ref[...]).astype(bf16)
    per = n_chunk // LN
    for c in range(2 * D_FF // n_chunk):
        u = jnp.dot(h, wup_ref[:, c * n_chunk:(c + 1) * n_chunk], preferred_element_type=f32)
        for k in range(per):
            ext_ref[c * per + k, H:H + tm, :] = u[:, k * LN:(k + 1) * LN]
    for c in range(nblk):
        st_ref[:, c * LN:(c + 1) * LN] = ext_ref[c, H + tm - 2 * s:H + tm, :]

    def conv(c):
        w = cw_ref[:, c * LN:(c + 1) * LN]
        return (ext_ref[c, H - 2 * s:H - 2 * s + tm, :] * w[0:1, :] + ext_ref[c, H - s:H - s + tm, :] * w[1:2, :]
                + ext_ref[c, H:H + tm, :] * w[2:3, :] + cb_ref[:, c * LN:(c + 1) * LN])

    for j in range(D_FF // LN):
        g_ref[:, j * LN:(j + 1) * LN] = (_silu(conv(j)) * conv(D_FF // LN + j)).astype(bf16)
    out_ref[...] = y + jnp.dot(g_ref[...], wdn_ref[...], preferred_element_type=f32)


def _ffn(y, hist, nw, wup, cw, cb, wdn, *, tm, shift, rows_per_seq):
    T = y.shape[0]
    n_seq = T // rows_per_seq
    tiles_per_seq = rows_per_seq // tm
    H = _round_up(2 * shift, V7X_SUBLANES)
    kern = functools.partial(_ffn_kernel, tm=tm, shift=shift, hist_rows=H, has_hist=hist is not None,
                             tiles_per_seq=tiles_per_seq, n_chunk=512)
    const = lambda i: (0, 0)
    in_specs = [pl.BlockSpec((tm, D_MODEL), lambda i: (i, 0))]
    args = [y]
    if hist is not None:
        in_specs.append(pl.BlockSpec((2 * shift, 2 * D_FF), lambda i: (i // tiles_per_seq, 0)))
        args.append(hist)
    in_specs += [pl.BlockSpec((1, D_MODEL), const),
                 pl.BlockSpec((D_MODEL, 2 * D_FF), const, pipeline_mode=pl.Buffered(1)),
                 pl.BlockSpec((FFN_CONV, 2 * D_FF), const),
                 pl.BlockSpec((1, 2 * D_FF), const),
                 pl.BlockSpec((D_FF, D_MODEL), const, pipeline_mode=pl.Buffered(1))]
    args += [nw, wup, cw, cb, wdn]
    out, st = pl.pallas_call(
        kern,
        grid=(T // tm,),
        in_specs=in_specs,
        out_specs=[pl.BlockSpec((tm, D_MODEL), lambda i: (i, 0)),
                   pl.BlockSpec((None, 2 * shift, 2 * D_FF), lambda i: (i // tiles_per_seq, 0, 0))],
        out_shape=[jax.ShapeDtypeStruct((T, D_MODEL), f32),
                   jax.ShapeDtypeStruct((n_seq, 2 * shift, 2 * D_FF), f32)],
        scratch_shapes=[pltpu.VMEM((2 * D_FF // V7X_LANES, H + tm, V7X_LANES), f32), pltpu.VMEM((tm, D_FF), bf16)],
        compiler_params=_cparams("arbitrary"),
        name="conv_ffn",
    )(*args)
    return out, st


def _gdn_in_kernel(*refs, tm, shift, hist_rows, has_hist, tiles_per_seq):
    if has_hist:
        (y_ref, hist_ref, nw_ref, wqkv_ref, wz_ref, wab_ref, cw_ref, alog_ref, dtb_ref,
         q_ref, k_ref, v_ref, z_ref, slab_ref, st_ref, ext_ref) = refs
    else:
        (y_ref, nw_ref, wqkv_ref, wz_ref, wab_ref, cw_ref, alog_ref, dtb_ref,
         q_ref, k_ref, v_ref, z_ref, slab_ref, st_ref, ext_ref) = refs
        hist_ref = None
    H = hist_rows
    s = shift
    nh = GDN_CONV - 1
    W3 = 3 * GDN_WIDTH
    t = pl.program_id(0) % tiles_per_seq
    hw = GDN_DK
    nblk = W3 // hw

    @pl.when(t == 0)
    def _():
        if has_hist:
            for c in range(nblk):
                ext_ref[c, H - nh * s:H, :] = hist_ref[:, c * hw:(c + 1) * hw]
        else:
            ext_ref[:, 0:H, :] = jnp.zeros((nblk, H, hw), f32)

    @pl.when(t != 0)
    def _():
        ext_ref[:, H - nh * s:H, :] = ext_ref[:, H + tm - nh * s:H + tm, :]

    h = _rms_rows(y_ref[...], nw_ref[...]).astype(bf16)
    nc = 512
    for c in range(W3 // nc):
        u = jnp.dot(h, wqkv_ref[:, c * nc:(c + 1) * nc], preferred_element_type=f32)
        for k in range(nc // hw):
            ext_ref[c * (nc // hw) + k, H:H + tm, :] = u[:, k * hw:(k + 1) * hw]
    for c in range(GDN_WIDTH // nc):
        z_ref[:, c * nc:(c + 1) * nc] = jnp.dot(h, wz_ref[:, c * nc:(c + 1) * nc], preferred_element_type=f32)
    ab = jnp.dot(h, wab_ref[...], preferred_element_type=f32)
    x = ab + dtb_ref[...]
    softplus = jnp.maximum(x, 0.0) + jnp.log(1.0 + jnp.exp(-jnp.abs(x)))
    g = -jnp.exp(alog_ref[...]) * softplus
    lane = lax.broadcasted_iota(jnp.int32, ab.shape, 1)
    slab_ref[...] = jnp.where(lane < GDN_HEADS, g, _sigmoid(ab))
    for c in range(nblk):
        st_ref[:, c * hw:(c + 1) * hw] = ext_ref[c, H + tm - nh * s:H + tm, :]

    for j in range(nblk):
        col = j * hw
        w = cw_ref[:, col:col + hw]
        c = ext_ref[j, H:H + tm, :] * w[nh:nh + 1, :]
        for i in range(nh):
            c = c + ext_ref[j, H - (nh - i) * s:H - (nh - i) * s + tm, :] * w[i:i + 1, :]
        c = _silu(c)
        if j < 2 * GDN_HEADS:
            c = c * lax.rsqrt(jnp.sum(c * c, axis=-1, keepdims=True) + RMS_EPS)
        if j < GDN_HEADS:
            q_ref[:, col:col + hw] = c * (GDN_DK ** -0.5)
        elif j < 2 * GDN_HEADS:
            k_ref[:, col - GDN_WIDTH:col - GDN_WIDTH + hw] = c
        else:
            v_ref[:, col - 2 * GDN_WIDTH:col - 2 * GDN_WIDTH + hw] = c


def _gdn_in(y, hist, nw, wqkv, wz, wab, cw, alog, dtb, *, tm, shift, rows_per_seq):
    T = y.shape[0]
    n_seq = T // rows_per_seq
    tiles_per_seq = rows_per_seq // tm
    nh = GDN_CONV - 1
    H = _round_up(nh * shift, V7X_SUBLANES)
    W3 = 3 * GDN_WIDTH
    kern = functools.partial(_gdn_in_kernel, tm=tm, shift=shift, hist_rows=H, has_hist=hist is not None,
                             tiles_per_seq=tiles_per_seq)
    const = lambda i: (0, 0)
    row = lambda i: (i, 0)
    in_specs = [pl.BlockSpec((tm, D_MODEL), row)]
    args = [y]
    if hist is not None:
        in_specs.append(pl.BlockSpec((nh * shift, W3), lambda i: (i // tiles_per_seq, 0)))
        args.append(hist)
    in_specs += [pl.BlockSpec((1, D_MODEL), const),
                 pl.BlockSpec((D_MODEL, W3), const, pipeline_mode=pl.Buffered(1)),
                 pl.BlockSpec((D_MODEL, GDN_WIDTH), const, pipeline_mode=pl.Buffered(1)),
                 pl.BlockSpec((D_MODEL, V7X_LANES), const),
                 pl.BlockSpec((GDN_CONV, W3), const),
                 pl.BlockSpec((1, V7X_LANES), const),
                 pl.BlockSpec((1, V7X_LANES), const)]
    args += [nw, wqkv, wz, wab, cw, alog, dtb]
    wide = pl.BlockSpec((tm, GDN_WIDTH), row)
    return pl.pallas_call(
        kern,
        grid=(T // tm,),
        in_specs=in_specs,
        out_specs=[wide, wide, wide, wide, pl.BlockSpec((tm, V7X_LANES), row),
                   pl.BlockSpec((None, nh * shift, W3), lambda i: (i // tiles_per_seq, 0, 0))],
        out_shape=[jax.ShapeDtypeStruct((T, GDN_WIDTH), f32)] * 4
        + [jax.ShapeDtypeStruct((T, V7X_LANES), f32), jax.ShapeDtypeStruct((n_seq, nh * shift, W3), f32)],
        scratch_shapes=[pltpu.VMEM((W3 // GDN_DK, H + tm, GDN_DK), f32)],
        compiler_params=_cparams("arbitrary"),
        name="gdn_in",
    )(*args)


def _gdn_chunk_kernel(q_ref, k_ref, v_ref, slab_ref, s0_ref, o_ref, s_ref, *, nseq):
    C = GDN_CHUNK
    H = GDN_HEADS
    n = pl.program_id(1)

    @pl.when(n == 0)
    def _():
        s_ref[...] = s0_ref[...]

    ri = lax.broadcasted_iota(jnp.int32, (C, C), 0)
    ci = lax.broadcasted_iota(jnp.int32, (C, C), 1)
    tril = ri >= ci
    strict = ri > ci
    eye = (ri == ci).astype(f32)
    slab = [slab_ref[s] for s in range(nseq)]
    G = [_dot_exact_lhs(tril.astype(f32), slab[s]) for s in range(nseq)]
    GT = [G[s].T for s in range(nseq)]
    ch = [(s, h) for s in range(nseq) for h in range(H)]
    idx = range(len(ch))
    col = lambda h: slice(h * GDN_DK, (h + 1) * GDN_DK)
    qh = [q_ref[s, :, col(h)] for s, h in ch]
    kh = [k_ref[s, :, col(h)] for s, h in ch]
    vh = [v_ref[s, :, col(h)] for s, h in ch]
    Gc = [G[s][:, h:h + 1] for s, h in ch]
    bc = [slab[s][:, H + h:H + h + 1] for s, h in ch]
    gl = [G[s][C - 1:C, h:h + 1] for s, h in ch]
    decay = [jnp.exp(jnp.where(tril, Gc[i] - GT[s][h:h + 1, :], NEG)) for i, (s, h) in enumerate(ch)]
    kb = [kh[i].astype(bf16) for i in idx]
    A = [jnp.where(strict, bc[i] * _dot_nt(kb[i], kb[i]) * decay[i], 0.0) for i in idx]
    P = [eye - A[i] for i in idx]
    Ap = A
    for _ in range(int(math.log2(C)) - 1):
        Ap = [_dot(Ap[i], Ap[i]) for i in idx]
        P = [P[i] + _dot(P[i], Ap[i]) for i in idx]
    Tb = [P[i].astype(bf16) for i in idx]
    uv = [_dot(Tb[i], vh[i] * bc[i]) for i in idx]
    wk = [_dot(Tb[i], kh[i] * (bc[i] * jnp.exp(Gc[i]))) for i in idx]
    qk = [_dot_nt(qh[i], kb[i]) * decay[i] for i in idx]
    kdec = [kh[i] * jnp.exp(gl[i] - Gc[i]) for i in idx]
    S = [s_ref[s, h] for s, h in ch]
    Sb = [S[i].astype(bf16) for i in idx]
    u = [uv[i] - _dot(wk[i], Sb[i]) for i in idx]
    o = [_dot(qh[i] * jnp.exp(Gc[i]), Sb[i]) + _dot(qk[i], u[i]) for i in idx]
    Sn = [S[i] * jnp.exp(gl[i]) + _dot_tn(kdec[i], u[i]) for i in idx]
    for i, (s, h) in enumerate(ch):
        o_ref[s, :, col(h)] = o[i]
        s_ref[s, h] = Sn[i]


GDN_SEQS_PER_STEP = 8


def _gdn_chunk(q, k, v, slab, s0, *, seq_len):
    T = q.shape[0]
    B = T // seq_len
    N = seq_len // GDN_CHUNK
    nseq = math.gcd(B, GDN_SEQS_PER_STEP)
    blk = lambda w: pl.BlockSpec((nseq, GDN_CHUNK, w), lambda b, n: (b, n, 0))
    st = pl.BlockSpec((nseq, GDN_HEADS, GDN_DK, GDN_DK), lambda b, n: (b, 0, 0, 0))
    seqs = lambda x: x.reshape(B, seq_len, x.shape[-1])
    o, s_out = pl.pallas_call(
        functools.partial(_gdn_chunk_kernel, nseq=nseq),
        grid=(B // nseq, N),
        in_specs=[blk(GDN_WIDTH), blk(GDN_WIDTH), blk(GDN_WIDTH), blk(V7X_LANES), st],
        out_specs=[blk(GDN_WIDTH), st],
        out_shape=[jax.ShapeDtypeStruct((B, seq_len, GDN_WIDTH), f32),
                   jax.ShapeDtypeStruct((B, GDN_HEADS, GDN_DK, GDN_DK), f32)],
        compiler_params=_cparams("parallel", "arbitrary"),
        name="gdn_chunk",
    )(seqs(q), seqs(k), seqs(v), seqs(slab), s0)
    return o.reshape(T, GDN_WIDTH), s_out


def _gdn_out_kernel(y_ref, o_ref, z_ref, nw_ref, wout_ref, out_ref, g_ref):
    for h in range(GDN_HEADS):
        lo = h * GDN_DK
        o = _rms_rows(o_ref[:, lo:lo + GDN_DK], nw_ref[...])
        g_ref[:, lo:lo + GDN_DK] = (o * _silu(z_ref[:, lo:lo + GDN_DK])).astype(bf16)
    out_ref[...] = y_ref[...] + jnp.dot(g_ref[...], wout_ref[...], preferred_element_type=f32)


def _gdn_out(y, o, z, nw, wout, *, tm):
    T = y.shape[0]
    row = lambda i: (i, 0)
    const = lambda i: (0, 0)
    wide = pl.BlockSpec((tm, GDN_WIDTH), row)
    return pl.pallas_call(
        _gdn_out_kernel,
        grid=(T // tm,),
        in_specs=[pl.BlockSpec((tm, D_MODEL), row), wide, wide, pl.BlockSpec((1, GDN_DK), const),
                  pl.BlockSpec((GDN_WIDTH, D_MODEL), const)],
        out_specs=pl.BlockSpec((tm, D_MODEL), row),
        out_shape=jax.ShapeDtypeStruct((T, D_MODEL), f32),
        scratch_shapes=[pltpu.VMEM((tm, GDN_WIDTH), bf16)],
        compiler_params=_cparams("parallel"),
        name="gdn_out",
    )(y, o, z, nw, wout)


NSA_IN_W = Q_WIDTH + 6 * KV_WIDTH + 3 * NSA_HEADS
NSA_IN_WP = _round_up(NSA_IN_W, V7X_LANES)
KV0 = Q_WIDTH
GATE0 = Q_WIDTH + 6 * KV_WIDTH


def _lane_iota(shape):
    return lax.broadcasted_iota(jnp.int32, shape, 1)


def _rope_pair(x, cos, sin):
    half = HEAD_DIM // 2
    lane = _lane_iota(x.shape)
    rot = jnp.where(lane % HEAD_DIM < half, pltpu.roll(x, V7X_LANES - half, 1), pltpu.roll(x, half, 1))
    return x * cos + rot * sin


def _split_pair(x, fill):
    lane = _lane_iota(x.shape)
    return jnp.where(lane < HEAD_DIM, x, fill), jnp.where(lane < HEAD_DIM, pltpu.roll(x, HEAD_DIM, 1), fill)


def _nsa_in_kernel(*refs, token_minor, n_alias):
    y_ref, nw_ref, w_ref, cos_ref, sin_ref, oh_ref = refs[:6]
    (qraw_ref, qrot_ref, cmp_ref, sel_ref, win_ref, ksel_ref, vsel_ref, kwin_ref, vwin_ref, gate_ref,
     proj_ref) = refs[6 + n_alias:]

    def put(o_ref, lo, x):
        if token_minor:
            o_ref[lo:lo + V7X_LANES, :] = x.T
        else:
            o_ref[:, lo:lo + V7X_LANES] = x

    h = _rms_rows(y_ref[...], nw_ref[...]).astype(bf16)
    nc = NSA_IN_WP // 3
    for c in range(3):
        proj_ref[:, c * nc:(c + 1) * nc] = jnp.dot(h, w_ref[:, c * nc:(c + 1) * nc], preferred_element_type=f32)
    cos = cos_ref[...]
    sin = sin_ref[...]
    oh = oh_ref[...]
    L2 = 2 * V7X_LANES
    for p in range(Q_WIDTH // V7X_LANES):
        x = proj_ref[:, p * V7X_LANES:(p + 1) * V7X_LANES]
        qraw_ref[:, p * V7X_LANES:(p + 1) * V7X_LANES] = x.astype(bf16)
        a, b = _split_pair(_rope_pair(x, cos, sin) * (ATTN_SCALE * LOG2E), 0.0)
        qrot_ref[:, p * L2:p * L2 + V7X_LANES] = a.astype(bf16)
        qrot_ref[:, p * L2 + V7X_LANES:(p + 1) * L2] = b.astype(bf16)
    for p in range(2 * KV_WIDTH // V7X_LANES):
        put(cmp_ref, p * V7X_LANES, proj_ref[:, KV0 + p * V7X_LANES:KV0 + (p + 1) * V7X_LANES])
    for br, (o_ref, k_ref, v_ref, fill) in enumerate(((sel_ref, ksel_ref, vsel_ref, oh), (win_ref, kwin_ref, vwin_ref, 0.0))):
        base = KV0 + (br + 1) * 2 * KV_WIDTH
        for p in range(KV_WIDTH // V7X_LANES):
            lo = p * V7X_LANES
            kx = _rope_pair(proj_ref[:, base + lo:base + lo + V7X_LANES], cos, sin)
            put(o_ref, lo, kx)
            a, b = _split_pair(kx, fill)
            k_ref[:, p * L2:p * L2 + V7X_LANES] = a.astype(bf16)
            k_ref[:, p * L2 + V7X_LANES:(p + 1) * L2] = b.astype(bf16)
            vx = proj_ref[:, base + KV_WIDTH + lo:base + KV_WIDTH + lo + V7X_LANES]
            put(o_ref, KV_WIDTH + lo, vx)
            a, b = _split_pair(vx, 1.0)
            v_ref[:, p * L2:p * L2 + V7X_LANES] = a.astype(bf16)
            v_ref[:, p * L2 + V7X_LANES:(p + 1) * L2] = b.astype(bf16)
    gate_ref[...] = _sigmoid(proj_ref[:, GATE0:GATE0 + V7X_LANES])


def _nsa_in(y, nw, w, cos, sin, onehot, *, tm, tiles_per_seq, kv_layers=None):
    T = y.shape[0]
    row = lambda i: (i, 0)
    const = lambda i: (0, 0)
    tab = lambda i: (i % tiles_per_seq, 0)
    P4 = NSA_KV_HEADS * V7X_LANES
    W = 2 * KV_WIDTH
    blk = lambda w_: pl.BlockSpec((tm, w_), row)
    shp = lambda w_, dt: jax.ShapeDtypeStruct((T, w_), dt)
    args = [y, nw, w, cos, sin, onehot]
    in_specs = [blk(D_MODEL), pl.BlockSpec((1, D_MODEL), const),
                pl.BlockSpec((D_MODEL, NSA_IN_WP), const, pipeline_mode=pl.Buffered(1)),
                pl.BlockSpec((tm, V7X_LANES), tab), pl.BlockSpec((tm, V7X_LANES), tab),
                pl.BlockSpec((tm, V7X_LANES), tab)]
    aliases = {}
    if kv_layers is None:
        kv_specs = [blk(W)] * 3
        kv_shapes = [shp(W, f32)] * 3
    else:
        layer, n_layers, prev = kv_layers
        n_seq = T // (tm * tiles_per_seq)
        kv_specs = [pl.BlockSpec((None, None, W, tm), lambda i: (layer, i // tiles_per_seq, 0, i % tiles_per_seq))] * 3
        kv_shapes = [jax.ShapeDtypeStruct((n_layers, n_seq, W, tm * tiles_per_seq), f32)] * 3
        if prev is not None:
            aliases = {len(args) + k: 2 + k for k in range(3)}
            args += list(prev)
            in_specs += [pl.BlockSpec(memory_space=pl.ANY)] * 3
    return pl.pallas_call(
        functools.partial(_nsa_in_kernel, token_minor=kv_layers is not None, n_alias=len(aliases)),
        grid=(T // tm,),
        in_specs=in_specs,
        out_specs=[blk(Q_WIDTH), blk(2 * Q_WIDTH)] + kv_specs + [blk(P4), blk(P4), blk(P4), blk(P4), blk(V7X_LANES)],
        out_shape=[shp(Q_WIDTH, bf16), shp(2 * Q_WIDTH, bf16)] + kv_shapes
        + [shp(P4, bf16), shp(P4, bf16), shp(P4, bf16), shp(P4, bf16), shp(V7X_LANES, f32)],
        scratch_shapes=[pltpu.VMEM((tm, NSA_IN_WP), f32)],
        input_output_aliases=aliases,
        compiler_params=_cparams("parallel"),
        name="nsa_in",
    )(*args)


def _cmp_select_kernel(q_ref, kbd_ref, vbd_ref, oc_ref, bias_ref, *, tq, nb, n_top):
    i = pl.program_id(1)
    G = NSA_GROUP
    gw = G * HEAD_DIM
    blk = lax.broadcasted_iota(jnp.int32, (nb, tq), 0)
    pos = i * tq + lax.broadcasted_iota(jnp.int32, (nb, tq), 1)
    complete = (blk + 1) * NSA_BLOCK - 1 <= pos
    cur = lax.shift_right_logical(pos, int(math.log2(NSA_BLOCK)))
    cand = blk < cur
    for g in range(NSA_KV_HEADS):
        sT = _dot_nt(kbd_ref[g], q_ref[:, g * gw:(g + 1) * gw]) * ATTN_SCALE
        ps = []
        imp = jnp.zeros((nb, tq), f32)
        for n in range(G):
            sm = jnp.where(complete, sT[n * nb:(n + 1) * nb], NEG)
            mx = jnp.max(sm, axis=0, keepdims=True)
            e = jnp.where(complete, jnp.exp(sm - mx), 0.0)
            den = jnp.sum(e, axis=0, keepdims=True)
            p = e / jnp.where(den > 0.0, den, 1.0)
            ps.append(p)
            imp = imp + p
        oc_ref[:, g * gw:(g + 1) * gw] = _dot_tn(jnp.concatenate(ps, axis=0), vbd_ref[g])
        vals = jnp.where(cand, imp, -1.0)
        cnt = jnp.zeros((nb, tq), jnp.int32)
        for j in range(nb):
            vj = vals[j:j + 1, :]
            cnt = cnt + ((vj > vals) | ((vj == vals) & (blk > j))).astype(jnp.int32)
        sel = (cand & (cnt < n_top)) | (blk == cur)
        biasT = jnp.where(sel, 0.0, NEG)
        full = jnp.concatenate([jnp.zeros((HEAD_DIM, tq), f32), biasT,
                                jnp.zeros((V7X_LANES - HEAD_DIM - nb, tq), f32)], axis=0)
        bias_ref[:, g * V7X_LANES:(g + 1) * V7X_LANES] = full.T.astype(bf16)


def _cmp_select(q_raw, kbd, vbd, *, seq_len, tq):
    T = q_raw.shape[0]
    nq = seq_len // tq
    nb = seq_len // NSA_BLOCK
    row = lambda b, i: (b * nq + i, 0)
    kern = functools.partial(_cmp_select_kernel, tq=tq, nb=nb, n_top=min(N_SEL - 1, nb))
    return pl.pallas_call(
        kern,
        grid=(T // seq_len, nq),
        in_specs=[pl.BlockSpec((tq, Q_WIDTH), row),
                  pl.BlockSpec((None,) + kbd.shape[1:], lambda b, i: (b, 0, 0, 0)),
                  pl.BlockSpec((None,) + vbd.shape[1:], lambda b, i: (b, 0, 0, 0))],
        out_specs=[pl.BlockSpec((tq, Q_WIDTH), row), pl.BlockSpec((tq, NSA_KV_HEADS * V7X_LANES), row)],
        out_shape=[jax.ShapeDtypeStruct((T, Q_WIDTH), f32),
                   jax.ShapeDtypeStruct((T, NSA_KV_HEADS * V7X_LANES), bf16)],
        compiler_params=_cparams("parallel", "parallel"),
        name="nsa_cmp_select",
    )(q_raw, kbd, vbd)


def _attn_kernel(*refs, tq, mode):
    if mode == "sel":
        q_ref, bias_ref, k_ref, v_ref, o_ref, mx_ref, acc_ref, s_ref = refs
    else:
        q_ref, k_ref, v_ref, o_ref, mx_ref, acc_ref, s_ref = refs
    i = pl.program_id(2)
    G = NSA_GROUP
    hq = tq // 2
    R2 = G * hq
    qn = [q_ref[:, n * V7X_LANES:(n + 1) * V7X_LANES] for n in range(G)]
    if mode == "sel":
        qn = [q + bias_ref[...] for q in qn]
    qa = jnp.concatenate([q[h * hq:(h + 1) * hq] for h in range(2) for q in qn], axis=0)
    rows = [slice(h * R2, (h + 1) * R2) for h in range(2)]
    q_in = lax.broadcasted_iota(jnp.int32, (R2, hq), 0) % hq
    k_in = lax.broadcasted_iota(jnp.int32, (R2, hq), 1)
    causal = ((0, 0, k_in <= q_in), (1, 0, None), (1, 1, k_in <= q_in))
    window = ((0, 0, k_in > q_in), (0, 1, None), (1, 1, k_in > q_in))
    nback = WINDOW // tq
    back_tiles = () if mode == "sel" else tuple(range(1, nback + 1))

    def slot_of(t):
        return t if mode == "sel" else i - t

    def cols(t, kh):
        return pl.ds(pl.multiple_of(slot_of(t) * tq + kh * hq, hq), hq)

    def keys(ref, t, kh):
        return ref[pl.ds(pl.multiple_of(t * tq + kh * hq, hq), hq), :]

    def first_pass_full(t):
        s = _dot_nt(qa, k_ref[pl.ds(pl.multiple_of(t * tq, tq), tq), :])
        s_ref[:, pl.ds(pl.multiple_of(slot_of(t) * tq, tq), tq)] = s
        mx_ref[...] = jnp.maximum(mx_ref[...], s)

    def first_pass_blocks(t, blocks, first=False):
        for qh, kh, mask in blocks:
            s = _dot_nt(qa[rows[qh]], keys(k_ref, t, kh))
            if mask is not None:
                s = jnp.where(mask, s, NEG)
            s_ref[rows[qh], cols(t, kh)] = s
            here = (rows[qh], slice(kh * hq, (kh + 1) * hq))
            mx_ref[here] = s if first else jnp.maximum(mx_ref[here], s)

    mx_ref[rows[0], hq:tq] = jnp.full((R2, hq), NEG, f32)
    first_pass_blocks(i, causal, first=True)
    if mode == "sel":
        def max_body(t, c):
            first_pass_full(t)
            return c
        lax.fori_loop(0, i, max_body, 0)
    else:
        for back in back_tiles:
            @pl.when(i >= back)
            def _(back=back):
                if back == nback:
                    first_pass_blocks(i - back, window)
                else:
                    first_pass_full(i - back)
    mx_ref[...] = jnp.broadcast_to(jnp.max(mx_ref[...], axis=-1, keepdims=True), mx_ref.shape)

    def accumulate_full(t):
        s = s_ref[:, pl.ds(pl.multiple_of(slot_of(t) * tq, tq), tq)]
        p = jnp.exp2(s - mx_ref[...]).astype(bf16)
        acc_ref[...] += jnp.dot(p, v_ref[pl.ds(pl.multiple_of(t * tq, tq), tq), :], preferred_element_type=f32)

    def accumulate_blocks(t, blocks, first=False):
        seen = set()
        for qh, kh, _ in blocks:
            p = jnp.exp2(s_ref[rows[qh], cols(t, kh)] - mx_ref[rows[qh], 0:hq]).astype(bf16)
            pv = jnp.dot(p, keys(v_ref, t, kh), preferred_element_type=f32)
            if first and qh not in seen:
                acc_ref[rows[qh], :] = pv
            else:
                acc_ref[rows[qh], :] += pv
            seen.add(qh)

    accumulate_blocks(i, causal, first=True)
    if mode == "sel":
        def acc_body(t, c):
            accumulate_full(t)
            return c
        lax.fori_loop(0, i, acc_body, 0)
    else:
        for back in back_tiles:
            @pl.when(i >= back)
            def _(back=back):
                if back == nback:
                    accumulate_blocks(i - back, window)
                else:
                    accumulate_full(i - back)

    acc = acc_ref[...]
    o = acc / pltpu.roll(acc, HEAD_DIM, 1)
    lane = _lane_iota((hq, V7X_LANES))
    for h in range(2):
        for p in range(G // 2):
            a = o[h * R2 + (2 * p) * hq:h * R2 + (2 * p + 1) * hq]
            b = o[h * R2 + (2 * p + 1) * hq:h * R2 + (2 * p + 2) * hq]
            o_ref[h * hq:(h + 1) * hq, p * V7X_LANES:(p + 1) * V7X_LANES] = jnp.where(
                lane < HEAD_DIM, a, pltpu.roll(b, HEAD_DIM, 1))


def _attn(q_pad, bias, k_pad, v_pad, *, seq_len, tq, mode):
    assert mode in ("sel", "win") and WINDOW % tq == 0
    T = q_pad.shape[0]
    B = T // seq_len
    nq = seq_len // tq
    G = NSA_GROUP
    qrow = lambda b, g, i: (b * nq + i, g)
    kv = lambda b, g, i: (b, g)
    in_specs = [pl.BlockSpec((tq, G * V7X_LANES), qrow)]
    args = [q_pad]
    if mode == "sel":
        in_specs.append(pl.BlockSpec((tq, V7X_LANES), qrow))
        args.append(bias)
    in_specs += [pl.BlockSpec((seq_len, V7X_LANES), kv), pl.BlockSpec((seq_len, V7X_LANES), kv)]
    args += [k_pad, v_pad]
    return pl.pallas_call(
        functools.partial(_attn_kernel, tq=tq, mode=mode),
        grid=(B, NSA_KV_HEADS, nq),
        in_specs=in_specs,
        out_specs=pl.BlockSpec((tq, G * HEAD_DIM), qrow),
        out_shape=jax.ShapeDtypeStruct((T, Q_WIDTH), f32),
        scratch_shapes=[pltpu.VMEM((G * tq, tq), f32), pltpu.VMEM((G * tq, V7X_LANES), f32),
                        pltpu.VMEM((G * tq, seq_len if mode == "sel" else WINDOW + tq), f32)],
        compiler_params=_cparams("parallel", "parallel", "parallel"),
        name="nsa_attn_" + mode,
    )(*args)


def _nsa_out_kernel(y_ref, oc_ref, os_ref, ow_ref, gate_ref, e_ref, wout_ref, out_ref):
    g = gate_ref[...]
    ghi = g.astype(bf16)
    glo = (g - ghi.astype(f32)).astype(bf16)
    o = jnp.zeros(oc_ref.shape, f32)
    for j, br in enumerate((oc_ref, os_ref, ow_ref)):
        ge = (jnp.dot(ghi, e_ref[j], preferred_element_type=f32) + jnp.dot(glo, e_ref[j], preferred_element_type=f32))
        o = o + ge * br[...]
    out_ref[...] = y_ref[...] + jnp.dot(o.astype(bf16), wout_ref[...], preferred_element_type=f32)


def _nsa_out(y, oc, osel, ow, gates, expand, wout, *, tm):
    T = y.shape[0]
    row = lambda i: (i, 0)
    wide = pl.BlockSpec((tm, Q_WIDTH), row)
    return pl.pallas_call(
        _nsa_out_kernel,
        grid=(T // tm,),
        in_specs=[pl.BlockSpec((tm, D_MODEL), row), wide, wide, wide, pl.BlockSpec((tm, V7X_LANES), row),
                  pl.BlockSpec((3, V7X_LANES, Q_WIDTH), lambda i: (0, 0, 0)),
                  pl.BlockSpec((Q_WIDTH, D_MODEL), lambda i: (0, 0))],
        out_specs=pl.BlockSpec((tm, D_MODEL), row),
        out_shape=jax.ShapeDtypeStruct((T, D_MODEL), f32),
        compiler_params=_cparams("parallel"),
        name="nsa_out",
    )(y, oc, osel, ow, gates, expand, wout)


def _final_norm_kernel(y_ref, nw_ref, o_ref):
    o_ref[...] = _rms_rows(y_ref[...], nw_ref[...])


def _final_norm(y, nw, *, tm):
    T = y.shape[0]
    return pl.pallas_call(
        _final_norm_kernel,
        grid=(T // tm,),
        in_specs=[pl.BlockSpec((tm, D_MODEL), lambda i: (i, 0)), pl.BlockSpec((1, D_MODEL), lambda i: (0, 0))],
        out_specs=pl.BlockSpec((tm, D_MODEL), lambda i: (i, 0)),
        out_shape=jax.ShapeDtypeStruct((T, D_MODEL), f32),
        compiler_params=_cparams("parallel"),
        name="final_norm",
    )(y, nw)


DEC_ROWS = V7X_SUBLANES


def _page_gather_kernel(pt_ref, tbl_ref, o_ref, *, n_pages):
    b = pl.program_id(0)
    for p in range(n_pages):
        o_ref[p:p + 1, :] = tbl_ref[pl.ds(pt_ref[b, p], 1), :]


def _page_gather(page_table, tbl):
    DB, n_pages = page_table.shape
    n_phys, W = tbl.shape
    return pl.pallas_call(
        functools.partial(_page_gather_kernel, n_pages=n_pages),
        grid_spec=pltpu.PrefetchScalarGridSpec(
            num_scalar_prefetch=1,
            grid=(DB,),
            in_specs=[pl.BlockSpec((n_phys, W), lambda b, pt: (0, 0), pipeline_mode=pl.Buffered(1))],
            out_specs=pl.BlockSpec((None, n_pages, W), lambda b, pt: (b, 0, 0)),
        ),
        out_shape=jax.ShapeDtypeStruct((DB, n_pages, W), tbl.dtype),
        compiler_params=_cparams("arbitrary"),
        name="nsa_page_gather",
    )(page_table, tbl)


PAGE_BLOCKS = PAGE_SIZE // NSA_BLOCK
PAGE_ROWS = 2 * KV_WIDTH


CMP_PITCH = HEAD_DIM + V7X_SUBLANES


def _compress_pages_kernel(x_hbm, pe_ref, w_ref, o_ref, buf_ref, sem_ref, *, first, steps, M):
    i = pl.program_id(0)

    def copies(step, slot):
        return [pltpu.make_async_copy(x_hbm.at[first + step * M + k], buf_ref.at[slot, pl.ds(k * CMP_PITCH, HEAD_DIM), :],
                                      sem_ref.at[slot]) for k in range(M)]

    @pl.when(i == 0)
    def _():
        for c in copies(0, 0):
            c.start()

    @pl.when(i + 1 < steps)
    def _():
        for c in copies(i + 1, (i + 1) % 2):
            c.start()

    slot = i % 2
    for c in copies(i, slot):
        c.wait()
    buf = buf_ref.at[slot]
    acc = jnp.zeros((M, 2 * V7X_LANES), f32)
    for d in range(HEAD_DIM):
        a = buf[pl.ds(d, M, stride=CMP_PITCH), :].reshape(M // 8, 8, V7X_LANES) + pe_ref[d]
        acc = acc + jnp.dot(a.reshape(M, V7X_LANES).astype(bf16), w_ref[d], preferred_element_type=f32)
    is_k = lax.broadcasted_iota(jnp.int32, (M, V7X_LANES), 0) % (2 * NSA_KV_HEADS) < NSA_KV_HEADS
    o_ref[...] = jnp.where(is_k, acc[:, :V7X_LANES], acc[:, V7X_LANES:])


def _compress_pages(slabs, pe8, w2, *, layer, n_phys, pages):
    assert n_phys % pages == 0
    steps = n_phys // pages
    per_page = PAGE_ROWS // HEAD_DIM
    M = pages * per_page
    return pl.pallas_call(
        functools.partial(_compress_pages_kernel, first=layer * n_phys * per_page, steps=steps, M=M),
        grid=(steps,),
        in_specs=[pl.BlockSpec(memory_space=pl.ANY),
                  pl.BlockSpec(pe8.shape, lambda i: (0, 0, 0)),
                  pl.BlockSpec(w2.shape, lambda i: (0, 0, 0), pipeline_mode=pl.Buffered(1))],
        out_specs=pl.BlockSpec((M, V7X_LANES), lambda i: (i, 0)),
        out_shape=jax.ShapeDtypeStruct((steps * M, V7X_LANES), f32),
        scratch_shapes=[pltpu.VMEM((2, M * CMP_PITCH, PAGE_SIZE), f32), pltpu.SemaphoreType.DMA((2,))],
        compiler_params=_cparams("arbitrary"),
        name="nsa_compress_pages",
    )(slabs, pe8, w2)


def _cmp_decode_kernel(q_ref, ckv_ref, oc_ref, bias_ref, *, past, n_top):
    R = DEC_ROWS
    n_pages = ckv_ref.shape[0]
    nb = PAGE_BLOCKS * n_pages
    KVH, G = NSA_KV_HEADS, NSA_GROUP
    lane = lax.broadcasted_iota(jnp.int32, (R, nb), 1)
    blk = PAGE_BLOCKS * (lane % n_pages) + lane // n_pages
    pos = past + lax.broadcasted_iota(jnp.int32, (R, nb), 0)
    complete = (blk + 1) * NSA_BLOCK - 1 <= pos
    cand = blk < lax.shift_right_logical(pos, int(math.log2(NSA_BLOCK)))

    def blocks(c, g):
        lo = (c * KVH + g) * PAGE_BLOCKS * HEAD_DIM
        return jnp.concatenate([ckv_ref[:, lo + h * HEAD_DIM:lo + (h + 1) * HEAD_DIM] for h in range(PAGE_BLOCKS)],
                               axis=0).astype(bf16)

    ck = [blocks(0, g) for g in range(KVH)]
    cv = [blocks(1, g) for g in range(KVH)]
    heads = [(g, n) for g in range(KVH) for n in range(G)]
    qh = [q_ref[:, (g * G + n) * HEAD_DIM:(g * G + n + 1) * HEAD_DIM] for g, n in heads]
    sm = [jnp.where(complete, _dot_nt(qh[i], ck[g]) * ATTN_SCALE, NEG) for i, (g, n) in enumerate(heads)]
    e = [jnp.where(complete, jnp.exp(s - jnp.max(s, axis=-1, keepdims=True)), 0.0) for s in sm]
    den = [jnp.sum(x, axis=-1, keepdims=True) for x in e]
    p = [x / jnp.where(d > 0.0, d, 1.0) for x, d in zip(e, den)]
    for i, (g, n) in enumerate(heads):
        lo = (g * G + n) * HEAD_DIM
        oc_ref[:, lo:lo + HEAD_DIM] = _dot(p[i], cv[g])
    imp = jnp.concatenate([sum(p[g * G:(g + 1) * G]) for g in range(KVH)], axis=0)
    cand4 = jnp.concatenate([cand] * KVH, axis=0)
    blk4 = jnp.concatenate([blk] * KVH, axis=0)
    vals = jnp.where(cand4, imp, -1.0)
    cnts = [jnp.zeros(vals.shape, jnp.int32) for _ in range(4)]
    for s in range(1, nb):
        other = pltpu.roll(vals, s, 1)
        other_blk = pltpu.roll(blk4, s, 1)
        cnts[s % 4] = cnts[s % 4] + ((other > vals) | ((other == vals) & (other_blk < blk4))).astype(jnp.int32)
    cnt = (cnts[0] + cnts[1]) + (cnts[2] + cnts[3])
    bias = jnp.where(cand4 & (cnt < n_top), 0.0, NEG)
    for g in range(KVH):
        bias_ref[g] = bias[g * R:(g + 1) * R]


def _cmp_decode(q_raw, ckv, *, past):
    DB, n_pages = ckv.shape[:2]
    nb = PAGE_BLOCKS * n_pages
    assert nb == V7X_LANES
    return pl.pallas_call(
        functools.partial(_cmp_decode_kernel, past=past, n_top=min(N_SEL - 1, nb)),
        grid=(DB,),
        in_specs=[pl.BlockSpec((None, DEC_ROWS, Q_WIDTH), lambda b: (b, 0, 0)),
                  pl.BlockSpec((None,) + ckv.shape[1:], lambda b: (b, 0, 0))],
        out_specs=[pl.BlockSpec((None, DEC_ROWS, Q_WIDTH), lambda b: (b, 0, 0)),
                   pl.BlockSpec((None, NSA_KV_HEADS, DEC_ROWS, nb), lambda b: (b, 0, 0, 0))],
        out_shape=[jax.ShapeDtypeStruct((DB, DEC_ROWS, Q_WIDTH), f32),
                   jax.ShapeDtypeStruct((DB, NSA_KV_HEADS, DEC_ROWS, nb), f32)],
        compiler_params=_cparams("parallel"),
        name="nsa_cmp_decode",
    )(q_raw, ckv)


SEL_PAGES_PER_STEP = 16


def _sel_decode_kernel(pt_ref, q_ref, bias_ref, new_ref, pool_ref, o_ref, buf_ref, sem_ref, m_ref, l_ref, acc_ref,
                       *, layer, n_seq, n_pages, n_new):
    PG = SEL_PAGES_PER_STEP
    NG = n_pages // PG
    R = NSA_GROUP * DEC_ROWS
    keys = PG * PAGE_SIZE

    def copies(step, slot):
        b = step // NG
        j = step % NG
        return [pltpu.make_async_copy(pool_ref.at[layer, pt_ref[b, j * PG + p]], buf_ref.at[slot, p],
                                      sem_ref.at[slot]) for p in range(PG)]

    KVH = NSA_KV_HEADS

    def online(s, pv):
        m_old = [m_ref[g] for g in range(KVH)]
        m_new = [jnp.maximum(m_old[g], jnp.max(s[g], axis=-1, keepdims=True)) for g in range(KVH)]
        alpha = [jnp.exp2(m_old[g] - m_new[g]) for g in range(KVH)]
        p = [jnp.exp2(s[g] - m_new[g]) for g in range(KVH)]
        psum = [jnp.sum(p[g], axis=-1, keepdims=True) for g in range(KVH)]
        pvs = [pv[g](p[g]) for g in range(KVH)]
        for g in range(KVH):
            l_ref[g] = alpha[g] * l_ref[g] + psum[g]
            acc_ref[g] = alpha[g] * acc_ref[g] + pvs[g]
            m_ref[g] = m_new[g]

    for c in copies(0, 0):
        c.start()

    def body(step, carry):
        slot = step % 2
        b = step // NG
        j = step % NG

        @pl.when(step + 1 < n_seq * NG)
        def _():
            for c in copies(step + 1, 1 - slot):
                c.start()

        @pl.when(j == 0)
        def _():
            m_ref[...] = jnp.full(m_ref.shape, NEG, f32)
            l_ref[...] = jnp.zeros(l_ref.shape, f32)
            acc_ref[...] = jnp.zeros(acc_ref.shape, f32)
            tok = lax.broadcasted_iota(jnp.int32, (R, DEC_ROWS), 0) % DEC_ROWS
            key = lax.broadcasted_iota(jnp.int32, (R, DEC_ROWS), 1)
            causal = (key <= tok) & (key < n_new)
            qn = [q_ref[b, g][:, :HEAD_DIM] for g in range(KVH)]
            kn = [new_ref[b, :, g * HEAD_DIM:(g + 1) * HEAD_DIM] for g in range(KVH)]
            vn = [new_ref[b, :, KV_WIDTH + g * HEAD_DIM:KV_WIDTH + (g + 1) * HEAD_DIM] for g in range(KVH)]
            online([jnp.where(causal, _dot_nt(qn[g], kn[g]), NEG) for g in range(KVH)],
                   [functools.partial(lambda p, v: _dot(p, v), v=vn[g]) for g in range(KVH)])

        for c in copies(step, slot):
            c.wait()
        lane = lax.broadcasted_iota(jnp.int32, (V7X_LANES, keys), 0)
        key = lax.broadcasted_iota(jnp.int32, (V7X_LANES, keys), 1)
        kpage = j * PG + lax.shift_right_logical(key, int(math.log2(PAGE_SIZE)))
        khalf = lax.shift_right_logical(key, int(math.log2(NSA_BLOCK))) % PAGE_BLOCKS
        expand = (lane == khalf * n_pages + kpage).astype(bf16)
        qg = [q_ref[b, g][:, :HEAD_DIM] for g in range(KVH)]
        kt = [jnp.concatenate([buf_ref[slot, p, 0, g] for p in range(PG)], axis=1).astype(bf16) for g in range(KVH)]
        vt = [jnp.concatenate([buf_ref[slot, p, 1, g] for p in range(PG)], axis=1).astype(bf16) for g in range(KVH)]
        bias = [jnp.concatenate([bias_ref[b, g]] * NSA_GROUP, axis=0).astype(bf16) for g in range(KVH)]
        s = [_dot(qg[g], kt[g]) + jnp.dot(bias[g], expand, preferred_element_type=f32) for g in range(KVH)]
        online(s, [functools.partial(lambda p, v: _dot_nt(p, v), v=vt[g]) for g in range(KVH)])

        @pl.when(j == NG - 1)
        def _():
            o = [acc_ref[g] / l_ref[g] for g in range(KVH)]
            for g in range(KVH):
                for n in range(NSA_GROUP):
                    lo = (g * NSA_GROUP + n) * HEAD_DIM
                    o_ref[b, :, lo:lo + HEAD_DIM] = o[g][n * DEC_ROWS:(n + 1) * DEC_ROWS]

        return carry

    lax.fori_loop(0, n_seq * NG, body, 0)


def _sel_decode(page_table, q, bias, new_kv, pool, *, layer, n_new):
    DB, n_pages = page_table.shape
    assert n_pages % SEL_PAGES_PER_STEP == 0 and n_pages * PAGE_BLOCKS == V7X_LANES
    R = NSA_GROUP * DEC_ROWS
    whole = lambda shape: pl.BlockSpec(shape, lambda i, pt, _n=len(shape): (0,) * _n)
    return pl.pallas_call(
        functools.partial(_sel_decode_kernel, layer=layer, n_seq=DB, n_pages=n_pages, n_new=n_new),
        grid_spec=pltpu.PrefetchScalarGridSpec(
            num_scalar_prefetch=1,
            grid=(1,),
            in_specs=[whole(q.shape), whole(bias.shape), whole(new_kv.shape), pl.BlockSpec(memory_space=pl.ANY)],
            out_specs=whole((DB, DEC_ROWS, Q_WIDTH)),
            scratch_shapes=[pltpu.VMEM((2, SEL_PAGES_PER_STEP) + pool.shape[2:], f32),
                            pltpu.SemaphoreType.DMA((2,)),
                            pltpu.VMEM((NSA_KV_HEADS, R, 1), f32), pltpu.VMEM((NSA_KV_HEADS, R, 1), f32),
                            pltpu.VMEM((NSA_KV_HEADS, R, HEAD_DIM), f32)],
        ),
        out_shape=jax.ShapeDtypeStruct((DB, DEC_ROWS, Q_WIDTH), f32),
        compiler_params=_cparams("arbitrary"),
        name="nsa_sel_decode",
    )(page_table, q, bias, new_kv, pool)


def _win_decode_kernel(q_ref, buf_ref, new_ref, o_ref, *, n_new):
    wbuf = buf_ref.shape[-1]
    R = NSA_GROUP * DEC_ROWS
    KVH = NSA_KV_HEADS
    tok = lax.broadcasted_iota(jnp.int32, (R, wbuf), 0) % DEC_ROWS
    dist = wbuf + tok - lax.broadcasted_iota(jnp.int32, (R, wbuf), 1)
    in_buf = (dist >= 0) & (dist < WINDOW)
    tokn = lax.broadcasted_iota(jnp.int32, (R, DEC_ROWS), 0) % DEC_ROWS
    keyn = lax.broadcasted_iota(jnp.int32, (R, DEC_ROWS), 1)
    in_new = (keyn <= tokn) & (keyn < n_new)
    hs = range(KVH)
    q = [q_ref[g][:, :HEAD_DIM] for g in hs]
    kn = [new_ref[:, g * HEAD_DIM:(g + 1) * HEAD_DIM] for g in hs]
    vn = [new_ref[:, KV_WIDTH + g * HEAD_DIM:KV_WIDTH + (g + 1) * HEAD_DIM] for g in hs]
    s1 = [jnp.where(in_buf, _dot(q[g], buf_ref[0, g]), NEG) for g in hs]
    s2 = [jnp.where(in_new, _dot_nt(q[g], kn[g]), NEG) for g in hs]
    m = [jnp.maximum(jnp.max(s1[g], axis=-1, keepdims=True), jnp.max(s2[g], axis=-1, keepdims=True)) for g in hs]
    p1 = [jnp.exp2(s1[g] - m[g]) for g in hs]
    p2 = [jnp.exp2(s2[g] - m[g]) for g in hs]
    den = [jnp.sum(p1[g], axis=-1, keepdims=True) + jnp.sum(p2[g], axis=-1, keepdims=True) for g in hs]
    o = [(_dot_nt(p1[g], buf_ref[1, g]) + _dot(p2[g], vn[g])) / den[g] for g in hs]
    for g in hs:
        for n in range(NSA_GROUP):
            lo = (g * NSA_GROUP + n) * HEAD_DIM
            o_ref[:, lo:lo + HEAD_DIM] = o[g][n * DEC_ROWS:(n + 1) * DEC_ROWS]


def _win_decode(q, win_t, new_kv, *, layer, n_new):
    DB = q.shape[0]
    R = NSA_GROUP * DEC_ROWS
    return pl.pallas_call(
        functools.partial(_win_decode_kernel, n_new=n_new),
        grid=(DB,),
        in_specs=[pl.BlockSpec((None, NSA_KV_HEADS, R, V7X_LANES), lambda b: (b, 0, 0, 0)),
                  pl.BlockSpec((None, None) + win_t.shape[2:], lambda b: (layer, b, 0, 0, 0, 0)),
                  pl.BlockSpec((None, DEC_ROWS, 2 * KV_WIDTH), lambda b: (b, 0, 0))],
        out_specs=pl.BlockSpec((None, DEC_ROWS, Q_WIDTH), lambda b: (b, 0, 0)),
        out_shape=jax.ShapeDtypeStruct((DB, DEC_ROWS, Q_WIDTH), f32),
        compiler_params=_cparams("parallel"),
        name="nsa_win_decode",
    )(q, win_t, new_kv)


def _rope_tables(pos):
    half = HEAD_DIM // 2
    inv = ROPE_THETA ** (-jnp.arange(half, dtype=f32) / half)
    ang = pos.astype(f32)[:, None] * inv[None, :]
    cos, sin = jnp.cos(ang), jnp.sin(ang)
    reps = V7X_LANES // HEAD_DIM
    return jnp.tile(cos, (1, 2 * reps)), jnp.tile(jnp.concatenate([-sin, sin], axis=1), (1, reps))


def _block_onehot(pos):
    return (jnp.arange(V7X_LANES, dtype=jnp.int32)[None, :] == HEAD_DIM + pos[:, None] // NSA_BLOCK).astype(f32)


def _gate_expand():
    src = jnp.arange(V7X_LANES)[None, :, None]
    head = jnp.arange(Q_WIDTH)[None, None, :] // HEAD_DIM
    j = jnp.arange(3)[:, None, None]
    return (src == head * 3 + j).astype(bf16)


def _compress_page_weights(pe, w):
    assert PAGE_BLOCKS == 2
    wt = jnp.transpose(w, (2, 0, 1, 3)).astype(bf16)
    z = jnp.zeros_like(wt)
    w2 = jnp.concatenate([jnp.concatenate([wt, z], axis=-1), jnp.concatenate([z, wt], axis=-1)], axis=1)
    w2 = w2.reshape(HEAD_DIM, PAGE_SIZE, 2 * PAGE_BLOCKS * HEAD_DIM)
    pe8 = jnp.broadcast_to(pe.transpose(2, 1, 0)[:, :, None, None, :],
                           (HEAD_DIM, 2, NSA_KV_HEADS, PAGE_BLOCKS, NSA_BLOCK))
    return pe8.reshape(HEAD_DIM, 2 * NSA_KV_HEADS, PAGE_SIZE), w2


def _token_minor(cache):
    n = cache.ndim
    return jnp.transpose(cache, tuple(range(n - 4)) + (n - 3, n - 2, n - 1, n - 4))


def _time_major(x):
    return jnp.swapaxes(x, 0, 1).reshape((x.shape[0] * x.shape[1],) + x.shape[2:])


def _batch_major(x, db):
    return jnp.swapaxes(x.reshape((x.shape[0] // db, db) + x.shape[1:]), 0, 1)


def _gdn_weights(w_in, a_log, dt_bias):
    W3 = 3 * GDN_WIDTH
    pad = V7X_LANES - 2 * GDN_HEADS
    return (w_in[:, :W3].astype(bf16), w_in[:, W3:W3 + GDN_WIDTH].astype(bf16),
            jnp.pad(w_in[:, W3 + GDN_WIDTH:], ((0, 0), (0, pad))).astype(bf16),
            jnp.pad(a_log, (0, V7X_LANES - GDN_HEADS))[None], jnp.pad(dt_bias, (0, V7X_LANES - GDN_HEADS))[None])


def _gdn_layer(yp, ys, S0s, conv_s, nw, w_in, conv_w, a_log, dt_bias, norm_w, w_out, *, B, L, DB, DS):
    wqkv, wz, wab, alog, dtb = _gdn_weights(w_in, a_log, dt_bias)
    wout = w_out.astype(bf16)
    nw = nw[None]
    norm_w = norm_w[None]
    q, k, v, z, slab, conv_p = _gdn_in(yp, None, nw, wqkv, wz, wab, conv_w, alog, dtb, tm=512, shift=1, rows_per_seq=L)
    o, S_p = _gdn_chunk(q, k, v, slab, jnp.zeros((B, GDN_HEADS, GDN_DK, GDN_DK), f32), seq_len=L)
    yp = _gdn_out(yp, o, z, norm_w, wout, tm=512)
    Ts = DS * DB
    hist = _time_major(conv_s).reshape((GDN_CONV - 1) * DB, 3 * GDN_WIDTH)
    q, k, v, z, slab, conv_tm = _gdn_in(ys, hist, nw, wqkv, wz, wab, conv_w, alog, dtb, tm=Ts, shift=DB, rows_per_seq=Ts)

    def chunked(x):
        x = _batch_major(x, DB)
        return jnp.pad(x, ((0, 0), (0, GDN_CHUNK - DS), (0, 0))).reshape(DB * GDN_CHUNK, x.shape[-1])

    o, S_s = _gdn_chunk(chunked(q), chunked(k), chunked(v), chunked(slab), S0s, seq_len=GDN_CHUNK)
    o = _time_major(o.reshape(DB, GDN_CHUNK, GDN_WIDTH)[:, :DS])
    ys = _gdn_out(ys, o, z, norm_w, wout, tm=Ts)
    conv_s_new = _batch_major(conv_tm.reshape((GDN_CONV - 1) * DB, 3 * GDN_WIDTH), DB)
    return yp, ys, S_p, S_s, conv_p, conv_s_new


def _ffn_layer(yp, ys, conv_s, nw, w_up, conv_w, conv_b, w_down, *, B, L, DB, DS):
    wup = w_up.astype(bf16)
    wdn = w_down.astype(bf16)
    nw = nw[None]
    cb = conv_b[None]
    yp, st_p = _ffn(yp, None, nw, wup, conv_w, cb, wdn, tm=512, shift=1, rows_per_seq=L)
    Ts = DS * DB
    hist = _time_major(conv_s).reshape((FFN_CONV - 1) * DB, 2 * D_FF)
    ys, st_tm = _ffn(ys, hist, nw, wup, conv_w, cb, wdn, tm=Ts, shift=DB, rows_per_seq=Ts)
    st_s = _batch_major(st_tm.reshape((FFN_CONV - 1) * DB, 2 * D_FF), DB)
    return yp, ys, st_p, st_s


def _nsa_layer(yp, ys, j, kv_prev, cmp_t, sel_t, win_t, win_buf, page_table, nw, w_in, cmp_pe, cmp_w, w_out,
               *, B, L, DB, DS):
    past = page_table.shape[1] * PAGE_SIZE
    n_nsa = cmp_t.shape[0]
    w = jnp.pad(w_in, ((0, 0), (0, NSA_IN_WP - NSA_IN_W))).astype(bf16)
    wout = w_out.astype(bf16)
    nw = nw[None]
    expand = _gate_expand()
    pe8, w2 = _compress_page_weights(cmp_pe, cmp_w)
    tq = WINDOW // 2
    tq_attn = WINDOW
    pos_p = jnp.arange(L, dtype=jnp.int32)
    cos, sin = _rope_tables(pos_p)
    if kv_prev is None:
        kv_prev = tuple(jnp.zeros((n_nsa, B, 2 * KV_WIDTH, L), f32) for _ in range(3))
    (q_raw, q_rot, cmp_l, sel_l, win_l, ksel, vsel, kwin, vwin, gates) = _nsa_in(
        yp, nw, w, cos, sin, _block_onehot(pos_p), tm=256, tiles_per_seq=L // 256, kv_layers=(j, n_nsa, kv_prev))
    kv_layers = (cmp_l, sel_l, win_l)
    nb = L // NSA_BLOCK
    npg = B * L // PAGE_SIZE
    per_page = PAGE_ROWS // HEAD_DIM
    slabs = cmp_l[j].reshape(B, per_page, HEAD_DIM, L // PAGE_SIZE, PAGE_SIZE).transpose(0, 3, 1, 2, 4)
    ckv = _compress_pages(slabs.reshape(npg * per_page, HEAD_DIM, PAGE_SIZE), pe8, w2, layer=0, n_phys=npg,
                          pages=_pick_tile(npg, 16))
    ck = ckv.reshape(B, L // PAGE_SIZE, 2, NSA_KV_HEADS, PAGE_BLOCKS, HEAD_DIM)
    ck = ck.transpose(0, 1, 4, 2, 3, 5).reshape(B, nb, 2, NSA_KV_HEADS, HEAD_DIM)
    eye = jnp.eye(NSA_GROUP, dtype=f32)
    kbd = jnp.einsum("bkgd,nm->bgnkmd", ck[:, :, 0], eye).reshape(B, NSA_KV_HEADS, NSA_GROUP * nb, NSA_GROUP * HEAD_DIM)
    vbd = jnp.einsum("bkgd,nm->bgnkmd", ck[:, :, 1], eye).reshape(B, NSA_KV_HEADS, NSA_GROUP * nb, NSA_GROUP * HEAD_DIM)
    o_c, bias = _cmp_select(q_raw, kbd.astype(bf16), vbd.astype(bf16), seq_len=L, tq=tq)
    o_s = _attn(q_rot, bias, ksel, vsel, seq_len=L, tq=tq_attn, mode="sel")
    o_w = _attn(q_rot, None, kwin, vwin, seq_len=L, tq=tq_attn, mode="win")
    yp = _nsa_out(yp, o_c, o_s, o_w, gates, expand, wout, tm=512)
    Ts = DS * DB
    pos_s = past + jnp.arange(DS, dtype=jnp.int32)
    pos_rows = jnp.repeat(pos_s, DB)
    cos, sin = _rope_tables(pos_rows)
    (q_raw, q_rot, kv_cmp, kv_sel, kv_win, _, _, _, _, gates) = _nsa_in(
        ys, nw, w, cos, sin, jnp.zeros((Ts, V7X_LANES), f32), tm=Ts, tiles_per_seq=1)
    n_layers, n_phys = cmp_t.shape[:2]
    ckv_all = _compress_pages(cmp_t.reshape(n_layers * n_phys * PAGE_ROWS // HEAD_DIM, HEAD_DIM, PAGE_SIZE), pe8, w2,
                              layer=j, n_phys=n_phys, pages=_pick_tile(n_phys, 16))
    ckv = _page_gather(page_table, ckv_all.reshape(n_phys, 2 * KV_WIDTH * PAGE_BLOCKS))

    def rows8(x):
        return jnp.pad(_batch_major(x, DB), ((0, 0), (0, DEC_ROWS - DS), (0, 0)))

    o_c, bias = _cmp_decode(rows8(q_raw), ckv, past=past)
    q4 = rows8(q_rot).reshape(DB, DEC_ROWS, NSA_KV_HEADS, NSA_GROUP, V7X_LANES)
    q4 = q4.transpose(0, 2, 3, 1, 4).reshape(DB, NSA_KV_HEADS, NSA_GROUP * DEC_ROWS, V7X_LANES)
    o_s = _sel_decode(page_table, q4, bias, rows8(kv_sel), sel_t, layer=j, n_new=DS)
    o_w = _win_decode(q4, win_t, rows8(kv_win), layer=j, n_new=DS)
    ks = (DB, DS, 2, NSA_KV_HEADS, HEAD_DIM)
    cmp_s, sel_s, win_new = (_batch_major(t, DB).reshape(ks) for t in (kv_cmp, kv_sel, kv_win))
    win_s = jnp.concatenate([win_buf[:, DS:], win_new], axis=1)
    tm_rows = lambda o: _time_major(o[:, :DS])
    ys = _nsa_out(ys, tm_rows(o_c), tm_rows(o_s), tm_rows(o_w), gates, expand, wout, tm=Ts)
    return yp, ys, kv_layers, cmp_s, sel_s, win_s


def kernel(x_prompt, x_sample, state_gdn, state_gdn_conv, cache_cmp, cache_sel, state_win, state_ffn_conv, page_table,
           norm_mix, norm_ffn, norm_final, gdn_w_in, gdn_conv_w, gdn_a_log, gdn_dt_bias, gdn_norm_w, gdn_w_out,
           nsa_w_in, nsa_cmp_pe, nsa_cmp_w, nsa_w_out, ffn_w_up, ffn_conv_w, ffn_conv_b, ffn_w_down):
    B, L, _ = x_prompt.shape
    DB, DS, _ = x_sample.shape
    depth = norm_mix.shape[0]
    dims = dict(B=B, L=L, DB=DB, DS=DS)
    yp = x_prompt.reshape(B * L, D_MODEL)
    ys = _time_major(x_sample)
    outs = [[] for _ in range(12)]
    kv_p = None
    cmp_t, sel_t, win_t = _token_minor(cache_cmp), _token_minor(cache_sel), _token_minor(state_win)
    for i in range(depth):
        j = i // 2
        if i % 2 == 0:
            yp, ys, S_p, S_s, c_p, c_s = _gdn_layer(yp, ys, state_gdn[j], state_gdn_conv[j], norm_mix[i], gdn_w_in[j],
                                                    gdn_conv_w[j], gdn_a_log[j], gdn_dt_bias[j], gdn_norm_w[j],
                                                    gdn_w_out[j], **dims)
            for lst, val in zip(outs[0:4], (S_p, S_s, c_p, c_s)):
                lst.append(val)
        else:
            yp, ys, kv_p, *kv_s = _nsa_layer(yp, ys, j, kv_p, cmp_t, sel_t, win_t, state_win[j], page_table,
                                             norm_mix[i], nsa_w_in[j], nsa_cmp_pe[j], nsa_cmp_w[j], nsa_w_out[j], **dims)
            for lst, val in zip((outs[5], outs[7], outs[9]), kv_s):
                lst.append(val)
        yp, ys, f_p, f_s = _ffn_layer(yp, ys, state_ffn_conv[i], norm_ffn[i], ffn_w_up[i], ffn_conv_w[i],
                                      ffn_conv_b[i], ffn_w_down[i], **dims)
        outs[10].append(f_p)
        outs[11].append(f_s)
    y_prompt = _final_norm(yp, norm_final[None], tm=512).reshape(B, L, D_MODEL)
    y_sample = _batch_major(_final_norm(ys, norm_final[None], tm=DS * DB), DB)
    rows_of = lambda t: jnp.transpose(t.reshape(t.shape[:2] + (2, NSA_KV_HEADS, HEAD_DIM, t.shape[-1])), (0, 1, 5, 2, 3, 4))
    cmp_p, sel_p, win_p = rows_of(kv_p[0]), rows_of(kv_p[1]), rows_of(kv_p[2][..., L - min(WINDOW, L):])
    stacked = [jnp.stack(o) if o else None for o in outs]
    stacked[4], stacked[6], stacked[8] = cmp_p, sel_p, win_p
    return (y_prompt, y_sample) + tuple(stacked)
```

```python
import functools
import math

import jax
import jax.numpy as jnp
from jax import lax
from jax.experimental import pallas as pl
from jax.experimental.pallas import tpu as pltpu

f32 = jnp.float32
bf16 = jnp.bfloat16

D_MODEL = 1024
GDN_HEADS = 8
GDN_DK = 128
GDN_WIDTH = GDN_HEADS * GDN_DK
GDN_CONV = 4
GDN_CHUNK = 64
NSA_HEADS = 16
NSA_KV_HEADS = 4
NSA_GROUP = 4
HEAD_DIM = 64
Q_WIDTH = NSA_HEADS * HEAD_DIM
KV_WIDTH = NSA_KV_HEADS * HEAD_DIM
NSA_BLOCK = 64
N_SEL = 16
WINDOW = 512
PAGE_SIZE = 128
ROPE_THETA = 10000.0
ATTN_SCALE = HEAD_DIM ** -0.5
LOG2E = math.log2(math.e)
D_FF = 2816
FFN_CONV = 3
RMS_EPS = 1e-6
NEG = -1e30

V7X_LANES = 128
V7X_SUBLANES = 8
V7X_VMEM_BYTES = 64 * 1024 * 1024
VMEM_LIMIT = 56 * 1024 * 1024


def _cparams(*sem):
    return pltpu.CompilerParams(dimension_semantics=sem, vmem_limit_bytes=VMEM_LIMIT)


def _round_up(x, m):
    return (x + m - 1) // m * m


def _pick_tile(n, cap):
    best = None
    for t in range(V7X_SUBLANES, min(n, cap) + 1, V7X_SUBLANES):
        if n % t == 0:
            best = t
    assert best is not None, (n, cap)
    return best


def _dot(a, b):
    return jnp.dot(a.astype(bf16), b.astype(bf16), preferred_element_type=f32)


def _dot_nt(a, b):
    return lax.dot_general(a.astype(bf16), b.astype(bf16), (((1,), (1,)), ((), ())), preferred_element_type=f32)


def _dot_tn(a, b):
    return lax.dot_general(a.astype(bf16), b.astype(bf16), (((0,), (0,)), ((), ())), preferred_element_type=f32)


def _split3(x):
    hi = x.astype(bf16)
    r = x - hi.astype(f32)
    mid = r.astype(bf16)
    lo = (r - mid.astype(f32)).astype(bf16)
    return hi, mid, lo


def _dot_exact_lhs(a01, x):
    a = a01.astype(bf16)
    hi, mid, lo = _split3(x)
    return (jnp.dot(a, hi, preferred_element_type=f32) + jnp.dot(a, mid, preferred_element_type=f32)
            + jnp.dot(a, lo, preferred_element_type=f32))


def _silu(x):
    return x / (1.0 + jnp.exp(-x))


def _sigmoid(x):
    return 1.0 / (1.0 + jnp.exp(-x))


def _rms_rows(x, w):
    return x * lax.rsqrt(jnp.mean(x * x, axis=-1, keepdims=True) + RMS_EPS) * w


def _ffn_kernel(*refs, tm, shift, hist_rows, has_hist, has_final, tiles_per_seq, n_chunk):
    refs = list(refs)
    y_ref = refs.pop(0)
    hist_ref = refs.pop(0) if has_hist else None
    nw_ref, wup_ref, cw_ref, cb_ref, wdn_ref = refs[:5]
    fw_ref = refs[5] if has_final else None
    out_ref, st_ref, ext_ref, g_ref = refs[5 + has_final:]
    H = hist_rows
    s = shift
    LN = V7X_LANES
    nblk = 2 * D_FF // LN
    t = pl.program_id(0) % tiles_per_seq

    @pl.when(t == 0)
    def _():
        if has_hist:
            for c in range(nblk):
                ext_ref[c, H - 2 * s:H, :] = hist_ref[:, c * LN:(c + 1) * LN]
        else:
            ext_ref[:, 0:H, :] = jnp.zeros((nblk, H, LN), f32)

    @pl.when(t != 0)
    def _():
        ext_ref[:, H - 2 * s:H, :] = ext_ref[:, H + tm - 2 * s:H + tm, :]

    y = y_ref[...]
    h = _rms_rows(y, nw_ref[...]).astype(bf16)
    per = n_chunk // LN
    for c in range(2 * D_FF // n_chunk):
        u = jnp.dot(h, wup_ref[:, c * n_chunk:(c + 1) * n_chunk], preferred_element_type=f32)
        for k in range(per):
            ext_ref[c * per + k, H:H + tm, :] = u[:, k * LN:(k + 1) * LN]
    for c in range(nblk):
        st_ref[:, c * LN:(c + 1) * LN] = ext_ref[c, H + tm - 2 * s:H + tm, :]

    def conv(c):
        w = cw_ref[:, c * LN:(c + 1) * LN]
        return (ext_ref[c, H - 2 * s:H - 2 * s + tm, :] * w[0:1, :] + ext_ref[c, H - s:H - s + tm, :] * w[1:2, :]
                + ext_ref[c, H:H + tm, :] * w[2:3, :] + cb_ref[:, c * LN:(c + 1) * LN])

    for j in range(D_FF // LN):
        g_ref[:, j * LN:(j + 1) * LN] = (_silu(conv(j)) * conv(D_FF // LN + j)).astype(bf16)
    res = y + jnp.dot(g_ref[...], wdn_ref[...], preferred_element_type=f32)
    out_ref[...] = _rms_rows(res, fw_ref[...]) if has_final else res


def _ffn(y, hist, nw, wup, cw, cb, wdn, *, tm, shift, rows_per_seq, final_nw=None):
    T = y.shape[0]
    n_seq = T // rows_per_seq
    tiles_per_seq = rows_per_seq // tm
    H = _round_up(2 * shift, V7X_SUBLANES)
    kern = functools.partial(_ffn_kernel, tm=tm, shift=shift, hist_rows=H, has_hist=hist is not None,
                             has_final=final_nw is not None, tiles_per_seq=tiles_per_seq, n_chunk=512)
    const = lambda i: (0, 0)
    in_specs = [pl.BlockSpec((tm, D_MODEL), lambda i: (i, 0))]
    args = [y]
    if hist is not None:
        in_specs.append(pl.BlockSpec((2 * shift, 2 * D_FF), lambda i: (i // tiles_per_seq, 0)))
        args.append(hist)
    in_specs += [pl.BlockSpec((1, D_MODEL), const),
                 pl.BlockSpec((D_MODEL, 2 * D_FF), const, pipeline_mode=pl.Buffered(1)),
                 pl.BlockSpec((FFN_CONV, 2 * D_FF), const),
                 pl.BlockSpec((1, 2 * D_FF), const),
                 pl.BlockSpec((D_FF, D_MODEL), const, pipeline_mode=pl.Buffered(1))]
    args += [nw, wup, cw, cb, wdn]
    if final_nw is not None:
        in_specs.append(pl.BlockSpec((1, D_MODEL), const))
        args.append(final_nw)
    out, st = pl.pallas_call(
        kern,
        grid=(T // tm,),
        in_specs=in_specs,
        out_specs=[pl.BlockSpec((tm, D_MODEL), lambda i: (i, 0)),
                   pl.BlockSpec((None, 2 * shift, 2 * D_FF), lambda i: (i // tiles_per_seq, 0, 0))],
        out_shape=[jax.ShapeDtypeStruct((T, D_MODEL), f32),
                   jax.ShapeDtypeStruct((n_seq, 2 * shift, 2 * D_FF), f32)],
        scratch_shapes=[pltpu.VMEM((2 * D_FF // V7X_LANES, H + tm, V7X_LANES), f32), pltpu.VMEM((tm, D_FF), bf16)],
        compiler_params=_cparams("arbitrary"),
        name="conv_ffn",
    )(*args)
    return out, st


def _gdn_in_kernel(*refs, tm, shift, hist_rows, has_hist, tiles_per_seq):
    if has_hist:
        (y_ref, hist_ref, nw_ref, wqkv_ref, wz_ref, wab_ref, cw_ref, alog_ref, dtb_ref,
         q_ref, k_ref, v_ref, z_ref, slab_ref, st_ref, ext_ref) = refs
    else:
        (y_ref, nw_ref, wqkv_ref, wz_ref, wab_ref, cw_ref, alog_ref, dtb_ref,
         q_ref, k_ref, v_ref, z_ref, slab_ref, st_ref, ext_ref) = refs
        hist_ref = None
    H = hist_rows
    s = shift
    nh = GDN_CONV - 1
    W3 = 3 * GDN_WIDTH
    t = pl.program_id(0) % tiles_per_seq
    hw = GDN_DK
    nblk = W3 // hw

    @pl.when(t == 0)
    def _():
        if has_hist:
            for c in range(nblk):
                ext_ref[c, H - nh * s:H, :] = hist_ref[:, c * hw:(c + 1) * hw]
        else:
            ext_ref[:, 0:H, :] = jnp.zeros((nblk, H, hw), f32)

    @pl.when(t != 0)
    def _():
        ext_ref[:, H - nh * s:H, :] = ext_ref[:, H + tm - nh * s:H + tm, :]

    h = _rms_rows(y_ref[...], nw_ref[...]).astype(bf16)
    nc = 512
    for c in range(W3 // nc):
        u = jnp.dot(h, wqkv_ref[:, c * nc:(c + 1) * nc], preferred_element_type=f32)
        for k in range(nc // hw):
            ext_ref[c * (nc // hw) + k, H:H + tm, :] = u[:, k * hw:(k + 1) * hw]
    for c in range(GDN_WIDTH // nc):
        z_ref[:, c * nc:(c + 1) * nc] = jnp.dot(h, wz_ref[:, c * nc:(c + 1) * nc], preferred_element_type=f32)
    ab = jnp.dot(h, wab_ref[...], preferred_element_type=f32)
    x = ab + dtb_ref[...]
    softplus = jnp.maximum(x, 0.0) + jnp.log(1.0 + jnp.exp(-jnp.abs(x)))
    g = -jnp.exp(alog_ref[...]) * softplus
    lane = lax.broadcasted_iota(jnp.int32, ab.shape, 1)
    slab_ref[...] = jnp.where(lane < GDN_HEADS, g, _sigmoid(ab))
    for c in range(nblk):
        st_ref[:, c * hw:(c + 1) * hw] = ext_ref[c, H + tm - nh * s:H + tm, :]

    for j in range(nblk):
        col = j * hw
        w = cw_ref[:, col:col + hw]
        c = ext_ref[j, H:H + tm, :] * w[nh:nh + 1, :]
        for i in range(nh):
            c = c + ext_ref[j, H - (nh - i) * s:H - (nh - i) * s + tm, :] * w[i:i + 1, :]
        c = _silu(c)
        if j < 2 * GDN_HEADS:
            c = c * lax.rsqrt(jnp.sum(c * c, axis=-1, keepdims=True) + RMS_EPS)
        if j < GDN_HEADS:
            q_ref[:, col:col + hw] = c * (GDN_DK ** -0.5)
        elif j < 2 * GDN_HEADS:
            k_ref[:, col - GDN_WIDTH:col - GDN_WIDTH + hw] = c
        else:
            v_ref[:, col - 2 * GDN_WIDTH:col - 2 * GDN_WIDTH + hw] = c


def _gdn_in(y, hist, nw, wqkv, wz, wab, cw, alog, dtb, *, tm, shift, rows_per_seq):
    T = y.shape[0]
    n_seq = T // rows_per_seq
    tiles_per_seq = rows_per_seq // tm
    nh = GDN_CONV - 1
    H = _round_up(nh * shift, V7X_SUBLANES)
    W3 = 3 * GDN_WIDTH
    kern = functools.partial(_gdn_in_kernel, tm=tm, shift=shift, hist_rows=H, has_hist=hist is not None,
                             tiles_per_seq=tiles_per_seq)
    const = lambda i: (0, 0)
    row = lambda i: (i, 0)
    in_specs = [pl.BlockSpec((tm, D_MODEL), row)]
    args = [y]
    if hist is not None:
        in_specs.append(pl.BlockSpec((nh * shift, W3), lambda i: (i // tiles_per_seq, 0)))
        args.append(hist)
    in_specs += [pl.BlockSpec((1, D_MODEL), const),
                 pl.BlockSpec((D_MODEL, W3), const, pipeline_mode=pl.Buffered(1)),
                 pl.BlockSpec((D_MODEL, GDN_WIDTH), const, pipeline_mode=pl.Buffered(1)),
                 pl.BlockSpec((D_MODEL, V7X_LANES), const),
                 pl.BlockSpec((GDN_CONV, W3), const),
                 pl.BlockSpec((1, V7X_LANES), const),
                 pl.BlockSpec((1, V7X_LANES), const)]
    args += [nw, wqkv, wz, wab, cw, alog, dtb]
    wide = pl.BlockSpec((tm, GDN_WIDTH), row)
    return pl.pallas_call(
        kern,
        grid=(T // tm,),
        in_specs=in_specs,
        out_specs=[wide, wide, wide, wide, pl.BlockSpec((tm, V7X_LANES), row),
                   pl.BlockSpec((None, nh * shift, W3), lambda i: (i // tiles_per_seq, 0, 0))],
        out_shape=[jax.ShapeDtypeStruct((T, GDN_WIDTH), f32)] * 4
        + [jax.ShapeDtypeStruct((T, V7X_LANES), f32), jax.ShapeDtypeStruct((n_seq, nh * shift, W3), f32)],
        scratch_shapes=[pltpu.VMEM((W3 // GDN_DK, H + tm, GDN_DK), f32)],
        compiler_params=_cparams("arbitrary"),
        name="gdn_in",
    )(*args)


def _gdn_chunk_kernel(q_ref, k_ref, v_ref, slab_ref, s0_ref, o_ref, s_ref, *, nseq):
    C = GDN_CHUNK
    H = GDN_HEADS
    n = pl.program_id(1)

    @pl.when(n == 0)
    def _():
        s_ref[...] = s0_ref[...]

    ri = lax.broadcasted_iota(jnp.int32, (C, C), 0)
    ci = lax.broadcasted_iota(jnp.int32, (C, C), 1)
    tril = ri >= ci
    strict = ri > ci
    eye = (ri == ci).astype(f32)
    slab = [slab_ref[s] for s in range(nseq)]
    G = [_dot_exact_lhs(tril.astype(f32), slab[s]) for s in range(nseq)]
    GT = [G[s].T for s in range(nseq)]
    ch = [(s, h) for s in range(nseq) for h in range(H)]
    idx = range(len(ch))
    col = lambda h: slice(h * GDN_DK, (h + 1) * GDN_DK)
    qh = [q_ref[s, :, col(h)] for s, h in ch]
    kh = [k_ref[s, :, col(h)] for s, h in ch]
    vh = [v_ref[s, :, col(h)] for s, h in ch]
    Gc = [G[s][:, h:h + 1] for s, h in ch]
    bc = [slab[s][:, H + h:H + h + 1] for s, h in ch]
    gl = [G[s][C - 1:C, h:h + 1] for s, h in ch]
    decay = [jnp.exp(jnp.where(tril, Gc[i] - GT[s][h:h + 1, :], NEG)) for i, (s, h) in enumerate(ch)]
    kb = [kh[i].astype(bf16) for i in idx]
    A = [jnp.where(strict, bc[i] * _dot_nt(kb[i], kb[i]) * decay[i], 0.0) for i in idx]
    P = [eye - A[i] for i in idx]
    Ap = A
    for _ in range(int(math.log2(C)) - 1):
        Ap = [_dot(Ap[i], Ap[i]) for i in idx]
        P = [P[i] + _dot(P[i], Ap[i]) for i in idx]
    Tb = [P[i].astype(bf16) for i in idx]
    uv = [_dot(Tb[i], vh[i] * bc[i]) for i in idx]
    wk = [_dot(Tb[i], kh[i] * (bc[i] * jnp.exp(Gc[i]))) for i in idx]
    qk = [_dot_nt(qh[i], kb[i]) * decay[i] for i in idx]
    kdec = [kh[i] * jnp.exp(gl[i] - Gc[i]) for i in idx]
    S = [s_ref[s, h] for s, h in ch]
    Sb = [S[i].astype(bf16) for i in idx]
    u = [uv[i] - _dot(wk[i], Sb[i]) for i in idx]
    o = [_dot(qh[i] * jnp.exp(Gc[i]), Sb[i]) + _dot(qk[i], u[i]) for i in idx]
    Sn = [S[i] * jnp.exp(gl[i]) + _dot_tn(kdec[i], u[i]) for i in idx]
    for i, (s, h) in enumerate(ch):
        o_ref[s, :, col(h)] = o[i]
        s_ref[s, h] = Sn[i]


GDN_SEQS_PER_STEP = 8


def _gdn_chunk(q, k, v, slab, s0, *, seq_len):
    T = q.shape[0]
    B = T // seq_len
    N = seq_len // GDN_CHUNK
    nseq = math.gcd(B, GDN_SEQS_PER_STEP)
    blk = lambda w: pl.BlockSpec((nseq, GDN_CHUNK, w), lambda b, n: (b, n, 0))
    st = pl.BlockSpec((nseq, GDN_HEADS, GDN_DK, GDN_DK), lambda b, n: (b, 0, 0, 0))
    seqs = lambda x: x.reshape(B, seq_len, x.shape[-1])
    o, s_out = pl.pallas_call(
        functools.partial(_gdn_chunk_kernel, nseq=nseq),
        grid=(B // nseq, N),
        in_specs=[blk(GDN_WIDTH), blk(GDN_WIDTH), blk(GDN_WIDTH), blk(V7X_LANES), st],
        out_specs=[blk(GDN_WIDTH), st],
        out_shape=[jax.ShapeDtypeStruct((B, seq_len, GDN_WIDTH), f32),
                   jax.ShapeDtypeStruct((B, GDN_HEADS, GDN_DK, GDN_DK), f32)],
        compiler_params=_cparams("parallel", "arbitrary"),
        name="gdn_chunk",
    )(seqs(q), seqs(k), seqs(v), seqs(slab), s0)
    return o.reshape(T, GDN_WIDTH), s_out


def _gdn_out_kernel(y_ref, o_ref, z_ref, nw_ref, wout_ref, out_ref, g_ref):
    for h in range(GDN_HEADS):
        lo = h * GDN_DK
        o = _rms_rows(o_ref[:, lo:lo + GDN_DK], nw_ref[...])
        g_ref[:, lo:lo + GDN_DK] = (o * _silu(z_ref[:, lo:lo + GDN_DK])).astype(bf16)
    out_ref[...] = y_ref[...] + jnp.dot(g_ref[...], wout_ref[...], preferred_element_type=f32)


def _gdn_out(y, o, z, nw, wout, *, tm):
    T = y.shape[0]
    row = lambda i: (i, 0)
    const = lambda i: (0, 0)
    wide = pl.BlockSpec((tm, GDN_WIDTH), row)
    return pl.pallas_call(
        _gdn_out_kernel,
        grid=(T // tm,),
        in_specs=[pl.BlockSpec((tm, D_MODEL), row), wide, wide, pl.BlockSpec((1, GDN_DK), const),
                  pl.BlockSpec((GDN_WIDTH, D_MODEL), const)],
        out_specs=pl.BlockSpec((tm, D_MODEL), row),
        out_shape=jax.ShapeDtypeStruct((T, D_MODEL), f32),
        scratch_shapes=[pltpu.VMEM((tm, GDN_WIDTH), bf16)],
        compiler_params=_cparams("parallel"),
        name="gdn_out",
    )(y, o, z, nw, wout)


NSA_IN_W = Q_WIDTH + 6 * KV_WIDTH + 3 * NSA_HEADS
NSA_IN_WP = _round_up(NSA_IN_W, V7X_LANES)
KV0 = Q_WIDTH
GATE0 = Q_WIDTH + 6 * KV_WIDTH


def _lane_iota(shape):
    return lax.broadcasted_iota(jnp.int32, shape, 1)


def _rope_pair(x, cos, sin):
    half = HEAD_DIM // 2
    lane = _lane_iota(x.shape)
    rot = jnp.where(lane % HEAD_DIM < half, pltpu.roll(x, V7X_LANES - half, 1), pltpu.roll(x, half, 1))
    return x * cos + rot * sin


def _split_pair(x, fill):
    lane = _lane_iota(x.shape)
    return jnp.where(lane < HEAD_DIM, x, fill), jnp.where(lane < HEAD_DIM, pltpu.roll(x, HEAD_DIM, 1), fill)


def _nsa_in_kernel(*refs, token_minor, n_alias):
    y_ref, nw_ref, w_ref, cos_ref, sin_ref, oh_ref = refs[:6]
    (qraw_ref, qrot_ref, cmp_ref, sel_ref, win_ref, ksel_ref, vsel_ref, kwin_ref, vwin_ref, gate_ref,
     proj_ref) = refs[6 + n_alias:]

    def put(o_ref, lo, x):
        if token_minor:
            o_ref[lo:lo + V7X_LANES, :] = x.T
        else:
            o_ref[:, lo:lo + V7X_LANES] = x

    h = _rms_rows(y_ref[...], nw_ref[...]).astype(bf16)
    nc = NSA_IN_WP // 3
    for c in range(3):
        proj_ref[:, c * nc:(c + 1) * nc] = jnp.dot(h, w_ref[:, c * nc:(c + 1) * nc], preferred_element_type=f32)
    cos = cos_ref[...]
    sin = sin_ref[...]
    oh = oh_ref[...]
    L2 = 2 * V7X_LANES
    for p in range(Q_WIDTH // V7X_LANES):
        x = proj_ref[:, p * V7X_LANES:(p + 1) * V7X_LANES]
        qraw_ref[:, p * V7X_LANES:(p + 1) * V7X_LANES] = x.astype(bf16)
        a, b = _split_pair(_rope_pair(x, cos, sin) * (ATTN_SCALE * LOG2E), 0.0)
        qrot_ref[:, p * L2:p * L2 + V7X_LANES] = a.astype(bf16)
        qrot_ref[:, p * L2 + V7X_LANES:(p + 1) * L2] = b.astype(bf16)
    for p in range(2 * KV_WIDTH // V7X_LANES):
        put(cmp_ref, p * V7X_LANES, proj_ref[:, KV0 + p * V7X_LANES:KV0 + (p + 1) * V7X_LANES])
    for br, (o_ref, k_ref, v_ref, fill) in enumerate(((sel_ref, ksel_ref, vsel_ref, oh), (win_ref, kwin_ref, vwin_ref, 0.0))):
        base = KV0 + (br + 1) * 2 * KV_WIDTH
        for p in range(KV_WIDTH // V7X_LANES):
            lo = p * V7X_LANES
            kx = _rope_pair(proj_ref[:, base + lo:base + lo + V7X_LANES], cos, sin)
            put(o_ref, lo, kx)
            a, b = _split_pair(kx, fill)
            k_ref[:, p * L2:p * L2 + V7X_LANES] = a.astype(bf16)
            k_ref[:, p * L2 + V7X_LANES:(p + 1) * L2] = b.astype(bf16)
            vx = proj_ref[:, base + KV_WIDTH + lo:base + KV_WIDTH + lo + V7X_LANES]
            put(o_ref, KV_WIDTH + lo, vx)
            a, b = _split_pair(vx, 1.0)
            v_ref[:, p * L2:p * L2 + V7X_LANES] = a.astype(bf16)
            v_ref[:, p * L2 + V7X_LANES:(p + 1) * L2] = b.astype(bf16)
    gate_ref[...] = _sigmoid(proj_ref[:, GATE0:GATE0 + V7X_LANES])


def _nsa_in(y, nw, w, cos, sin, onehot, *, tm, tiles_per_seq, kv_layers=None):
    T = y.shape[0]
    row = lambda i: (i, 0)
    const = lambda i: (0, 0)
    tab = lambda i: (i % tiles_per_seq, 0)
    P4 = NSA_KV_HEADS * V7X_LANES
    W = 2 * KV_WIDTH
    blk = lambda w_: pl.BlockSpec((tm, w_), row)
    shp = lambda w_, dt: jax.ShapeDtypeStruct((T, w_), dt)
    args = [y, nw, w, cos, sin, onehot]
    in_specs = [blk(D_MODEL), pl.BlockSpec((1, D_MODEL), const),
                pl.BlockSpec((D_MODEL, NSA_IN_WP), const, pipeline_mode=pl.Buffered(1)),
                pl.BlockSpec((tm, V7X_LANES), tab), pl.BlockSpec((tm, V7X_LANES), tab),
                pl.BlockSpec((tm, V7X_LANES), tab)]
    aliases = {}
    if kv_layers is None:
        kv_specs = [blk(W)] * 3
        kv_shapes = [shp(W, f32)] * 3
    else:
        layer, n_layers, prev = kv_layers
        n_seq = T // (tm * tiles_per_seq)
        kv_specs = [pl.BlockSpec((None, None, W, tm), lambda i: (layer, i // tiles_per_seq, 0, i % tiles_per_seq))] * 3
        kv_shapes = [jax.ShapeDtypeStruct((n_layers, n_seq, W, tm * tiles_per_seq), f32)] * 3
        if prev is not None:
            aliases = {len(args) + k: 2 + k for k in range(3)}
            args += list(prev)
            in_specs += [pl.BlockSpec(memory_space=pl.ANY)] * 3
    return pl.pallas_call(
        functools.partial(_nsa_in_kernel, token_minor=kv_layers is not None, n_alias=len(aliases)),
        grid=(T // tm,),
        in_specs=in_specs,
        out_specs=[blk(Q_WIDTH), blk(2 * Q_WIDTH)] + kv_specs + [blk(P4), blk(P4), blk(P4), blk(P4), blk(V7X_LANES)],
        out_shape=[shp(Q_WIDTH, bf16), shp(2 * Q_WIDTH, bf16)] + kv_shapes
        + [shp(P4, bf16), shp(P4, bf16), shp(P4, bf16), shp(P4, bf16), shp(V7X_LANES, f32)],
        scratch_shapes=[pltpu.VMEM((tm, NSA_IN_WP), f32)],
        input_output_aliases=aliases,
        compiler_params=_cparams("parallel"),
        name="nsa_in",
    )(*args)


def _cmp_select_kernel(q_ref, kbd_ref, vbd_ref, oc_ref, bias_ref, *, tq, nb, n_top):
    i = pl.program_id(1)
    G = NSA_GROUP
    gw = G * HEAD_DIM
    blk = lax.broadcasted_iota(jnp.int32, (nb, tq), 0)
    pos = i * tq + lax.broadcasted_iota(jnp.int32, (nb, tq), 1)
    complete = (blk + 1) * NSA_BLOCK - 1 <= pos
    cur = lax.shift_right_logical(pos, int(math.log2(NSA_BLOCK)))
    cand = blk < cur
    for g in range(NSA_KV_HEADS):
        sT = _dot_nt(kbd_ref[g], q_ref[:, g * gw:(g + 1) * gw]) * ATTN_SCALE
        ps = []
        imp = jnp.zeros((nb, tq), f32)
        for n in range(G):
            sm = jnp.where(complete, sT[n * nb:(n + 1) * nb], NEG)
            mx = jnp.max(sm, axis=0, keepdims=True)
            e = jnp.where(complete, jnp.exp(sm - mx), 0.0)
            den = jnp.sum(e, axis=0, keepdims=True)
            p = e / jnp.where(den > 0.0, den, 1.0)
            ps.append(p)
            imp = imp + p
        oc_ref[:, g * gw:(g + 1) * gw] = _dot_tn(jnp.concatenate(ps, axis=0), vbd_ref[g])
        vals = jnp.where(cand, imp, -1.0)
        cnt = jnp.zeros((nb, tq), jnp.int32)
        for j in range(nb):
            vj = vals[j:j + 1, :]
            cnt = cnt + ((vj > vals) | ((vj == vals) & (blk > j))).astype(jnp.int32)
        sel = (cand & (cnt < n_top)) | (blk == cur)
        biasT = jnp.where(sel, 0.0, NEG)
        full = jnp.concatenate([jnp.zeros((HEAD_DIM, tq), f32), biasT,
                                jnp.zeros((V7X_LANES - HEAD_DIM - nb, tq), f32)], axis=0)
        bias_ref[:, g * V7X_LANES:(g + 1) * V7X_LANES] = full.T.astype(bf16)


def _cmp_select(q_raw, kbd, vbd, *, seq_len, tq):
    T = q_raw.shape[0]
    nq = seq_len // tq
    nb = seq_len // NSA_BLOCK
    row = lambda b, i: (b * nq + i, 0)
    kern = functools.partial(_cmp_select_kernel, tq=tq, nb=nb, n_top=min(N_SEL - 1, nb))
    return pl.pallas_call(
        kern,
        grid=(T // seq_len, nq),
        in_specs=[pl.BlockSpec((tq, Q_WIDTH), row),
                  pl.BlockSpec((None,) + kbd.shape[1:], lambda b, i: (b, 0, 0, 0)),
                  pl.BlockSpec((None,) + vbd.shape[1:], lambda b, i: (b, 0, 0, 0))],
        out_specs=[pl.BlockSpec((tq, Q_WIDTH), row), pl.BlockSpec((tq, NSA_KV_HEADS * V7X_LANES), row)],
        out_shape=[jax.ShapeDtypeStruct((T, Q_WIDTH), f32),
                   jax.ShapeDtypeStruct((T, NSA_KV_HEADS * V7X_LANES), bf16)],
        compiler_params=_cparams("parallel", "parallel"),
        name="nsa_cmp_select",
    )(q_raw, kbd, vbd)


def _attn_kernel(*refs, tq, mode):
    if mode == "sel":
        q_ref, bias_ref, k_ref, v_ref, o_ref, mx_ref, acc_ref, s_ref = refs
    else:
        q_ref, k_ref, v_ref, o_ref, mx_ref, acc_ref, s_ref = refs
    i = pl.program_id(2)
    G = NSA_GROUP
    hq = tq // 2
    R2 = G * hq
    qn = [q_ref[:, n * V7X_LANES:(n + 1) * V7X_LANES] for n in range(G)]
    if mode == "sel":
        qn = [q + bias_ref[...] for q in qn]
    qa = jnp.concatenate([q[h * hq:(h + 1) * hq] for h in range(2) for q in qn], axis=0)
    rows = [slice(h * R2, (h + 1) * R2) for h in range(2)]
    q_in = lax.broadcasted_iota(jnp.int32, (R2, hq), 0) % hq
    k_in = lax.broadcasted_iota(jnp.int32, (R2, hq), 1)
    causal = ((0, 0, k_in <= q_in), (1, 0, None), (1, 1, k_in <= q_in))
    window = ((0, 0, k_in > q_in), (0, 1, None), (1, 1, k_in > q_in))
    nback = WINDOW // tq
    back_tiles = () if mode == "sel" else tuple(range(1, nback + 1))

    def slot_of(t):
        return t if mode == "sel" else i - t

    def cols(t, kh):
        return pl.ds(pl.multiple_of(slot_of(t) * tq + kh * hq, hq), hq)

    def keys(ref, t, kh):
        return ref[pl.ds(pl.multiple_of(t * tq + kh * hq, hq), hq), :]

    def first_pass_full(t):
        s = _dot_nt(qa, k_ref[pl.ds(pl.multiple_of(t * tq, tq), tq), :])
        s_ref[:, pl.ds(pl.multiple_of(slot_of(t) * tq, tq), tq)] = s
        mx_ref[...] = jnp.maximum(mx_ref[...], s)

    def first_pass_blocks(t, blocks, first=False):
        for qh, kh, mask in blocks:
            s = _dot_nt(qa[rows[qh]], keys(k_ref, t, kh))
            if mask is not None:
                s = jnp.where(mask, s, NEG)
            s_ref[rows[qh], cols(t, kh)] = s
            here = (rows[qh], slice(kh * hq, (kh + 1) * hq))
            mx_ref[here] = s if first else jnp.maximum(mx_ref[here], s)

    mx_ref[rows[0], hq:tq] = jnp.full((R2, hq), NEG, f32)
    first_pass_blocks(i, causal, first=True)
    if mode == "sel":
        def max_body(t, c):
            first_pass_full(t)
            return c
        lax.fori_loop(0, i, max_body, 0)
    else:
        for back in back_tiles:
            @pl.when(i >= back)
            def _(back=back):
                if back == nback:
                    first_pass_blocks(i - back, window)
                else:
                    first_pass_full(i - back)
    mx_ref[...] = jnp.broadcast_to(jnp.max(mx_ref[...], axis=-1, keepdims=True), mx_ref.shape)

    def accumulate_full(t):
        s = s_ref[:, pl.ds(pl.multiple_of(slot_of(t) * tq, tq), tq)]
        p = jnp.exp2(s - mx_ref[...]).astype(bf16)
        acc_ref[...] += jnp.dot(p, v_ref[pl.ds(pl.multiple_of(t * tq, tq), tq), :], preferred_element_type=f32)

    def accumulate_blocks(t, blocks, first=False):
        seen = set()
        for qh, kh, _ in blocks:
            p = jnp.exp2(s_ref[rows[qh], cols(t, kh)] - mx_ref[rows[qh], 0:hq]).astype(bf16)
            pv = jnp.dot(p, keys(v_ref, t, kh), preferred_element_type=f32)
            if first and qh not in seen:
                acc_ref[rows[qh], :] = pv
            else:
                acc_ref[rows[qh], :] += pv
            seen.add(qh)

    accumulate_blocks(i, causal, first=True)
    if mode == "sel":
        def acc_body(t, c):
            accumulate_full(t)
            return c
        lax.fori_loop(0, i, acc_body, 0)
    else:
        for back in back_tiles:
            @pl.when(i >= back)
            def _(back=back):
                if back == nback:
                    accumulate_blocks(i - back, window)
                else:
                    accumulate_full(i - back)

    acc = acc_ref[...]
    o = acc / pltpu.roll(acc, HEAD_DIM, 1)
    lane = _lane_iota((hq, V7X_LANES))
    for h in range(2):
        for p in range(G // 2):
            a = o[h * R2 + (2 * p) * hq:h * R2 + (2 * p + 1) * hq]
            b = o[h * R2 + (2 * p + 1) * hq:h * R2 + (2 * p + 2) * hq]
            o_ref[h * hq:(h + 1) * hq, p * V7X_LANES:(p + 1) * V7X_LANES] = jnp.where(
                lane < HEAD_DIM, a, pltpu.roll(b, HEAD_DIM, 1))


def _attn(q_pad, bias, k_pad, v_pad, *, seq_len, tq, mode):
    assert mode in ("sel", "win") and WINDOW % tq == 0
    T = q_pad.shape[0]
    B = T // seq_len
    nq = seq_len // tq
    G = NSA_GROUP
    qrow = lambda b, g, i: (b * nq + i, g)
    kv = lambda b, g, i: (b, g)
    in_specs = [pl.BlockSpec((tq, G * V7X_LANES), qrow)]
    args = [q_pad]
    if mode == "sel":
        in_specs.append(pl.BlockSpec((tq, V7X_LANES), qrow))
        args.append(bias)
    in_specs += [pl.BlockSpec((seq_len, V7X_LANES), kv), pl.BlockSpec((seq_len, V7X_LANES), kv)]
    args += [k_pad, v_pad]
    return pl.pallas_call(
        functools.partial(_attn_kernel, tq=tq, mode=mode),
        grid=(B, NSA_KV_HEADS, nq),
        in_specs=in_specs,
        out_specs=pl.BlockSpec((tq, G * HEAD_DIM), qrow),
        out_shape=jax.ShapeDtypeStruct((T, Q_WIDTH), f32),
        scratch_shapes=[pltpu.VMEM((G * tq, tq), f32), pltpu.VMEM((G * tq, V7X_LANES), f32),
                        pltpu.VMEM((G * tq, seq_len if mode == "sel" else WINDOW + tq), f32)],
        compiler_params=_cparams("parallel", "parallel", "parallel"),
        name="nsa_attn_" + mode,
    )(*args)


def _nsa_out_kernel(y_ref, oc_ref, os_ref, ow_ref, gate_ref, e_ref, wout_ref, out_ref):
    g = gate_ref[...]
    ghi = g.astype(bf16)
    glo = (g - ghi.astype(f32)).astype(bf16)
    o = jnp.zeros(oc_ref.shape, f32)
    for j, br in enumerate((oc_ref, os_ref, ow_ref)):
        ge = (jnp.dot(ghi, e_ref[j], preferred_element_type=f32) + jnp.dot(glo, e_ref[j], preferred_element_type=f32))
        o = o + ge * br[...]
    out_ref[...] = y_ref[...] + jnp.dot(o.astype(bf16), wout_ref[...], preferred_element_type=f32)


def _nsa_out(y, oc, osel, ow, gates, expand, wout, *, tm):
    T = y.shape[0]
    row = lambda i: (i, 0)
    wide = pl.BlockSpec((tm, Q_WIDTH), row)
    return pl.pallas_call(
        _nsa_out_kernel,
        grid=(T // tm,),
        in_specs=[pl.BlockSpec((tm, D_MODEL), row), wide, wide, wide, pl.BlockSpec((tm, V7X_LANES), row),
                  pl.BlockSpec((3, V7X_LANES, Q_WIDTH), lambda i: (0, 0, 0)),
                  pl.BlockSpec((Q_WIDTH, D_MODEL), lambda i: (0, 0))],
        out_specs=pl.BlockSpec((tm, D_MODEL), row),
        out_shape=jax.ShapeDtypeStruct((T, D_MODEL), f32),
        compiler_params=_cparams("parallel"),
        name="nsa_out",
    )(y, oc, osel, ow, gates, expand, wout)


DEC_ROWS = V7X_SUBLANES


def _page_gather_kernel(pt_ref, tbl_ref, o_ref, *, n_pages):
    b = pl.program_id(0)
    for p in range(n_pages):
        o_ref[p:p + 1, :] = tbl_ref[pl.ds(pt_ref[b, p], 1), :]


def _page_gather(page_table, tbl):
    DB, n_pages = page_table.shape
    n_phys, W = tbl.shape
    return pl.pallas_call(
        functools.partial(_page_gather_kernel, n_pages=n_pages),
        grid_spec=pltpu.PrefetchScalarGridSpec(
            num_scalar_prefetch=1,
            grid=(DB,),
            in_specs=[pl.BlockSpec((n_phys, W), lambda b, pt: (0, 0), pipeline_mode=pl.Buffered(1))],
            out_specs=pl.BlockSpec((None, n_pages, W), lambda b, pt: (b, 0, 0)),
        ),
        out_shape=jax.ShapeDtypeStruct((DB, n_pages, W), tbl.dtype),
        compiler_params=_cparams("arbitrary"),
        name="nsa_page_gather",
    )(page_table, tbl)


PAGE_BLOCKS = PAGE_SIZE // NSA_BLOCK
PAGE_ROWS = 2 * KV_WIDTH


CMP_PITCH = HEAD_DIM + V7X_SUBLANES


def _compress_pages_kernel(x_hbm, pe_ref, w_ref, o_ref, buf_ref, sem_ref, *, first, steps, M):
    i = pl.program_id(0)

    def copies(step, slot):
        return [pltpu.make_async_copy(x_hbm.at[first + step * M + k], buf_ref.at[slot, pl.ds(k * CMP_PITCH, HEAD_DIM), :],
                                      sem_ref.at[slot]) for k in range(M)]

    @pl.when(i == 0)
    def _():
        for c in copies(0, 0):
            c.start()

    @pl.when(i + 1 < steps)
    def _():
        for c in copies(i + 1, (i + 1) % 2):
            c.start()

    slot = i % 2
    for c in copies(i, slot):
        c.wait()
    buf = buf_ref.at[slot]
    acc = jnp.zeros((M, 2 * V7X_LANES), f32)
    for d in range(HEAD_DIM):
        a = buf[pl.ds(d, M, stride=CMP_PITCH), :].reshape(M // 8, 8, V7X_LANES) + pe_ref[d]
        acc = acc + jnp.dot(a.reshape(M, V7X_LANES).astype(bf16), w_ref[d], preferred_element_type=f32)
    is_k = lax.broadcasted_iota(jnp.int32, (M, V7X_LANES), 0) % (2 * NSA_KV_HEADS) < NSA_KV_HEADS
    o_ref[...] = jnp.where(is_k, acc[:, :V7X_LANES], acc[:, V7X_LANES:])


def _compress_pages(slabs, pe8, w2, *, layer, n_phys, pages):
    assert n_phys % pages == 0
    steps = n_phys // pages
    per_page = PAGE_ROWS // HEAD_DIM
    M = pages * per_page
    return pl.pallas_call(
        functools.partial(_compress_pages_kernel, first=layer * n_phys * per_page, steps=steps, M=M),
        grid=(steps,),
        in_specs=[pl.BlockSpec(memory_space=pl.ANY),
                  pl.BlockSpec(pe8.shape, lambda i: (0, 0, 0)),
                  pl.BlockSpec(w2.shape, lambda i: (0, 0, 0), pipeline_mode=pl.Buffered(1))],
        out_specs=pl.BlockSpec((M, V7X_LANES), lambda i: (i, 0)),
        out_shape=jax.ShapeDtypeStruct((steps * M, V7X_LANES), f32),
        scratch_shapes=[pltpu.VMEM((2, M * CMP_PITCH, PAGE_SIZE), f32), pltpu.SemaphoreType.DMA((2,))],
        compiler_params=_cparams("arbitrary"),
        name="nsa_compress_pages",
    )(slabs, pe8, w2)


def _cmp_decode_kernel(q_ref, ckv_ref, oc_ref, bias_ref, *, past, n_top):
    R = DEC_ROWS
    n_pages = ckv_ref.shape[0]
    nb = PAGE_BLOCKS * n_pages
    KVH, G = NSA_KV_HEADS, NSA_GROUP
    lane = lax.broadcasted_iota(jnp.int32, (R, nb), 1)
    blk = PAGE_BLOCKS * (lane % n_pages) + lane // n_pages
    pos = past + lax.broadcasted_iota(jnp.int32, (R, nb), 0)
    complete = (blk + 1) * NSA_BLOCK - 1 <= pos
    cand = blk < lax.shift_right_logical(pos, int(math.log2(NSA_BLOCK)))

    def blocks(c, g):
        lo = (c * KVH + g) * PAGE_BLOCKS * HEAD_DIM
        return jnp.concatenate([ckv_ref[:, lo + h * HEAD_DIM:lo + (h + 1) * HEAD_DIM] for h in range(PAGE_BLOCKS)],
                               axis=0).astype(bf16)

    ck = [blocks(0, g) for g in range(KVH)]
    cv = [blocks(1, g) for g in range(KVH)]
    heads = [(g, n) for g in range(KVH) for n in range(G)]
    qh = [q_ref[:, (g * G + n) * HEAD_DIM:(g * G + n + 1) * HEAD_DIM] for g, n in heads]
    sm = [jnp.where(complete, _dot_nt(qh[i], ck[g]) * ATTN_SCALE, NEG) for i, (g, n) in enumerate(heads)]
    e = [jnp.where(complete, jnp.exp(s - jnp.max(s, axis=-1, keepdims=True)), 0.0) for s in sm]
    den = [jnp.sum(x, axis=-1, keepdims=True) for x in e]
    p = [x / jnp.where(d > 0.0, d, 1.0) for x, d in zip(e, den)]
    for i, (g, n) in enumerate(heads):
        lo = (g * G + n) * HEAD_DIM
        oc_ref[:, lo:lo + HEAD_DIM] = _dot(p[i], cv[g])
    imp = jnp.concatenate([sum(p[g * G:(g + 1) * G]) for g in range(KVH)], axis=0)
    cand4 = jnp.concatenate([cand] * KVH, axis=0)
    blk4 = jnp.concatenate([blk] * KVH, axis=0)
    vals = jnp.where(cand4, imp, -1.0)
    cnts = [jnp.zeros(vals.shape, jnp.int32) for _ in range(4)]
    for s in range(1, nb):
        other = pltpu.roll(vals, s, 1)
        other_blk = pltpu.roll(blk4, s, 1)
        cnts[s % 4] = cnts[s % 4] + ((other > vals) | ((other == vals) & (other_blk < blk4))).astype(jnp.int32)
    cnt = (cnts[0] + cnts[1]) + (cnts[2] + cnts[3])
    bias = jnp.where(cand4 & (cnt < n_top), 0.0, NEG)
    for g in range(KVH):
        bias_ref[g] = bias[g * R:(g + 1) * R]


def _cmp_decode(q_raw, ckv, *, past):
    DB, n_pages = ckv.shape[:2]
    nb = PAGE_BLOCKS * n_pages
    assert nb == V7X_LANES
    return pl.pallas_call(
        functools.partial(_cmp_decode_kernel, past=past, n_top=min(N_SEL - 1, nb)),
        grid=(DB,),
        in_specs=[pl.BlockSpec((None, DEC_ROWS, Q_WIDTH), lambda b: (b, 0, 0)),
                  pl.BlockSpec((None,) + ckv.shape[1:], lambda b: (b, 0, 0))],
        out_specs=[pl.BlockSpec((None, DEC_ROWS, Q_WIDTH), lambda b: (b, 0, 0)),
                   pl.BlockSpec((None, NSA_KV_HEADS, DEC_ROWS, nb), lambda b: (b, 0, 0, 0))],
        out_shape=[jax.ShapeDtypeStruct((DB, DEC_ROWS, Q_WIDTH), f32),
                   jax.ShapeDtypeStruct((DB, NSA_KV_HEADS, DEC_ROWS, nb), f32)],
        compiler_params=_cparams("parallel"),
        name="nsa_cmp_decode",
    )(q_raw, ckv)


SEL_PAGES_PER_STEP = 16


def _sel_decode_kernel(pt_ref, q_ref, bias_ref, new_ref, pool_ref, o_ref, buf_ref, sem_ref, m_ref, l_ref, acc_ref,
                       *, layer, n_seq, n_pages, n_new):
    PG = SEL_PAGES_PER_STEP
    NG = n_pages // PG
    R = NSA_GROUP * DEC_ROWS
    keys = PG * PAGE_SIZE

    def copies(step, slot):
        b = step // NG
        j = step % NG
        return [pltpu.make_async_copy(pool_ref.at[layer, pt_ref[b, j * PG + p]], buf_ref.at[slot, p],
                                      sem_ref.at[slot]) for p in range(PG)]

    KVH = NSA_KV_HEADS

    def online(s, pv):
        m_old = [m_ref[g] for g in range(KVH)]
        m_new = [jnp.maximum(m_old[g], jnp.max(s[g], axis=-1, keepdims=True)) for g in range(KVH)]
        alpha = [jnp.exp2(m_old[g] - m_new[g]) for g in range(KVH)]
        p = [jnp.exp2(s[g] - m_new[g]) for g in range(KVH)]
        psum = [jnp.sum(p[g], axis=-1, keepdims=True) for g in range(KVH)]
        pvs = [pv[g](p[g]) for g in range(KVH)]
        for g in range(KVH):
            l_ref[g] = alpha[g] * l_ref[g] + psum[g]
            acc_ref[g] = alpha[g] * acc_ref[g] + pvs[g]
            m_ref[g] = m_new[g]

    for c in copies(0, 0):
        c.start()

    def body(step, carry):
        slot = step % 2
        b = step // NG
        j = step % NG

        @pl.when(step + 1 < n_seq * NG)
        def _():
            for c in copies(step + 1, 1 - slot):
                c.start()

        @pl.when(j == 0)
        def _():
            m_ref[...] = jnp.full(m_ref.shape, NEG, f32)
            l_ref[...] = jnp.zeros(l_ref.shape, f32)
            acc_ref[...] = jnp.zeros(acc_ref.shape, f32)
            tok = lax.broadcasted_iota(jnp.int32, (R, DEC_ROWS), 0) % DEC_ROWS
            key = lax.broadcasted_iota(jnp.int32, (R, DEC_ROWS), 1)
            causal = (key <= tok) & (key < n_new)
            qn = [q_ref[b, g][:, :HEAD_DIM] for g in range(KVH)]
            kn = [new_ref[b, :, g * HEAD_DIM:(g + 1) * HEAD_DIM] for g in range(KVH)]
            vn = [new_ref[b, :, KV_WIDTH + g * HEAD_DIM:KV_WIDTH + (g + 1) * HEAD_DIM] for g in range(KVH)]
            online([jnp.where(causal, _dot_nt(qn[g], kn[g]), NEG) for g in range(KVH)],
                   [functools.partial(lambda p, v: _dot(p, v), v=vn[g]) for g in range(KVH)])

        for c in copies(step, slot):
            c.wait()
        lane = lax.broadcasted_iota(jnp.int32, (V7X_LANES, keys), 0)
        key = lax.broadcasted_iota(jnp.int32, (V7X_LANES, keys), 1)
        kpage = j * PG + lax.shift_right_logical(key, int(math.log2(PAGE_SIZE)))
        khalf = lax.shift_right_logical(key, int(math.log2(NSA_BLOCK))) % PAGE_BLOCKS
        expand = (lane == khalf * n_pages + kpage).astype(bf16)
        qg = [q_ref[b, g][:, :HEAD_DIM] for g in range(KVH)]
        kt = [jnp.concatenate([buf_ref[slot, p, 0, g] for p in range(PG)], axis=1).astype(bf16) for g in range(KVH)]
        vt = [jnp.concatenate([buf_ref[slot, p, 1, g] for p in range(PG)], axis=1).astype(bf16) for g in range(KVH)]
        bias = [jnp.concatenate([bias_ref[b, g]] * NSA_GROUP, axis=0).astype(bf16) for g in range(KVH)]
        s = [_dot(qg[g], kt[g]) + jnp.dot(bias[g], expand, preferred_element_type=f32) for g in range(KVH)]
        online(s, [functools.partial(lambda p, v: _dot_nt(p, v), v=vt[g]) for g in range(KVH)])

        @pl.when(j == NG - 1)
        def _():
            o = [acc_ref[g] / l_ref[g] for g in range(KVH)]
            for g in range(KVH):
                for n in range(NSA_GROUP):
                    lo = (g * NSA_GROUP + n) * HEAD_DIM
                    o_ref[b, :, lo:lo + HEAD_DIM] = o[g][n * DEC_ROWS:(n + 1) * DEC_ROWS]

        return carry

    lax.fori_loop(0, n_seq * NG, body, 0)


def _sel_decode(page_table, q, bias, new_kv, pool, *, layer, n_new):
    DB, n_pages = page_table.shape
    assert n_pages % SEL_PAGES_PER_STEP == 0 and n_pages * PAGE_BLOCKS == V7X_LANES
    R = NSA_GROUP * DEC_ROWS
    whole = lambda shape: pl.BlockSpec(shape, lambda i, pt, _n=len(shape): (0,) * _n)
    return pl.pallas_call(
        functools.partial(_sel_decode_kernel, layer=layer, n_seq=DB, n_pages=n_pages, n_new=n_new),
        grid_spec=pltpu.PrefetchScalarGridSpec(
            num_scalar_prefetch=1,
            grid=(1,),
            in_specs=[whole(q.shape), whole(bias.shape), whole(new_kv.shape), pl.BlockSpec(memory_space=pl.ANY)],
            out_specs=whole((DB, DEC_ROWS, Q_WIDTH)),
            scratch_shapes=[pltpu.VMEM((2, SEL_PAGES_PER_STEP) + pool.shape[2:], f32),
                            pltpu.SemaphoreType.DMA((2,)),
                            pltpu.VMEM((NSA_KV_HEADS, R, 1), f32), pltpu.VMEM((NSA_KV_HEADS, R, 1), f32),
                            pltpu.VMEM((NSA_KV_HEADS, R, HEAD_DIM), f32)],
        ),
        out_shape=jax.ShapeDtypeStruct((DB, DEC_ROWS, Q_WIDTH), f32),
        compiler_params=_cparams("arbitrary"),
        name="nsa_sel_decode",
    )(page_table, q, bias, new_kv, pool)


def _win_decode_kernel(q_ref, buf_ref, new_ref, o_ref, *, n_new):
    wbuf = buf_ref.shape[-1]
    R = NSA_GROUP * DEC_ROWS
    KVH = NSA_KV_HEADS
    tok = lax.broadcasted_iota(jnp.int32, (R, wbuf), 0) % DEC_ROWS
    dist = wbuf + tok - lax.broadcasted_iota(jnp.int32, (R, wbuf), 1)
    in_buf = (dist >= 0) & (dist < WINDOW)
    tokn = lax.broadcasted_iota(jnp.int32, (R, DEC_ROWS), 0) % DEC_ROWS
    keyn = lax.broadcasted_iota(jnp.int32, (R, DEC_ROWS), 1)
    in_new = (keyn <= tokn) & (keyn < n_new)
    hs = range(KVH)
    q = [q_ref[g][:, :HEAD_DIM] for g in hs]
    kn = [new_ref[:, g * HEAD_DIM:(g + 1) * HEAD_DIM] for g in hs]
    vn = [new_ref[:, KV_WIDTH + g * HEAD_DIM:KV_WIDTH + (g + 1) * HEAD_DIM] for g in hs]
    s1 = [jnp.where(in_buf, _dot(q[g], buf_ref[0, g]), NEG) for g in hs]
    s2 = [jnp.where(in_new, _dot_nt(q[g], kn[g]), NEG) for g in hs]
    m = [jnp.maximum(jnp.max(s1[g], axis=-1, keepdims=True), jnp.max(s2[g], axis=-1, keepdims=True)) for g in hs]
    p1 = [jnp.exp2(s1[g] - m[g]) for g in hs]
    p2 = [jnp.exp2(s2[g] - m[g]) for g in hs]
    den = [jnp.sum(p1[g], axis=-1, keepdims=True) + jnp.sum(p2[g], axis=-1, keepdims=True) for g in hs]
    o = [(_dot_nt(p1[g], buf_ref[1, g]) + _dot(p2[g], vn[g])) / den[g] for g in hs]
    for g in hs:
        for n in range(NSA_GROUP):
            lo = (g * NSA_GROUP + n) * HEAD_DIM
            o_ref[:, lo:lo + HEAD_DIM] = o[g][n * DEC_ROWS:(n + 1) * DEC_ROWS]


def _win_decode(q, win_t, new_kv, *, layer, n_new):
    DB = q.shape[0]
    R = NSA_GROUP * DEC_ROWS
    return pl.pallas_call(
        functools.partial(_win_decode_kernel, n_new=n_new),
        grid=(DB,),
        in_specs=[pl.BlockSpec((None, NSA_KV_HEADS, R, V7X_LANES), lambda b: (b, 0, 0, 0)),
                  pl.BlockSpec((None, None) + win_t.shape[2:], lambda b: (layer, b, 0, 0, 0, 0)),
                  pl.BlockSpec((None, DEC_ROWS, 2 * KV_WIDTH), lambda b: (b, 0, 0))],
        out_specs=pl.BlockSpec((None, DEC_ROWS, Q_WIDTH), lambda b: (b, 0, 0)),
        out_shape=jax.ShapeDtypeStruct((DB, DEC_ROWS, Q_WIDTH), f32),
        compiler_params=_cparams("parallel"),
        name="nsa_win_decode",
    )(q, win_t, new_kv)


def _rope_tables(pos):
    half = HEAD_DIM // 2
    inv = ROPE_THETA ** (-jnp.arange(half, dtype=f32) / half)
    ang = pos.astype(f32)[:, None] * inv[None, :]
    cos, sin = jnp.cos(ang), jnp.sin(ang)
    reps = V7X_LANES // HEAD_DIM
    return jnp.tile(cos, (1, 2 * reps)), jnp.tile(jnp.concatenate([-sin, sin], axis=1), (1, reps))


def _block_onehot(pos):
    return (jnp.arange(V7X_LANES, dtype=jnp.int32)[None, :] == HEAD_DIM + pos[:, None] // NSA_BLOCK).astype(f32)


def _gate_expand():
    src = jnp.arange(V7X_LANES)[None, :, None]
    head = jnp.arange(Q_WIDTH)[None, None, :] // HEAD_DIM
    j = jnp.arange(3)[:, None, None]
    return (src == head * 3 + j).astype(bf16)


def _compress_page_weights(pe, w):
    assert PAGE_BLOCKS == 2
    wt = jnp.transpose(w, (2, 0, 1, 3)).astype(bf16)
    z = jnp.zeros_like(wt)
    w2 = jnp.concatenate([jnp.concatenate([wt, z], axis=-1), jnp.concatenate([z, wt], axis=-1)], axis=1)
    w2 = w2.reshape(HEAD_DIM, PAGE_SIZE, 2 * PAGE_BLOCKS * HEAD_DIM)
    pe8 = jnp.broadcast_to(pe.transpose(2, 1, 0)[:, :, None, None, :],
                           (HEAD_DIM, 2, NSA_KV_HEADS, PAGE_BLOCKS, NSA_BLOCK))
    return pe8.reshape(HEAD_DIM, 2 * NSA_KV_HEADS, PAGE_SIZE), w2


def _token_minor(cache):
    n = cache.ndim
    return jnp.transpose(cache, tuple(range(n - 4)) + (n - 3, n - 2, n - 1, n - 4))


def _time_major(x):
    return jnp.swapaxes(x, 0, 1).reshape((x.shape[0] * x.shape[1],) + x.shape[2:])


def _batch_major(x, db):
    return jnp.swapaxes(x.reshape((x.shape[0] // db, db) + x.shape[1:]), 0, 1)


def _gdn_weights(w_in, a_log, dt_bias):
    W3 = 3 * GDN_WIDTH
    pad = V7X_LANES - 2 * GDN_HEADS
    return (w_in[:, :W3].astype(bf16), w_in[:, W3:W3 + GDN_WIDTH].astype(bf16),
            jnp.pad(w_in[:, W3 + GDN_WIDTH:], ((0, 0), (0, pad))).astype(bf16),
            jnp.pad(a_log, (0, V7X_LANES - GDN_HEADS))[None], jnp.pad(dt_bias, (0, V7X_LANES - GDN_HEADS))[None])


def _gdn_layer(yp, ys, S0s, conv_s, nw, w_in, conv_w, a_log, dt_bias, norm_w, w_out, *, B, L, DB, DS):
    wqkv, wz, wab, alog, dtb = _gdn_weights(w_in, a_log, dt_bias)
    wout = w_out.astype(bf16)
    nw = nw[None]
    norm_w = norm_w[None]
    q, k, v, z, slab, conv_p = _gdn_in(yp, None, nw, wqkv, wz, wab, conv_w, alog, dtb, tm=512, shift=1, rows_per_seq=L)
    o, S_p = _gdn_chunk(q, k, v, slab, jnp.zeros((B, GDN_HEADS, GDN_DK, GDN_DK), f32), seq_len=L)
    yp = _gdn_out(yp, o, z, norm_w, wout, tm=512)
    Ts = DS * DB
    hist = _time_major(conv_s).reshape((GDN_CONV - 1) * DB, 3 * GDN_WIDTH)
    q, k, v, z, slab, conv_tm = _gdn_in(ys, hist, nw, wqkv, wz, wab, conv_w, alog, dtb, tm=Ts, shift=DB, rows_per_seq=Ts)

    def chunked(x):
        x = _batch_major(x, DB)
        return jnp.pad(x, ((0, 0), (0, GDN_CHUNK - DS), (0, 0))).reshape(DB * GDN_CHUNK, x.shape[-1])

    o, S_s = _gdn_chunk(chunked(q), chunked(k), chunked(v), chunked(slab), S0s, seq_len=GDN_CHUNK)
    o = _time_major(o.reshape(DB, GDN_CHUNK, GDN_WIDTH)[:, :DS])
    ys = _gdn_out(ys, o, z, norm_w, wout, tm=Ts)
    conv_s_new = _batch_major(conv_tm.reshape((GDN_CONV - 1) * DB, 3 * GDN_WIDTH), DB)
    return yp, ys, S_p, S_s, conv_p, conv_s_new


def _ffn_layer(yp, ys, conv_s, nw, w_up, conv_w, conv_b, w_down, final_nw, *, B, L, DB, DS):
    wup = w_up.astype(bf16)
    wdn = w_down.astype(bf16)
    nw = nw[None]
    cb = conv_b[None]
    yp, st_p = _ffn(yp, None, nw, wup, conv_w, cb, wdn, tm=512, shift=1, rows_per_seq=L, final_nw=final_nw)
    Ts = DS * DB
    hist = _time_major(conv_s).reshape((FFN_CONV - 1) * DB, 2 * D_FF)
    ys, st_tm = _ffn(ys, hist, nw, wup, conv_w, cb, wdn, tm=Ts, shift=DB, rows_per_seq=Ts, final_nw=final_nw)
    st_s = _batch_major(st_tm.reshape((FFN_CONV - 1) * DB, 2 * D_FF), DB)
    return yp, ys, st_p, st_s


def _nsa_layer(yp, ys, j, kv_prev, cmp_t, sel_t, win_t, page_table, nw, w_in, cmp_pe, cmp_w, w_out,
               *, B, L, DB, DS):
    past = page_table.shape[1] * PAGE_SIZE
    n_nsa = cmp_t.shape[0]
    w = jnp.pad(w_in, ((0, 0), (0, NSA_IN_WP - NSA_IN_W))).astype(bf16)
    wout = w_out.astype(bf16)
    nw = nw[None]
    expand = _gate_expand()
    pe8, w2 = _compress_page_weights(cmp_pe, cmp_w)
    tq = WINDOW // 2
    tq_attn = WINDOW
    pos_p = jnp.arange(L, dtype=jnp.int32)
    cos, sin = _rope_tables(pos_p)
    if kv_prev is None:
        kv_prev = tuple(jnp.zeros((n_nsa, B, 2 * KV_WIDTH, L), f32) for _ in range(3))
    (q_raw, q_rot, cmp_l, sel_l, win_l, ksel, vsel, kwin, vwin, gates) = _nsa_in(
        yp, nw, w, cos, sin, _block_onehot(pos_p), tm=256, tiles_per_seq=L // 256, kv_layers=(j, n_nsa, kv_prev))
    kv_layers = (cmp_l, sel_l, win_l)
    nb = L // NSA_BLOCK
    npg = B * L // PAGE_SIZE
    per_page = PAGE_ROWS // HEAD_DIM
    slabs = cmp_l[j].reshape(B, per_page, HEAD_DIM, L // PAGE_SIZE, PAGE_SIZE).transpose(0, 3, 1, 2, 4)
    ckv = _compress_pages(slabs.reshape(npg * per_page, HEAD_DIM, PAGE_SIZE), pe8, w2, layer=0, n_phys=npg,
                          pages=_pick_tile(npg, 16))
    ck = ckv.reshape(B, L // PAGE_SIZE, 2, NSA_KV_HEADS, PAGE_BLOCKS, HEAD_DIM)
    ck = ck.transpose(0, 1, 4, 2, 3, 5).reshape(B, nb, 2, NSA_KV_HEADS, HEAD_DIM)
    eye = jnp.eye(NSA_GROUP, dtype=f32)
    kbd = jnp.einsum("bkgd,nm->bgnkmd", ck[:, :, 0], eye).reshape(B, NSA_KV_HEADS, NSA_GROUP * nb, NSA_GROUP * HEAD_DIM)
    vbd = jnp.einsum("bkgd,nm->bgnkmd", ck[:, :, 1], eye).reshape(B, NSA_KV_HEADS, NSA_GROUP * nb, NSA_GROUP * HEAD_DIM)
    o_c, bias = _cmp_select(q_raw, kbd.astype(bf16), vbd.astype(bf16), seq_len=L, tq=tq)
    o_s = _attn(q_rot, bias, ksel, vsel, seq_len=L, tq=tq_attn, mode="sel")
    o_w = _attn(q_rot, None, kwin, vwin, seq_len=L, tq=tq_attn, mode="win")
    yp = _nsa_out(yp, o_c, o_s, o_w, gates, expand, wout, tm=512)
    Ts = DS * DB
    pos_s = past + jnp.arange(DS, dtype=jnp.int32)
    pos_rows = jnp.repeat(pos_s, DB)
    cos, sin = _rope_tables(pos_rows)
    (q_raw, q_rot, kv_cmp, kv_sel, kv_win, _, _, _, _, gates) = _nsa_in(
        ys, nw, w, cos, sin, jnp.zeros((Ts, V7X_LANES), f32), tm=Ts, tiles_per_seq=1)
    n_layers, n_phys = cmp_t.shape[:2]
    ckv_all = _compress_pages(cmp_t.reshape(n_layers * n_phys * PAGE_ROWS // HEAD_DIM, HEAD_DIM, PAGE_SIZE), pe8, w2,
                              layer=j, n_phys=n_phys, pages=_pick_tile(n_phys, 16))
    ckv = _page_gather(page_table, ckv_all.reshape(n_phys, 2 * KV_WIDTH * PAGE_BLOCKS))

    def rows8(x):
        return jnp.pad(_batch_major(x, DB), ((0, 0), (0, DEC_ROWS - DS), (0, 0)))

    o_c, bias = _cmp_decode(rows8(q_raw), ckv, past=past)
    q4 = rows8(q_rot).reshape(DB, DEC_ROWS, NSA_KV_HEADS, NSA_GROUP, V7X_LANES)
    q4 = q4.transpose(0, 2, 3, 1, 4).reshape(DB, NSA_KV_HEADS, NSA_GROUP * DEC_ROWS, V7X_LANES)
    o_s = _sel_decode(page_table, q4, bias, rows8(kv_sel), sel_t, layer=j, n_new=DS)
    o_w = _win_decode(q4, win_t, rows8(kv_win), layer=j, n_new=DS)
    ks = (DB, DS, 2, NSA_KV_HEADS, HEAD_DIM)
    cmp_s, sel_s, win_new = (_batch_major(t, DB).reshape(ks) for t in (kv_cmp, kv_sel, kv_win))
    tm_rows = lambda o: _time_major(o[:, :DS])
    ys = _nsa_out(ys, tm_rows(o_c), tm_rows(o_s), tm_rows(o_w), gates, expand, wout, tm=Ts)
    return yp, ys, kv_layers, cmp_s, sel_s, win_new


def kernel(x_prompt, x_sample, state_gdn, state_gdn_conv, cache_cmp, cache_sel, state_win, state_ffn_conv, page_table,
           norm_mix, norm_ffn, norm_final, gdn_w_in, gdn_conv_w, gdn_a_log, gdn_dt_bias, gdn_norm_w, gdn_w_out,
           nsa_w_in, nsa_cmp_pe, nsa_cmp_w, nsa_w_out, ffn_w_up, ffn_conv_w, ffn_conv_b, ffn_w_down):
    B, L, _ = x_prompt.shape
    DB, DS, _ = x_sample.shape
    depth = norm_mix.shape[0]
    dims = dict(B=B, L=L, DB=DB, DS=DS)
    yp = x_prompt.reshape(B * L, D_MODEL)
    ys = _time_major(x_sample)
    outs = [[] for _ in range(12)]
    kv_p = None
    cmp_t, sel_t, win_t = _token_minor(cache_cmp), _token_minor(cache_sel), _token_minor(state_win)
    for i in range(depth):
        j = i // 2
        if i % 2 == 0:
            yp, ys, S_p, S_s, c_p, c_s = _gdn_layer(yp, ys, state_gdn[j], state_gdn_conv[j], norm_mix[i], gdn_w_in[j],
                                                    gdn_conv_w[j], gdn_a_log[j], gdn_dt_bias[j], gdn_norm_w[j],
                                                    gdn_w_out[j], **dims)
            for lst, val in zip(outs[0:4], (S_p, S_s, c_p, c_s)):
                lst.append(val)
        else:
            yp, ys, kv_p, *kv_s = _nsa_layer(yp, ys, j, kv_p, cmp_t, sel_t, win_t, page_table,
                                             norm_mix[i], nsa_w_in[j], nsa_cmp_pe[j], nsa_cmp_w[j], nsa_w_out[j], **dims)
            for lst, val in zip((outs[5], outs[7], outs[9]), kv_s):
                lst.append(val)
        yp, ys, f_p, f_s = _ffn_layer(yp, ys, state_ffn_conv[i], norm_ffn[i], ffn_w_up[i], ffn_conv_w[i],
                                      ffn_conv_b[i], ffn_w_down[i], norm_final[None] if i == depth - 1 else None,
                                      **dims)
        outs[10].append(f_p)
        outs[11].append(f_s)
    y_prompt = yp.reshape(B, L, D_MODEL)
    y_sample = _batch_major(ys, DB)
    rows_of = lambda t: jnp.transpose(t.reshape(t.shape[:2] + (2, NSA_KV_HEADS, HEAD_DIM, t.shape[-1])), (0, 1, 5, 2, 3, 4))
    cmp_p, sel_p, win_p = rows_of(kv_p[0]), rows_of(kv_p[1]), rows_of(kv_p[2][..., L - min(WINDOW, L):])
    stacked = [jnp.stack(o) if o else None for o in outs]
    stacked[4], stacked[6], stacked[8] = cmp_p, sel_p, win_p
    stacked[9] = jnp.concatenate([state_win[:, :, DS:], stacked[9]], axis=2)
    return (y_prompt, y_sample) + tuple(stacked)
```

```python
import functools
import math

import jax
import jax.numpy as jnp
from jax import lax
from jax.experimental import pallas as pl
from jax.experimental.pallas import tpu as pltpu

f32 = jnp.float32
bf16 = jnp.bfloat16

D_MODEL = 1024
GDN_HEADS = 8
GDN_DK = 128
GDN_WIDTH = GDN_HEADS * GDN_DK
GDN_CONV = 4
GDN_CHUNK = 64
NSA_HEADS = 16
NSA_KV_HEADS = 4
NSA_GROUP = 4
HEAD_DIM = 64
Q_WIDTH = NSA_HEADS * HEAD_DIM
KV_WIDTH = NSA_KV_HEADS * HEAD_DIM
NSA_BLOCK = 64
N_SEL = 16
WINDOW = 512
PAGE_SIZE = 128
ROPE_THETA = 10000.0
ATTN_SCALE = HEAD_DIM ** -0.5
LOG2E = math.log2(math.e)
D_FF = 2816
FFN_CONV = 3
RMS_EPS = 1e-6
NEG = -1e30

V7X_LANES = 128
V7X_SUBLANES = 8
V7X_VMEM_BYTES = 64 * 1024 * 1024
VMEM_LIMIT = 56 * 1024 * 1024


def _cparams(*sem):
    return pltpu.CompilerParams(dimension_semantics=sem, vmem_limit_bytes=VMEM_LIMIT)


def _round_up(x, m):
    return (x + m - 1) // m * m


def _pick_tile(n, cap):
    best = None
    for t in range(V7X_SUBLANES, min(n, cap) + 1, V7X_SUBLANES):
        if n % t == 0:
            best = t
    assert best is not None, (n, cap)
    return best


def _dot(a, b):
    return jnp.dot(a.astype(bf16), b.astype(bf16), preferred_element_type=f32)


def _dot_nt(a, b):
    return lax.dot_general(a.astype(bf16), b.astype(bf16), (((1,), (1,)), ((), ())), preferred_element_type=f32)


def _dot_tn(a, b):
    return lax.dot_general(a.astype(bf16), b.astype(bf16), (((0,), (0,)), ((), ())), preferred_element_type=f32)


def _split3(x):
    hi = x.astype(bf16)
    r = x - hi.astype(f32)
    mid = r.astype(bf16)
    lo = (r - mid.astype(f32)).astype(bf16)
    return hi, mid, lo


def _dot_exact_lhs(a01, x):
    a = a01.astype(bf16)
    hi, mid, lo = _split3(x)
    return (jnp.dot(a, hi, preferred_element_type=f32) + jnp.dot(a, mid, preferred_element_type=f32)
            + jnp.dot(a, lo, preferred_element_type=f32))


def _silu(x):
    return x / (1.0 + jnp.exp(-x))


def _sigmoid(x):
    return 1.0 / (1.0 + jnp.exp(-x))


def _rms_rows(x, w):
    return x * lax.rsqrt(jnp.mean(x * x, axis=-1, keepdims=True) + RMS_EPS) * w


def _ffn_kernel(*refs, tm, shift, hist_rows, has_hist, has_final, tiles_per_seq, n_chunk):
    refs = list(refs)
    y_ref = refs.pop(0)
    hist_ref = refs.pop(0) if has_hist else None
    nw_ref, wup_ref, cw_ref, cb_ref, wdn_ref = refs[:5]
    fw_ref = refs[5] if has_final else None
    out_ref, st_ref, ext_ref, g_ref = refs[5 + has_final:]
    H = hist_rows
    s = shift
    LN = V7X_LANES
    nblk = 2 * D_FF // LN
    t = pl.program_id(0) % tiles_per_seq

    @pl.when(t == 0)
    def _():
        if has_hist:
            for c in range(nblk):
                ext_ref[c, H - 2 * s:H, :] = hist_ref[:, c * LN:(c + 1) * LN]
        else:
            ext_ref[:, 0:H, :] = jnp.zeros((nblk, H, LN), f32)

    @pl.when(t != 0)
    def _():
        ext_ref[:, H - 2 * s:H, :] = ext_ref[:, H + tm - 2 * s:H + tm, :]

    y = y_ref[...]
    h = _rms_rows(y, nw_ref[...]).astype(bf16)
    per = n_chunk // LN
    for c in range(2 * D_FF // n_chunk):
        u = jnp.dot(h, wup_ref[:, c * n_chunk:(c + 1) * n_chunk], preferred_element_type=f32)
        for k in range(per):
            ext_ref[c * per + k, H:H + tm, :] = u[:, k * LN:(k + 1) * LN]
    for c in range(nblk):
        st_ref[:, c * LN:(c + 1) * LN] = ext_ref[c, H + tm - 2 * s:H + tm, :]

    def conv(c):
        w = cw_ref[:, c * LN:(c + 1) * LN]
        return (ext_ref[c, H - 2 * s:H - 2 * s + tm, :] * w[0:1, :] + ext_ref[c, H - s:H - s + tm, :] * w[1:2, :]
                + ext_ref[c, H:H + tm, :] * w[2:3, :] + cb_ref[:, c * LN:(c + 1) * LN])

    for j in range(D_FF // LN):
        g_ref[:, j * LN:(j + 1) * LN] = (_silu(conv(j)) * conv(D_FF // LN + j)).astype(bf16)
    res = y + jnp.dot(g_ref[...], wdn_ref[...], preferred_element_type=f32)
    out_ref[...] = _rms_rows(res, fw_ref[...]) if has_final else res


def _ffn(y, hist, nw, wup, cw, cb, wdn, *, tm, shift, rows_per_seq, final_nw=None):
    T = y.shape[0]
    n_seq = T // rows_per_seq
    tiles_per_seq = rows_per_seq // tm
    H = _round_up(2 * shift, V7X_SUBLANES)
    kern = functools.partial(_ffn_kernel, tm=tm, shift=shift, hist_rows=H, has_hist=hist is not None,
                             has_final=final_nw is not None, tiles_per_seq=tiles_per_seq, n_chunk=512)
    const = lambda i: (0, 0)
    in_specs = [pl.BlockSpec((tm, D_MODEL), lambda i: (i, 0))]
    args = [y]
    if hist is not None:
        in_specs.append(pl.BlockSpec((2 * shift, 2 * D_FF), lambda i: (i // tiles_per_seq, 0)))
        args.append(hist)
    in_specs += [pl.BlockSpec((1, D_MODEL), const),
                 pl.BlockSpec((D_MODEL, 2 * D_FF), const, pipeline_mode=pl.Buffered(1)),
                 pl.BlockSpec((FFN_CONV, 2 * D_FF), const),
                 pl.BlockSpec((1, 2 * D_FF), const),
                 pl.BlockSpec((D_FF, D_MODEL), const, pipeline_mode=pl.Buffered(1))]
    args += [nw, wup, cw, cb, wdn]
    if final_nw is not None:
        in_specs.append(pl.BlockSpec((1, D_MODEL), const))
        args.append(final_nw)
    out, st = pl.pallas_call(
        kern,
        grid=(T // tm,),
        in_specs=in_specs,
        out_specs=[pl.BlockSpec((tm, D_MODEL), lambda i: (i, 0)),
                   pl.BlockSpec((None, 2 * shift, 2 * D_FF), lambda i: (i // tiles_per_seq, 0, 0))],
        out_shape=[jax.ShapeDtypeStruct((T, D_MODEL), f32),
                   jax.ShapeDtypeStruct((n_seq, 2 * shift, 2 * D_FF), f32)],
        scratch_shapes=[pltpu.VMEM((2 * D_FF // V7X_LANES, H + tm, V7X_LANES), f32), pltpu.VMEM((tm, D_FF), bf16)],
        compiler_params=_cparams("arbitrary"),
        name="conv_ffn",
    )(*args)
    return out, st


def _gdn_in_kernel(*refs, tm, shift, hist_rows, has_hist, tiles_per_seq):
    if has_hist:
        (y_ref, hist_ref, nw_ref, wqkv_ref, wz_ref, wab_ref, cw_ref, alog_ref, dtb_ref,
         q_ref, k_ref, v_ref, z_ref, slab_ref, st_ref, ext_ref) = refs
    else:
        (y_ref, nw_ref, wqkv_ref, wz_ref, wab_ref, cw_ref, alog_ref, dtb_ref,
         q_ref, k_ref, v_ref, z_ref, slab_ref, st_ref, ext_ref) = refs
        hist_ref = None
    H = hist_rows
    s = shift
    nh = GDN_CONV - 1
    W3 = 3 * GDN_WIDTH
    t = pl.program_id(0) % tiles_per_seq
    hw = GDN_DK
    nblk = W3 // hw

    @pl.when(t == 0)
    def _():
        if has_hist:
            for c in range(nblk):
                ext_ref[c, H - nh * s:H, :] = hist_ref[:, c * hw:(c + 1) * hw]
        else:
            ext_ref[:, 0:H, :] = jnp.zeros((nblk, H, hw), f32)

    @pl.when(t != 0)
    def _():
        ext_ref[:, H - nh * s:H, :] = ext_ref[:, H + tm - nh * s:H + tm, :]

    h = _rms_rows(y_ref[...], nw_ref[...]).astype(bf16)
    nc = 512
    for c in range(W3 // nc):
        u = jnp.dot(h, wqkv_ref[:, c * nc:(c + 1) * nc], preferred_element_type=f32)
        for k in range(nc // hw):
            ext_ref[c * (nc // hw) + k, H:H + tm, :] = u[:, k * hw:(k + 1) * hw]
    for c in range(GDN_WIDTH // nc):
        z_ref[:, c * nc:(c + 1) * nc] = jnp.dot(h, wz_ref[:, c * nc:(c + 1) * nc], preferred_element_type=f32)
    ab = jnp.dot(h, wab_ref[...], preferred_element_type=f32)
    x = ab + dtb_ref[...]
    softplus = jnp.maximum(x, 0.0) + jnp.log(1.0 + jnp.exp(-jnp.abs(x)))
    g = -jnp.exp(alog_ref[...]) * softplus
    lane = lax.broadcasted_iota(jnp.int32, ab.shape, 1)
    slab_ref[...] = jnp.where(lane < GDN_HEADS, g, _sigmoid(ab))
    for c in range(nblk):
        st_ref[:, c * hw:(c + 1) * hw] = ext_ref[c, H + tm - nh * s:H + tm, :]

    for j in range(nblk):
        col = j * hw
        w = cw_ref[:, col:col + hw]
        c = ext_ref[j, H:H + tm, :] * w[nh:nh + 1, :]
        for i in range(nh):
            c = c + ext_ref[j, H - (nh - i) * s:H - (nh - i) * s + tm, :] * w[i:i + 1, :]
        c = _silu(c)
        if j < 2 * GDN_HEADS:
            c = c * lax.rsqrt(jnp.sum(c * c, axis=-1, keepdims=True) + RMS_EPS)
        if j < GDN_HEADS:
            q_ref[:, col:col + hw] = c * (GDN_DK ** -0.5)
        elif j < 2 * GDN_HEADS:
            k_ref[:, col - GDN_WIDTH:col - GDN_WIDTH + hw] = c
        else:
            v_ref[:, col - 2 * GDN_WIDTH:col - 2 * GDN_WIDTH + hw] = c


def _gdn_in(y, hist, nw, wqkv, wz, wab, cw, alog, dtb, *, tm, shift, rows_per_seq):
    T = y.shape[0]
    n_seq = T // rows_per_seq
    tiles_per_seq = rows_per_seq // tm
    nh = GDN_CONV - 1
    H = _round_up(nh * shift, V7X_SUBLANES)
    W3 = 3 * GDN_WIDTH
    kern = functools.partial(_gdn_in_kernel, tm=tm, shift=shift, hist_rows=H, has_hist=hist is not None,
                             tiles_per_seq=tiles_per_seq)
    const = lambda i: (0, 0)
    row = lambda i: (i, 0)
    in_specs = [pl.BlockSpec((tm, D_MODEL), row)]
    args = [y]
    if hist is not None:
        in_specs.append(pl.BlockSpec((nh * shift, W3), lambda i: (i // tiles_per_seq, 0)))
        args.append(hist)
    in_specs += [pl.BlockSpec((1, D_MODEL), const),
                 pl.BlockSpec((D_MODEL, W3), const, pipeline_mode=pl.Buffered(1)),
                 pl.BlockSpec((D_MODEL, GDN_WIDTH), const, pipeline_mode=pl.Buffered(1)),
                 pl.BlockSpec((D_MODEL, V7X_LANES), const),
                 pl.BlockSpec((GDN_CONV, W3), const),
                 pl.BlockSpec((1, V7X_LANES), const),
                 pl.BlockSpec((1, V7X_LANES), const)]
    args += [nw, wqkv, wz, wab, cw, alog, dtb]
    wide = pl.BlockSpec((tm, GDN_WIDTH), row)
    return pl.pallas_call(
        kern,
        grid=(T // tm,),
        in_specs=in_specs,
        out_specs=[wide, wide, wide, wide, pl.BlockSpec((tm, V7X_LANES), row),
                   pl.BlockSpec((None, nh * shift, W3), lambda i: (i // tiles_per_seq, 0, 0))],
        out_shape=[jax.ShapeDtypeStruct((T, GDN_WIDTH), f32)] * 4
        + [jax.ShapeDtypeStruct((T, V7X_LANES), f32), jax.ShapeDtypeStruct((n_seq, nh * shift, W3), f32)],
        scratch_shapes=[pltpu.VMEM((W3 // GDN_DK, H + tm, GDN_DK), f32)],
        compiler_params=_cparams("arbitrary"),
        name="gdn_in",
    )(*args)


def _gdn_chunk_kernel(q_ref, k_ref, v_ref, slab_ref, s0_ref, o_ref, s_ref, *, nseq):
    C = GDN_CHUNK
    H = GDN_HEADS
    n = pl.program_id(1)

    @pl.when(n == 0)
    def _():
        s_ref[...] = s0_ref[...]

    ri = lax.broadcasted_iota(jnp.int32, (C, C), 0)
    ci = lax.broadcasted_iota(jnp.int32, (C, C), 1)
    tril = ri >= ci
    strict = ri > ci
    eye = (ri == ci).astype(f32)
    slab = [slab_ref[s] for s in range(nseq)]
    G = [_dot_exact_lhs(tril.astype(f32), slab[s]) for s in range(nseq)]
    GT = [G[s].T for s in range(nseq)]
    ch = [(s, h) for s in range(nseq) for h in range(H)]
    idx = range(len(ch))
    col = lambda h: slice(h * GDN_DK, (h + 1) * GDN_DK)
    qh = [q_ref[s, :, col(h)] for s, h in ch]
    kh = [k_ref[s, :, col(h)] for s, h in ch]
    vh = [v_ref[s, :, col(h)] for s, h in ch]
    Gc = [G[s][:, h:h + 1] for s, h in ch]
    bc = [slab[s][:, H + h:H + h + 1] for s, h in ch]
    gl = [G[s][C - 1:C, h:h + 1] for s, h in ch]
    decay = [jnp.exp(jnp.where(tril, Gc[i] - GT[s][h:h + 1, :], NEG)) for i, (s, h) in enumerate(ch)]
    kb = [kh[i].astype(bf16) for i in idx]
    A = [jnp.where(strict, bc[i] * _dot_nt(kb[i], kb[i]) * decay[i], 0.0) for i in idx]
    P = [eye - A[i] for i in idx]
    Ap = A
    for _ in range(int(math.log2(C)) - 1):
        Ap = [_dot(Ap[i], Ap[i]) for i in idx]
        P = [P[i] + _dot(P[i], Ap[i]) for i in idx]
    Tb = [P[i].astype(bf16) for i in idx]
    uv = [_dot(Tb[i], vh[i] * bc[i]) for i in idx]
    wk = [_dot(Tb[i], kh[i] * (bc[i] * jnp.exp(Gc[i]))) for i in idx]
    qk = [_dot_nt(qh[i], kb[i]) * decay[i] for i in idx]
    kdec = [kh[i] * jnp.exp(gl[i] - Gc[i]) for i in idx]
    S = [s_ref[s, h] for s, h in ch]
    Sb = [S[i].astype(bf16) for i in idx]
    u = [uv[i] - _dot(wk[i], Sb[i]) for i in idx]
    o = [_dot(qh[i] * jnp.exp(Gc[i]), Sb[i]) + _dot(qk[i], u[i]) for i in idx]
    Sn = [S[i] * jnp.exp(gl[i]) + _dot_tn(kdec[i], u[i]) for i in idx]
    for i, (s, h) in enumerate(ch):
        o_ref[s, :, col(h)] = o[i]
        s_ref[s, h] = Sn[i]


GDN_SEQS_PER_STEP = 8


def _gdn_chunk(q, k, v, slab, s0, *, seq_len):
    T = q.shape[0]
    B = T // seq_len
    N = seq_len // GDN_CHUNK
    nseq = math.gcd(B, GDN_SEQS_PER_STEP)
    blk = lambda w: pl.BlockSpec((nseq, GDN_CHUNK, w), lambda b, n: (b, n, 0))
    st = pl.BlockSpec((nseq, GDN_HEADS, GDN_DK, GDN_DK), lambda b, n: (b, 0, 0, 0))
    seqs = lambda x: x.reshape(B, seq_len, x.shape[-1])
    o, s_out = pl.pallas_call(
        functools.partial(_gdn_chunk_kernel, nseq=nseq),
        grid=(B // nseq, N),
        in_specs=[blk(GDN_WIDTH), blk(GDN_WIDTH), blk(GDN_WIDTH), blk(V7X_LANES), st],
        out_specs=[blk(GDN_WIDTH), st],
        out_shape=[jax.ShapeDtypeStruct((B, seq_len, GDN_WIDTH), f32),
                   jax.ShapeDtypeStruct((B, GDN_HEADS, GDN_DK, GDN_DK), f32)],
        compiler_params=_cparams("parallel", "arbitrary"),
        name="gdn_chunk",
    )(seqs(q), seqs(k), seqs(v), seqs(slab), s0)
    return o.reshape(T, GDN_WIDTH), s_out


def _gdn_out_kernel(y_ref, o_ref, z_ref, nw_ref, wout_ref, out_ref, g_ref):
    for h in range(GDN_HEADS):
        lo = h * GDN_DK
        o = _rms_rows(o_ref[:, lo:lo + GDN_DK], nw_ref[...])
        g_ref[:, lo:lo + GDN_DK] = (o * _silu(z_ref[:, lo:lo + GDN_DK])).astype(bf16)
    out_ref[...] = y_ref[...] + jnp.dot(g_ref[...], wout_ref[...], preferred_element_type=f32)


def _gdn_out(y, o, z, nw, wout, *, tm):
    T = y.shape[0]
    row = lambda i: (i, 0)
    const = lambda i: (0, 0)
    wide = pl.BlockSpec((tm, GDN_WIDTH), row)
    return pl.pallas_call(
        _gdn_out_kernel,
        grid=(T // tm,),
        in_specs=[pl.BlockSpec((tm, D_MODEL), row), wide, wide, pl.BlockSpec((1, GDN_DK), const),
                  pl.BlockSpec((GDN_WIDTH, D_MODEL), const)],
        out_specs=pl.BlockSpec((tm, D_MODEL), row),
        out_shape=jax.ShapeDtypeStruct((T, D_MODEL), f32),
        scratch_shapes=[pltpu.VMEM((tm, GDN_WIDTH), bf16)],
        compiler_params=_cparams("parallel"),
        name="gdn_out",
    )(y, o, z, nw, wout)


NSA_IN_W = Q_WIDTH + 6 * KV_WIDTH + 3 * NSA_HEADS
NSA_IN_WP = _round_up(NSA_IN_W, V7X_LANES)
KV0 = Q_WIDTH
GATE0 = Q_WIDTH + 6 * KV_WIDTH


def _lane_iota(shape):
    return lax.broadcasted_iota(jnp.int32, shape, 1)


def _rope_pair(x, cos, sin):
    half = HEAD_DIM // 2
    lane = _lane_iota(x.shape)
    rot = jnp.where(lane % HEAD_DIM < half, pltpu.roll(x, V7X_LANES - half, 1), pltpu.roll(x, half, 1))
    return x * cos + rot * sin


def _split_pair(x, fill):
    lane = _lane_iota(x.shape)
    return jnp.where(lane < HEAD_DIM, x, fill), jnp.where(lane < HEAD_DIM, pltpu.roll(x, HEAD_DIM, 1), fill)


def _nsa_in_kernel(*refs, token_minor, n_alias):
    y_ref, nw_ref, w_ref, cos_ref, sin_ref, oh_ref = refs[:6]
    (qraw_ref, qrot_ref, cmp_ref, sel_ref, win_ref, ksel_ref, vsel_ref, kwin_ref, vwin_ref, gate_ref,
     proj_ref) = refs[6 + n_alias:]

    def put(o_ref, lo, x):
        if token_minor:
            o_ref[lo:lo + V7X_LANES, :] = x.T
        else:
            o_ref[:, lo:lo + V7X_LANES] = x

    h = _rms_rows(y_ref[...], nw_ref[...]).astype(bf16)
    nc = NSA_IN_WP // 3
    for c in range(3):
        proj_ref[:, c * nc:(c + 1) * nc] = jnp.dot(h, w_ref[:, c * nc:(c + 1) * nc], preferred_element_type=f32)
    cos = cos_ref[...]
    sin = sin_ref[...]
    oh = oh_ref[...]
    L2 = 2 * V7X_LANES
    for p in range(Q_WIDTH // V7X_LANES):
        x = proj_ref[:, p * V7X_LANES:(p + 1) * V7X_LANES]
        qraw_ref[:, p * V7X_LANES:(p + 1) * V7X_LANES] = x.astype(bf16)
        a, b = _split_pair(_rope_pair(x, cos, sin) * (ATTN_SCALE * LOG2E), 0.0)
        qrot_ref[:, p * L2:p * L2 + V7X_LANES] = a.astype(bf16)
        qrot_ref[:, p * L2 + V7X_LANES:(p + 1) * L2] = b.astype(bf16)
    for p in range(2 * KV_WIDTH // V7X_LANES):
        put(cmp_ref, p * V7X_LANES, proj_ref[:, KV0 + p * V7X_LANES:KV0 + (p + 1) * V7X_LANES])
    for br, (o_ref, k_ref, v_ref, fill) in enumerate(((sel_ref, ksel_ref, vsel_ref, oh), (win_ref, kwin_ref, vwin_ref, 0.0))):
        base = KV0 + (br + 1) * 2 * KV_WIDTH
        for p in range(KV_WIDTH // V7X_LANES):
            lo = p * V7X_LANES
            kx = _rope_pair(proj_ref[:, base + lo:base + lo + V7X_LANES], cos, sin)
            put(o_ref, lo, kx)
            a, b = _split_pair(kx, fill)
            k_ref[:, p * L2:p * L2 + V7X_LANES] = a.astype(bf16)
            k_ref[:, p * L2 + V7X_LANES:(p + 1) * L2] = b.astype(bf16)
            vx = proj_ref[:, base + KV_WIDTH + lo:base + KV_WIDTH + lo + V7X_LANES]
            put(o_ref, KV_WIDTH + lo, vx)
            a, b = _split_pair(vx, 1.0)
            v_ref[:, p * L2:p * L2 + V7X_LANES] = a.astype(bf16)
            v_ref[:, p * L2 + V7X_LANES:(p + 1) * L2] = b.astype(bf16)
    gate_ref[...] = _sigmoid(proj_ref[:, GATE0:GATE0 + V7X_LANES])


def _nsa_in(y, nw, w, cos, sin, onehot, *, tm, tiles_per_seq, kv_layers=None):
    T = y.shape[0]
    row = lambda i: (i, 0)
    const = lambda i: (0, 0)
    tab = lambda i: (i % tiles_per_seq, 0)
    P4 = NSA_KV_HEADS * V7X_LANES
    W = 2 * KV_WIDTH
    blk = lambda w_: pl.BlockSpec((tm, w_), row)
    shp = lambda w_, dt: jax.ShapeDtypeStruct((T, w_), dt)
    args = [y, nw, w, cos, sin, onehot]
    in_specs = [blk(D_MODEL), pl.BlockSpec((1, D_MODEL), const),
                pl.BlockSpec((D_MODEL, NSA_IN_WP), const, pipeline_mode=pl.Buffered(1)),
                pl.BlockSpec((tm, V7X_LANES), tab), pl.BlockSpec((tm, V7X_LANES), tab),
                pl.BlockSpec((tm, V7X_LANES), tab)]
    aliases = {}
    if kv_layers is None:
        kv_specs = [blk(W)] * 3
        kv_shapes = [shp(W, f32)] * 3
    else:
        layer, n_layers, prev = kv_layers
        n_seq = T // (tm * tiles_per_seq)
        kv_specs = [pl.BlockSpec((None, None, W, tm), lambda i: (layer, i // tiles_per_seq, 0, i % tiles_per_seq))] * 3
        kv_shapes = [jax.ShapeDtypeStruct((n_layers, n_seq, W, tm * tiles_per_seq), f32)] * 3
        if prev is not None:
            aliases = {len(args) + k: 2 + k for k in range(3)}
            args += list(prev)
            in_specs += [pl.BlockSpec(memory_space=pl.ANY)] * 3
    return pl.pallas_call(
        functools.partial(_nsa_in_kernel, token_minor=kv_layers is not None, n_alias=len(aliases)),
        grid=(T // tm,),
        in_specs=in_specs,
        out_specs=[blk(Q_WIDTH), blk(2 * Q_WIDTH)] + kv_specs + [blk(P4), blk(P4), blk(P4), blk(P4), blk(V7X_LANES)],
        out_shape=[shp(Q_WIDTH, bf16), shp(2 * Q_WIDTH, bf16)] + kv_shapes
        + [shp(P4, bf16), shp(P4, bf16), shp(P4, bf16), shp(P4, bf16), shp(V7X_LANES, f32)],
        scratch_shapes=[pltpu.VMEM((tm, NSA_IN_WP), f32)],
        input_output_aliases=aliases,
        compiler_params=_cparams("parallel"),
        name="nsa_in",
    )(*args)


def _cmp_select_kernel(q_ref, kbd_ref, vbd_ref, oc_ref, bias_ref, *, tq, nb, n_top):
    i = pl.program_id(1)
    G = NSA_GROUP
    gw = G * HEAD_DIM
    blk = lax.broadcasted_iota(jnp.int32, (nb, tq), 0)
    pos = i * tq + lax.broadcasted_iota(jnp.int32, (nb, tq), 1)
    complete = (blk + 1) * NSA_BLOCK - 1 <= pos
    cur = lax.shift_right_logical(pos, int(math.log2(NSA_BLOCK)))
    cand = blk < cur
    for g in range(NSA_KV_HEADS):
        sT = _dot_nt(kbd_ref[g], q_ref[:, g * gw:(g + 1) * gw]) * ATTN_SCALE
        ps = []
        imp = jnp.zeros((nb, tq), f32)
        for n in range(G):
            sm = jnp.where(complete, sT[n * nb:(n + 1) * nb], NEG)
            mx = jnp.max(sm, axis=0, keepdims=True)
            e = jnp.where(complete, jnp.exp(sm - mx), 0.0)
            den = jnp.sum(e, axis=0, keepdims=True)
            p = e / jnp.where(den > 0.0, den, 1.0)
            ps.append(p)
            imp = imp + p
        oc_ref[:, g * gw:(g + 1) * gw] = _dot_tn(jnp.concatenate(ps, axis=0), vbd_ref[g])
        vals = jnp.where(cand, imp, -1.0)
        cnt = jnp.zeros((nb, tq), jnp.int32)
        for j in range(nb):
            vj = vals[j:j + 1, :]
            cnt = cnt + ((vj > vals) | ((vj == vals) & (blk > j))).astype(jnp.int32)
        sel = (cand & (cnt < n_top)) | (blk == cur)
        biasT = jnp.where(sel, 0.0, NEG)
        full = jnp.concatenate([jnp.zeros((HEAD_DIM, tq), f32), biasT,
                                jnp.zeros((V7X_LANES - HEAD_DIM - nb, tq), f32)], axis=0)
        bias_ref[:, g * V7X_LANES:(g + 1) * V7X_LANES] = full.T.astype(bf16)


def _cmp_select(q_raw, kbd, vbd, *, seq_len, tq):
    T = q_raw.shape[0]
    nq = seq_len // tq
    nb = seq_len // NSA_BLOCK
    row = lambda b, i: (b * nq + i, 0)
    kern = functools.partial(_cmp_select_kernel, tq=tq, nb=nb, n_top=min(N_SEL - 1, nb))
    return pl.pallas_call(
        kern,
        grid=(T // seq_len, nq),
        in_specs=[pl.BlockSpec((tq, Q_WIDTH), row),
                  pl.BlockSpec((None,) + kbd.shape[1:], lambda b, i: (b, 0, 0, 0)),
                  pl.BlockSpec((None,) + vbd.shape[1:], lambda b, i: (b, 0, 0, 0))],
        out_specs=[pl.BlockSpec((tq, Q_WIDTH), row), pl.BlockSpec((tq, NSA_KV_HEADS * V7X_LANES), row)],
        out_shape=[jax.ShapeDtypeStruct((T, Q_WIDTH), f32),
                   jax.ShapeDtypeStruct((T, NSA_KV_HEADS * V7X_LANES), bf16)],
        compiler_params=_cparams("parallel", "parallel"),
        name="nsa_cmp_select",
    )(q_raw, kbd, vbd)


def _attn_kernel(*refs, tq, mode):
    if mode == "sel":
        q_ref, bias_ref, k_ref, v_ref, o_ref, mx_ref, acc_ref, s_ref = refs
    else:
        q_ref, k_ref, v_ref, o_ref, mx_ref, acc_ref, s_ref = refs
    i = pl.program_id(2)
    G = NSA_GROUP
    hq = tq // 2
    R2 = G * hq
    qn = [q_ref[:, n * V7X_LANES:(n + 1) * V7X_LANES] for n in range(G)]
    if mode == "sel":
        qn = [q + bias_ref[...] for q in qn]
    qa = jnp.concatenate([q[h * hq:(h + 1) * hq] for h in range(2) for q in qn], axis=0)
    rows = [slice(h * R2, (h + 1) * R2) for h in range(2)]
    q_in = lax.broadcasted_iota(jnp.int32, (R2, hq), 0) % hq
    k_in = lax.broadcasted_iota(jnp.int32, (R2, hq), 1)
    causal = ((0, 0, k_in <= q_in), (1, 0, None), (1, 1, k_in <= q_in))
    window = ((0, 0, k_in > q_in), (0, 1, None), (1, 1, k_in > q_in))
    nback = WINDOW // tq
    back_tiles = () if mode == "sel" else tuple(range(1, nback + 1))

    def slot_of(t):
        return t if mode == "sel" else i - t

    def cols(t, kh):
        return pl.ds(pl.multiple_of(slot_of(t) * tq + kh * hq, hq), hq)

    def keys(ref, t, kh):
        return ref[pl.ds(pl.multiple_of(t * tq + kh * hq, hq), hq), :]

    def first_pass_full(t):
        s = _dot_nt(qa, k_ref[pl.ds(pl.multiple_of(t * tq, tq), tq), :])
        s_ref[:, pl.ds(pl.multiple_of(slot_of(t) * tq, tq), tq)] = s
        mx_ref[...] = jnp.maximum(mx_ref[...], s)

    def first_pass_blocks(t, blocks, first=False):
        for qh, kh, mask in blocks:
            s = _dot_nt(qa[rows[qh]], keys(k_ref, t, kh))
            if mask is not None:
                s = jnp.where(mask, s, NEG)
            s_ref[rows[qh], cols(t, kh)] = s
            here = (rows[qh], slice(kh * hq, (kh + 1) * hq))
            mx_ref[here] = s if first else jnp.maximum(mx_ref[here], s)

    mx_ref[rows[0], hq:tq] = jnp.full((R2, hq), NEG, f32)
    first_pass_blocks(i, causal, first=True)
    if mode == "sel":
        def max_body(t, c):
            first_pass_full(t)
            return c
        lax.fori_loop(0, i, max_body, 0)
    else:
        for back in back_tiles:
            @pl.when(i >= back)
            def _(back=back):
                if back == nback:
                    first_pass_blocks(i - back, window)
                else:
                    first_pass_full(i - back)
    mx_ref[...] = jnp.broadcast_to(jnp.max(mx_ref[...], axis=-1, keepdims=True), mx_ref.shape)

    def accumulate_full(t):
        s = s_ref[:, pl.ds(pl.multiple_of(slot_of(t) * tq, tq), tq)]
        p = jnp.exp2(s - mx_ref[...]).astype(bf16)
        acc_ref[...] += jnp.dot(p, v_ref[pl.ds(pl.multiple_of(t * tq, tq), tq), :], preferred_element_type=f32)

    def accumulate_blocks(t, blocks, first=False):
        seen = set()
        for qh, kh, _ in blocks:
            p = jnp.exp2(s_ref[rows[qh], cols(t, kh)] - mx_ref[rows[qh], 0:hq]).astype(bf16)
            pv = jnp.dot(p, keys(v_ref, t, kh), preferred_element_type=f32)
            if first and qh not in seen:
                acc_ref[rows[qh], :] = pv
            else:
                acc_ref[rows[qh], :] += pv
            seen.add(qh)

    accumulate_blocks(i, causal, first=True)
    if mode == "sel":
        def acc_body(t, c):
            accumulate_full(t)
            return c
        lax.fori_loop(0, i, acc_body, 0)
    else:
        for back in back_tiles:
            @pl.when(i >= back)
            def _(back=back):
                if back == nback:
                    accumulate_blocks(i - back, window)
                else:
                    accumulate_full(i - back)

    acc = acc_ref[...]
    o = acc / pltpu.roll(acc, HEAD_DIM, 1)
    lane = _lane_iota((hq, V7X_LANES))
    for h in range(2):
        for p in range(G // 2):
            a = o[h * R2 + (2 * p) * hq:h * R2 + (2 * p + 1) * hq]
            b = o[h * R2 + (2 * p + 1) * hq:h * R2 + (2 * p + 2) * hq]
            o_ref[h * hq:(h + 1) * hq, p * V7X_LANES:(p + 1) * V7X_LANES] = jnp.where(
                lane < HEAD_DIM, a, pltpu.roll(b, HEAD_DIM, 1))


def _attn(q_pad, bias, k_pad, v_pad, *, seq_len, tq, mode):
    assert mode in ("sel", "win") and WINDOW % tq == 0
    T = q_pad.shape[0]
    B = T // seq_len
    nq = seq_len // tq
    G = NSA_GROUP
    qrow = lambda b, g, i: (b * nq + i, g)
    kv = lambda b, g, i: (b, g)
    in_specs = [pl.BlockSpec((tq, G * V7X_LANES), qrow)]
    args = [q_pad]
    if mode == "sel":
        in_specs.append(pl.BlockSpec((tq, V7X_LANES), qrow))
        args.append(bias)
    in_specs += [pl.BlockSpec((seq_len, V7X_LANES), kv), pl.BlockSpec((seq_len, V7X_LANES), kv)]
    args += [k_pad, v_pad]
    return pl.pallas_call(
        functools.partial(_attn_kernel, tq=tq, mode=mode),
        grid=(B, NSA_KV_HEADS, nq),
        in_specs=in_specs,
        out_specs=pl.BlockSpec((tq, G * HEAD_DIM), qrow),
        out_shape=jax.ShapeDtypeStruct((T, Q_WIDTH), f32),
        scratch_shapes=[pltpu.VMEM((G * tq, tq), f32), pltpu.VMEM((G * tq, V7X_LANES), f32),
                        pltpu.VMEM((G * tq, seq_len if mode == "sel" else WINDOW + tq), f32)],
        compiler_params=_cparams("parallel", "parallel", "parallel"),
        name="nsa_attn_" + mode,
    )(*args)


def _nsa_out_kernel(y_ref, oc_ref, os_ref, ow_ref, gate_ref, e_ref, wout_ref, out_ref):
    g = gate_ref[...]
    ghi = g.astype(bf16)
    glo = (g - ghi.astype(f32)).astype(bf16)
    o = jnp.zeros(oc_ref.shape, f32)
    for j, br in enumerate((oc_ref, os_ref, ow_ref)):
        ge = (jnp.dot(ghi, e_ref[j], preferred_element_type=f32) + jnp.dot(glo, e_ref[j], preferred_element_type=f32))
        o = o + ge * br[...]
    out_ref[...] = y_ref[...] + jnp.dot(o.astype(bf16), wout_ref[...], preferred_element_type=f32)


def _nsa_out(y, oc, osel, ow, gates, expand, wout, *, tm):
    T = y.shape[0]
    row = lambda i: (i, 0)
    wide = pl.BlockSpec((tm, Q_WIDTH), row)
    return pl.pallas_call(
        _nsa_out_kernel,
        grid=(T // tm,),
        in_specs=[pl.BlockSpec((tm, D_MODEL), row), wide, wide, wide, pl.BlockSpec((tm, V7X_LANES), row),
                  pl.BlockSpec((3, V7X_LANES, Q_WIDTH), lambda i: (0, 0, 0)),
                  pl.BlockSpec((Q_WIDTH, D_MODEL), lambda i: (0, 0))],
        out_specs=pl.BlockSpec((tm, D_MODEL), row),
        out_shape=jax.ShapeDtypeStruct((T, D_MODEL), f32),
        compiler_params=_cparams("parallel"),
        name="nsa_out",
    )(y, oc, osel, ow, gates, expand, wout)


DEC_ROWS = V7X_SUBLANES


def _page_gather_kernel(pt_ref, tbl_ref, o_ref, *, n_pages):
    b = pl.program_id(0)
    for p in range(n_pages):
        o_ref[p:p + 1, :] = tbl_ref[pl.ds(pt_ref[b, p], 1), :]


def _page_gather(page_table, tbl):
    DB, n_pages = page_table.shape
    n_phys, W = tbl.shape
    return pl.pallas_call(
        functools.partial(_page_gather_kernel, n_pages=n_pages),
        grid_spec=pltpu.PrefetchScalarGridSpec(
            num_scalar_prefetch=1,
            grid=(DB,),
            in_specs=[pl.BlockSpec((n_phys, W), lambda b, pt: (0, 0), pipeline_mode=pl.Buffered(1))],
            out_specs=pl.BlockSpec((None, n_pages, W), lambda b, pt: (b, 0, 0)),
        ),
        out_shape=jax.ShapeDtypeStruct((DB, n_pages, W), tbl.dtype),
        compiler_params=_cparams("arbitrary"),
        name="nsa_page_gather",
    )(page_table, tbl)


PAGE_BLOCKS = PAGE_SIZE // NSA_BLOCK
PAGE_ROWS = 2 * KV_WIDTH


CMP_PITCH = HEAD_DIM + V7X_SUBLANES


def _compress_pages_kernel(x_hbm, pe_ref, w_ref, o_ref, buf_ref, sem_ref, *, first, steps, M):
    i = pl.program_id(0)

    def copies(step, slot):
        return [pltpu.make_async_copy(x_hbm.at[first + step * M + k], buf_ref.at[slot, pl.ds(k * CMP_PITCH, HEAD_DIM), :],
                                      sem_ref.at[slot]) for k in range(M)]

    @pl.when(i == 0)
    def _():
        for c in copies(0, 0):
            c.start()

    @pl.when(i + 1 < steps)
    def _():
        for c in copies(i + 1, (i + 1) % 2):
            c.start()

    slot = i % 2
    for c in copies(i, slot):
        c.wait()
    buf = buf_ref.at[slot]
    acc = jnp.zeros((M, 2 * V7X_LANES), f32)
    for d in range(HEAD_DIM):
        a = buf[pl.ds(d, M, stride=CMP_PITCH), :].reshape(M // 8, 8, V7X_LANES) + pe_ref[d]
        acc = acc + jnp.dot(a.reshape(M, V7X_LANES).astype(bf16), w_ref[d], preferred_element_type=f32)
    is_k = lax.broadcasted_iota(jnp.int32, (M, V7X_LANES), 0) % (2 * NSA_KV_HEADS) < NSA_KV_HEADS
    o_ref[...] = jnp.where(is_k, acc[:, :V7X_LANES], acc[:, V7X_LANES:])


def _compress_pages(slabs, pe8, w2, *, layer, n_phys, pages):
    assert n_phys % pages == 0
    steps = n_phys // pages
    per_page = PAGE_ROWS // HEAD_DIM
    M = pages * per_page
    return pl.pallas_call(
        functools.partial(_compress_pages_kernel, first=layer * n_phys * per_page, steps=steps, M=M),
        grid=(steps,),
        in_specs=[pl.BlockSpec(memory_space=pl.ANY),
                  pl.BlockSpec(pe8.shape, lambda i: (0, 0, 0)),
                  pl.BlockSpec(w2.shape, lambda i: (0, 0, 0), pipeline_mode=pl.Buffered(1))],
        out_specs=pl.BlockSpec((M, V7X_LANES), lambda i: (i, 0)),
        out_shape=jax.ShapeDtypeStruct((steps * M, V7X_LANES), f32),
        scratch_shapes=[pltpu.VMEM((2, M * CMP_PITCH, PAGE_SIZE), f32), pltpu.SemaphoreType.DMA((2,))],
        compiler_params=_cparams("arbitrary"),
        name="nsa_compress_pages",
    )(slabs, pe8, w2)


def _cmp_decode_kernel(q_ref, ckv_ref, oc_ref, bias_ref, *, past, n_top):
    R = DEC_ROWS
    n_pages = ckv_ref.shape[0]
    nb = PAGE_BLOCKS * n_pages
    KVH, G = NSA_KV_HEADS, NSA_GROUP
    lane = lax.broadcasted_iota(jnp.int32, (R, nb), 1)
    blk = PAGE_BLOCKS * (lane % n_pages) + lane // n_pages
    pos = past + lax.broadcasted_iota(jnp.int32, (R, nb), 0)
    complete = (blk + 1) * NSA_BLOCK - 1 <= pos
    cand = blk < lax.shift_right_logical(pos, int(math.log2(NSA_BLOCK)))

    def blocks(c, g):
        lo = (c * KVH + g) * PAGE_BLOCKS * HEAD_DIM
        return jnp.concatenate([ckv_ref[:, lo + h * HEAD_DIM:lo + (h + 1) * HEAD_DIM] for h in range(PAGE_BLOCKS)],
                               axis=0).astype(bf16)

    ck = [blocks(0, g) for g in range(KVH)]
    cv = [blocks(1, g) for g in range(KVH)]
    heads = [(g, n) for g in range(KVH) for n in range(G)]
    qh = [q_ref[:, (g * G + n) * HEAD_DIM:(g * G + n + 1) * HEAD_DIM] for g, n in heads]
    sm = [jnp.where(complete, _dot_nt(qh[i], ck[g]) * ATTN_SCALE, NEG) for i, (g, n) in enumerate(heads)]
    e = [jnp.where(complete, jnp.exp(s - jnp.max(s, axis=-1, keepdims=True)), 0.0) for s in sm]
    den = [jnp.sum(x, axis=-1, keepdims=True) for x in e]
    p = [x / jnp.where(d > 0.0, d, 1.0) for x, d in zip(e, den)]
    for i, (g, n) in enumerate(heads):
        lo = (g * G + n) * HEAD_DIM
        oc_ref[:, lo:lo + HEAD_DIM] = _dot(p[i], cv[g])
    imp = jnp.concatenate([sum(p[g * G:(g + 1) * G]) for g in range(KVH)], axis=0)
    cand4 = jnp.concatenate([cand] * KVH, axis=0)
    blk4 = jnp.concatenate([blk] * KVH, axis=0)
    vals = jnp.where(cand4, imp, -1.0)
    cnts = [jnp.zeros(vals.shape, jnp.int32) for _ in range(4)]
    for s in range(1, nb):
        other = pltpu.roll(vals, s, 1)
        other_blk = pltpu.roll(blk4, s, 1)
        cnts[s % 4] = cnts[s % 4] + ((other > vals) | ((other == vals) & (other_blk < blk4))).astype(jnp.int32)
    cnt = (cnts[0] + cnts[1]) + (cnts[2] + cnts[3])
    bias = jnp.where(cand4 & (cnt < n_top), 0.0, NEG)
    for g in range(KVH):
        bias_ref[g] = bias[g * R:(g + 1) * R]


def _cmp_decode(q_raw, ckv, *, past):
    DB, n_pages = ckv.shape[:2]
    nb = PAGE_BLOCKS * n_pages
    assert nb == V7X_LANES
    return pl.pallas_call(
        functools.partial(_cmp_decode_kernel, past=past, n_top=min(N_SEL - 1, nb)),
        grid=(DB,),
        in_specs=[pl.BlockSpec((None, DEC_ROWS, Q_WIDTH), lambda b: (b, 0, 0)),
                  pl.BlockSpec((None,) + ckv.shape[1:], lambda b: (b, 0, 0))],
        out_specs=[pl.BlockSpec((None, DEC_ROWS, Q_WIDTH), lambda b: (b, 0, 0)),
                   pl.BlockSpec((None, NSA_KV_HEADS, DEC_ROWS, nb), lambda b: (b, 0, 0, 0))],
        out_shape=[jax.ShapeDtypeStruct((DB, DEC_ROWS, Q_WIDTH), f32),
                   jax.ShapeDtypeStruct((DB, NSA_KV_HEADS, DEC_ROWS, nb), f32)],
        compiler_params=_cparams("parallel"),
        name="nsa_cmp_decode",
    )(q_raw, ckv)


SEL_PAGES_PER_STEP = 32


def _sel_decode_kernel(pt_ref, q_ref, bias_ref, new_ref, pool_ref, o_ref, buf_ref, sem_ref, m_ref, l_ref, acc_ref,
                       *, layer, n_seq, n_pages, n_new):
    PG = SEL_PAGES_PER_STEP
    NG = n_pages // PG
    R = NSA_GROUP * DEC_ROWS
    keys = PG * PAGE_SIZE

    def copies(step, slot):
        b = step // NG
        j = step % NG
        return [pltpu.make_async_copy(pool_ref.at[layer, pt_ref[b, j * PG + p]], buf_ref.at[slot, p],
                                      sem_ref.at[slot]) for p in range(PG)]

    KVH = NSA_KV_HEADS

    def online(s, pv):
        m_old = [m_ref[g] for g in range(KVH)]
        m_new = [jnp.maximum(m_old[g], jnp.max(s[g], axis=-1, keepdims=True)) for g in range(KVH)]
        alpha = [jnp.exp2(m_old[g] - m_new[g]) for g in range(KVH)]
        p = [jnp.exp2(s[g] - m_new[g]) for g in range(KVH)]
        psum = [jnp.sum(p[g], axis=-1, keepdims=True) for g in range(KVH)]
        pvs = [pv[g](p[g]) for g in range(KVH)]
        for g in range(KVH):
            l_ref[g] = alpha[g] * l_ref[g] + psum[g]
            acc_ref[g] = alpha[g] * acc_ref[g] + pvs[g]
            m_ref[g] = m_new[g]

    for c in copies(0, 0):
        c.start()

    def body(step, carry):
        slot = step % 2
        b = step // NG
        j = step % NG

        @pl.when(step + 1 < n_seq * NG)
        def _():
            for c in copies(step + 1, 1 - slot):
                c.start()

        @pl.when(j == 0)
        def _():
            m_ref[...] = jnp.full(m_ref.shape, NEG, f32)
            l_ref[...] = jnp.zeros(l_ref.shape, f32)
            acc_ref[...] = jnp.zeros(acc_ref.shape, f32)
            tok = lax.broadcasted_iota(jnp.int32, (R, DEC_ROWS), 0) % DEC_ROWS
            key = lax.broadcasted_iota(jnp.int32, (R, DEC_ROWS), 1)
            causal = (key <= tok) & (key < n_new)
            qn = [q_ref[b, g][:, :HEAD_DIM] for g in range(KVH)]
            kn = [new_ref[b, :, g * HEAD_DIM:(g + 1) * HEAD_DIM] for g in range(KVH)]
            vn = [new_ref[b, :, KV_WIDTH + g * HEAD_DIM:KV_WIDTH + (g + 1) * HEAD_DIM] for g in range(KVH)]
            online([jnp.where(causal, _dot_nt(qn[g], kn[g]), NEG) for g in range(KVH)],
                   [functools.partial(lambda p, v: _dot(p, v), v=vn[g]) for g in range(KVH)])

        for c in copies(step, slot):
            c.wait()
        lane = lax.broadcasted_iota(jnp.int32, (V7X_LANES, keys), 0)
        key = lax.broadcasted_iota(jnp.int32, (V7X_LANES, keys), 1)
        kpage = j * PG + lax.shift_right_logical(key, int(math.log2(PAGE_SIZE)))
        khalf = lax.shift_right_logical(key, int(math.log2(NSA_BLOCK))) % PAGE_BLOCKS
        expand = (lane == khalf * n_pages + kpage).astype(bf16)
        qg = [q_ref[b, g][:, :HEAD_DIM] for g in range(KVH)]
        kt = [jnp.concatenate([buf_ref[slot, p, 0, g] for p in range(PG)], axis=1).astype(bf16) for g in range(KVH)]
        vt = [jnp.concatenate([buf_ref[slot, p, 1, g] for p in range(PG)], axis=1).astype(bf16) for g in range(KVH)]
        bias = [jnp.concatenate([bias_ref[b, g]] * NSA_GROUP, axis=0).astype(bf16) for g in range(KVH)]
        s = [_dot(qg[g], kt[g]) + jnp.dot(bias[g], expand, preferred_element_type=f32) for g in range(KVH)]
        online(s, [functools.partial(lambda p, v: _dot_nt(p, v), v=vt[g]) for g in range(KVH)])

        @pl.when(j == NG - 1)
        def _():
            o = [acc_ref[g] / l_ref[g] for g in range(KVH)]
            for g in range(KVH):
                for n in range(NSA_GROUP):
                    lo = (g * NSA_GROUP + n) * HEAD_DIM
                    o_ref[b, :, lo:lo + HEAD_DIM] = o[g][n * DEC_ROWS:(n + 1) * DEC_ROWS]

        return carry

    lax.fori_loop(0, n_seq * NG, body, 0)


def _sel_decode(page_table, q, bias, new_kv, pool, *, layer, n_new):
    DB, n_pages = page_table.shape
    assert n_pages % SEL_PAGES_PER_STEP == 0 and n_pages * PAGE_BLOCKS == V7X_LANES
    R = NSA_GROUP * DEC_ROWS
    whole = lambda shape: pl.BlockSpec(shape, lambda i, pt, _n=len(shape): (0,) * _n)
    return pl.pallas_call(
        functools.partial(_sel_decode_kernel, layer=layer, n_seq=DB, n_pages=n_pages, n_new=n_new),
        grid_spec=pltpu.PrefetchScalarGridSpec(
            num_scalar_prefetch=1,
            grid=(1,),
            in_specs=[whole(q.shape), whole(bias.shape), whole(new_kv.shape), pl.BlockSpec(memory_space=pl.ANY)],
            out_specs=whole((DB, DEC_ROWS, Q_WIDTH)),
            scratch_shapes=[pltpu.VMEM((2, SEL_PAGES_PER_STEP) + pool.shape[2:], f32),
                            pltpu.SemaphoreType.DMA((2,)),
                            pltpu.VMEM((NSA_KV_HEADS, R, 1), f32), pltpu.VMEM((NSA_KV_HEADS, R, 1), f32),
                            pltpu.VMEM((NSA_KV_HEADS, R, HEAD_DIM), f32)],
        ),
        out_shape=jax.ShapeDtypeStruct((DB, DEC_ROWS, Q_WIDTH), f32),
        compiler_params=_cparams("arbitrary"),
        name="nsa_sel_decode",
    )(page_table, q, bias, new_kv, pool)


def _win_decode_kernel(q_ref, buf_ref, new_ref, o_ref, *, n_new):
    wbuf = buf_ref.shape[-1]
    R = NSA_GROUP * DEC_ROWS
    KVH = NSA_KV_HEADS
    tok = lax.broadcasted_iota(jnp.int32, (R, wbuf), 0) % DEC_ROWS
    dist = wbuf + tok - lax.broadcasted_iota(jnp.int32, (R, wbuf), 1)
    in_buf = (dist >= 0) & (dist < WINDOW)
    tokn = lax.broadcasted_iota(jnp.int32, (R, DEC_ROWS), 0) % DEC_ROWS
    keyn = lax.broadcasted_iota(jnp.int32, (R, DEC_ROWS), 1)
    in_new = (keyn <= tokn) & (keyn < n_new)
    hs = range(KVH)
    q = [q_ref[g][:, :HEAD_DIM] for g in hs]
    kn = [new_ref[:, g * HEAD_DIM:(g + 1) * HEAD_DIM] for g in hs]
    vn = [new_ref[:, KV_WIDTH + g * HEAD_DIM:KV_WIDTH + (g + 1) * HEAD_DIM] for g in hs]
    s1 = [jnp.where(in_buf, _dot(q[g], buf_ref[0, g]), NEG) for g in hs]
    s2 = [jnp.where(in_new, _dot_nt(q[g], kn[g]), NEG) for g in hs]
    m = [jnp.maximum(jnp.max(s1[g], axis=-1, keepdims=True), jnp.max(s2[g], axis=-1, keepdims=True)) for g in hs]
    p1 = [jnp.exp2(s1[g] - m[g]) for g in hs]
    p2 = [jnp.exp2(s2[g] - m[g]) for g in hs]
    den = [jnp.sum(p1[g], axis=-1, keepdims=True) + jnp.sum(p2[g], axis=-1, keepdims=True) for g in hs]
    o = [(_dot_nt(p1[g], buf_ref[1, g]) + _dot(p2[g], vn[g])) / den[g] for g in hs]
    for g in hs:
        for n in range(NSA_GROUP):
            lo = (g * NSA_GROUP + n) * HEAD_DIM
            o_ref[:, lo:lo + HEAD_DIM] = o[g][n * DEC_ROWS:(n + 1) * DEC_ROWS]


def _win_decode(q, win_t, new_kv, *, layer, n_new):
    DB = q.shape[0]
    R = NSA_GROUP * DEC_ROWS
    return pl.pallas_call(
        functools.partial(_win_decode_kernel, n_new=n_new),
        grid=(DB,),
        in_specs=[pl.BlockSpec((None, NSA_KV_HEADS, R, V7X_LANES), lambda b: (b, 0, 0, 0)),
                  pl.BlockSpec((None, None) + win_t.shape[2:], lambda b: (layer, b, 0, 0, 0, 0)),
                  pl.BlockSpec((None, DEC_ROWS, 2 * KV_WIDTH), lambda b: (b, 0, 0))],
        out_specs=pl.BlockSpec((None, DEC_ROWS, Q_WIDTH), lambda b: (b, 0, 0)),
        out_shape=jax.ShapeDtypeStruct((DB, DEC_ROWS, Q_WIDTH), f32),
        compiler_params=_cparams("parallel"),
        name="nsa_win_decode",
    )(q, win_t, new_kv)


def _rope_tables(pos):
    half = HEAD_DIM // 2
    inv = ROPE_THETA ** (-jnp.arange(half, dtype=f32) / half)
    ang = pos.astype(f32)[:, None] * inv[None, :]
    cos, sin = jnp.cos(ang), jnp.sin(ang)
    reps = V7X_LANES // HEAD_DIM
    return jnp.tile(cos, (1, 2 * reps)), jnp.tile(jnp.concatenate([-sin, sin], axis=1), (1, reps))


def _block_onehot(pos):
    return (jnp.arange(V7X_LANES, dtype=jnp.int32)[None, :] == HEAD_DIM + pos[:, None] // NSA_BLOCK).astype(f32)


def _gate_expand():
    src = jnp.arange(V7X_LANES)[None, :, None]
    head = jnp.arange(Q_WIDTH)[None, None, :] // HEAD_DIM
    j = jnp.arange(3)[:, None, None]
    return (src == head * 3 + j).astype(bf16)


def _compress_page_weights(pe, w):
    assert PAGE_BLOCKS == 2
    wt = jnp.transpose(w, (2, 0, 1, 3)).astype(bf16)
    z = jnp.zeros_like(wt)
    w2 = jnp.concatenate([jnp.concatenate([wt, z], axis=-1), jnp.concatenate([z, wt], axis=-1)], axis=1)
    w2 = w2.reshape(HEAD_DIM, PAGE_SIZE, 2 * PAGE_BLOCKS * HEAD_DIM)
    pe8 = jnp.broadcast_to(pe.transpose(2, 1, 0)[:, :, None, None, :],
                           (HEAD_DIM, 2, NSA_KV_HEADS, PAGE_BLOCKS, NSA_BLOCK))
    return pe8.reshape(HEAD_DIM, 2 * NSA_KV_HEADS, PAGE_SIZE), w2


def _token_minor(cache):
    n = cache.ndim
    return jnp.transpose(cache, tuple(range(n - 4)) + (n - 3, n - 2, n - 1, n - 4))


def _time_major(x):
    return jnp.swapaxes(x, 0, 1).reshape((x.shape[0] * x.shape[1],) + x.shape[2:])


def _batch_major(x, db):
    return jnp.swapaxes(x.reshape((x.shape[0] // db, db) + x.shape[1:]), 0, 1)


def _gdn_weights(w_in, a_log, dt_bias):
    W3 = 3 * GDN_WIDTH
    pad = V7X_LANES - 2 * GDN_HEADS
    return (w_in[:, :W3].astype(bf16), w_in[:, W3:W3 + GDN_WIDTH].astype(bf16),
            jnp.pad(w_in[:, W3 + GDN_WIDTH:], ((0, 0), (0, pad))).astype(bf16),
            jnp.pad(a_log, (0, V7X_LANES - GDN_HEADS))[None], jnp.pad(dt_bias, (0, V7X_LANES - GDN_HEADS))[None])


def _gdn_layer(yp, ys, S0s, conv_s, nw, w_in, conv_w, a_log, dt_bias, norm_w, w_out, *, B, L, DB, DS):
    wqkv, wz, wab, alog, dtb = _gdn_weights(w_in, a_log, dt_bias)
    wout = w_out.astype(bf16)
    nw = nw[None]
    norm_w = norm_w[None]
    q, k, v, z, slab, conv_p = _gdn_in(yp, None, nw, wqkv, wz, wab, conv_w, alog, dtb, tm=512, shift=1, rows_per_seq=L)
    o, S_p = _gdn_chunk(q, k, v, slab, jnp.zeros((B, GDN_HEADS, GDN_DK, GDN_DK), f32), seq_len=L)
    yp = _gdn_out(yp, o, z, norm_w, wout, tm=512)
    Ts = DS * DB
    hist = _time_major(conv_s).reshape((GDN_CONV - 1) * DB, 3 * GDN_WIDTH)
    q, k, v, z, slab, conv_tm = _gdn_in(ys, hist, nw, wqkv, wz, wab, conv_w, alog, dtb, tm=Ts, shift=DB, rows_per_seq=Ts)

    def chunked(x):
        x = _batch_major(x, DB)
        return jnp.pad(x, ((0, 0), (0, GDN_CHUNK - DS), (0, 0))).reshape(DB * GDN_CHUNK, x.shape[-1])

    o, S_s = _gdn_chunk(chunked(q), chunked(k), chunked(v), chunked(slab), S0s, seq_len=GDN_CHUNK)
    o = _time_major(o.reshape(DB, GDN_CHUNK, GDN_WIDTH)[:, :DS])
    ys = _gdn_out(ys, o, z, norm_w, wout, tm=Ts)
    conv_s_new = _batch_major(conv_tm.reshape((GDN_CONV - 1) * DB, 3 * GDN_WIDTH), DB)
    return yp, ys, S_p, S_s, conv_p, conv_s_new


def _ffn_layer(yp, ys, conv_s, nw, w_up, conv_w, conv_b, w_down, final_nw, *, B, L, DB, DS):
    wup = w_up.astype(bf16)
    wdn = w_down.astype(bf16)
    nw = nw[None]
    cb = conv_b[None]
    yp, st_p = _ffn(yp, None, nw, wup, conv_w, cb, wdn, tm=512, shift=1, rows_per_seq=L, final_nw=final_nw)
    Ts = DS * DB
    hist = _time_major(conv_s).reshape((FFN_CONV - 1) * DB, 2 * D_FF)
    ys, st_tm = _ffn(ys, hist, nw, wup, conv_w, cb, wdn, tm=Ts, shift=DB, rows_per_seq=Ts, final_nw=final_nw)
    st_s = _batch_major(st_tm.reshape((FFN_CONV - 1) * DB, 2 * D_FF), DB)
    return yp, ys, st_p, st_s


def _nsa_layer(yp, ys, j, kv_prev, cmp_t, sel_t, win_t, page_table, nw, w_in, cmp_pe, cmp_w, w_out,
               *, B, L, DB, DS):
    past = page_table.shape[1] * PAGE_SIZE
    n_nsa = cmp_t.shape[0]
    w = jnp.pad(w_in, ((0, 0), (0, NSA_IN_WP - NSA_IN_W))).astype(bf16)
    wout = w_out.astype(bf16)
    nw = nw[None]
    expand = _gate_expand()
    pe8, w2 = _compress_page_weights(cmp_pe, cmp_w)
    tq = WINDOW // 2
    tq_attn = WINDOW
    pos_p = jnp.arange(L, dtype=jnp.int32)
    cos, sin = _rope_tables(pos_p)
    if kv_prev is None:
        kv_prev = tuple(jnp.zeros((n_nsa, B, 2 * KV_WIDTH, L), f32) for _ in range(3))
    (q_raw, q_rot, cmp_l, sel_l, win_l, ksel, vsel, kwin, vwin, gates) = _nsa_in(
        yp, nw, w, cos, sin, _block_onehot(pos_p), tm=256, tiles_per_seq=L // 256, kv_layers=(j, n_nsa, kv_prev))
    kv_layers = (cmp_l, sel_l, win_l)
    nb = L // NSA_BLOCK
    npg = B * L // PAGE_SIZE
    per_page = PAGE_ROWS // HEAD_DIM
    slabs = cmp_l[j].reshape(B, per_page, HEAD_DIM, L // PAGE_SIZE, PAGE_SIZE).transpose(0, 3, 1, 2, 4)
    ckv = _compress_pages(slabs.reshape(npg * per_page, HEAD_DIM, PAGE_SIZE), pe8, w2, layer=0, n_phys=npg,
                          pages=_pick_tile(npg, 16))
    ck = ckv.reshape(B, L // PAGE_SIZE, 2, NSA_KV_HEADS, PAGE_BLOCKS, HEAD_DIM)
    ck = ck.transpose(0, 1, 4, 2, 3, 5).reshape(B, nb, 2, NSA_KV_HEADS, HEAD_DIM)
    eye = jnp.eye(NSA_GROUP, dtype=f32)
    kbd = jnp.einsum("bkgd,nm->bgnkmd", ck[:, :, 0], eye).reshape(B, NSA_KV_HEADS, NSA_GROUP * nb, NSA_GROUP * HEAD_DIM)
    vbd = jnp.einsum("bkgd,nm->bgnkmd", ck[:, :, 1], eye).reshape(B, NSA_KV_HEADS, NSA_GROUP * nb, NSA_GROUP * HEAD_DIM)
    o_c, bias = _cmp_select(q_raw, kbd.astype(bf16), vbd.astype(bf16), seq_len=L, tq=tq)
    o_s = _attn(q_rot, bias, ksel, vsel, seq_len=L, tq=tq_attn, mode="sel")
    o_w = _attn(q_rot, None, kwin, vwin, seq_len=L, tq=tq_attn, mode="win")
    yp = _nsa_out(yp, o_c, o_s, o_w, gates, expand, wout, tm=512)
    Ts = DS * DB
    pos_s = past + jnp.arange(DS, dtype=jnp.int32)
    pos_rows = jnp.repeat(pos_s, DB)
    cos, sin = _rope_tables(pos_rows)
    (q_raw, q_rot, kv_cmp, kv_sel, kv_win, _, _, _, _, gates) = _nsa_in(
        ys, nw, w, cos, sin, jnp.zeros((Ts, V7X_LANES), f32), tm=Ts, tiles_per_seq=1)
    n_layers, n_phys = cmp_t.shape[:2]
    ckv_all = _compress_pages(cmp_t.reshape(n_layers * n_phys * PAGE_ROWS // HEAD_DIM, HEAD_DIM, PAGE_SIZE), pe8, w2,
                              layer=j, n_phys=n_phys, pages=_pick_tile(n_phys, 16))
    ckv = _page_gather(page_table, ckv_all.reshape(n_phys, 2 * KV_WIDTH * PAGE_BLOCKS))

    def rows8(x):
        return jnp.pad(_batch_major(x, DB), ((0, 0), (0, DEC_ROWS - DS), (0, 0)))

    o_c, bias = _cmp_decode(rows8(q_raw), ckv, past=past)
    q4 = rows8(q_rot).reshape(DB, DEC_ROWS, NSA_KV_HEADS, NSA_GROUP, V7X_LANES)
    q4 = q4.transpose(0, 2, 3, 1, 4).reshape(DB, NSA_KV_HEADS, NSA_GROUP * DEC_ROWS, V7X_LANES)
    o_s = _sel_decode(page_table, q4, bias, rows8(kv_sel), sel_t, layer=j, n_new=DS)
    o_w = _win_decode(q4, win_t, rows8(kv_win), layer=j, n_new=DS)
    ks = (DB, DS, 2, NSA_KV_HEADS, HEAD_DIM)
    cmp_s, sel_s, win_new = (_batch_major(t, DB).reshape(ks) for t in (kv_cmp, kv_sel, kv_win))
    tm_rows = lambda o: _time_major(o[:, :DS])
    ys = _nsa_out(ys, tm_rows(o_c), tm_rows(o_s), tm_rows(o_w), gates, expand, wout, tm=Ts)
    return yp, ys, kv_layers, cmp_s, sel_s, win_new


def kernel(x_prompt, x_sample, state_gdn, state_gdn_conv, cache_cmp, cache_sel, state_win, state_ffn_conv, page_table,
           norm_mix, norm_ffn, norm_final, gdn_w_in, gdn_conv_w, gdn_a_log, gdn_dt_bias, gdn_norm_w, gdn_w_out,
           nsa_w_in, nsa_cmp_pe, nsa_cmp_w, nsa_w_out, ffn_w_up, ffn_conv_w, ffn_conv_b, ffn_w_down):
    B, L, _ = x_prompt.shape
    DB, DS, _ = x_sample.shape
    depth = norm_mix.shape[0]
    dims = dict(B=B, L=L, DB=DB, DS=DS)
    yp = x_prompt.reshape(B * L, D_MODEL)
    ys = _time_major(x_sample)
    outs = [[] for _ in range(12)]
    kv_p = None
    cmp_t, sel_t, win_t = _token_minor(cache_cmp), _token_minor(cache_sel), _token_minor(state_win)
    for i in range(depth):
        j = i // 2
        if i % 2 == 0:
            yp, ys, S_p, S_s, c_p, c_s = _gdn_layer(yp, ys, state_gdn[j], state_gdn_conv[j], norm_mix[i], gdn_w_in[j],
                                                    gdn_conv_w[j], gdn_a_log[j], gdn_dt_bias[j], gdn_norm_w[j],
                                                    gdn_w_out[j], **dims)
            for lst, val in zip(outs[0:4], (S_p, S_s, c_p, c_s)):
                lst.append(val)
        else:
            yp, ys, kv_p, *kv_s = _nsa_layer(yp, ys, j, kv_p, cmp_t, sel_t, win_t, page_table,
                                             norm_mix[i], nsa_w_in[j], nsa_cmp_pe[j], nsa_cmp_w[j], nsa_w_out[j], **dims)
            for lst, val in zip((outs[5], outs[7], outs[9]), kv_s):
                lst.append(val)
        yp, ys, f_p, f_s = _ffn_layer(yp, ys, state_ffn_conv[i], norm_ffn[i], ffn_w_up[i], ffn_conv_w[i],
                                      ffn_conv_b[i], ffn_w_down[i], norm_final[None] if i == depth - 1 else None,
                                      **dims)
        outs[10].append(f_p)
        outs[11].append(f_s)
    y_prompt = yp.reshape(B, L, D_MODEL)
    y_sample = _batch_major(ys, DB)
    rows_of = lambda t: jnp.transpose(t.reshape(t.shape[:2] + (2, NSA_KV_HEADS, HEAD_DIM, t.shape[-1])), (0, 1, 5, 2, 3, 4))
    cmp_p, sel_p, win_p = rows_of(kv_p[0]), rows_of(kv_p[1]), rows_of(kv_p[2][..., L - min(WINDOW, L):])
    stacked = [jnp.stack(o) if o else None for o in outs]
    stacked[4], stacked[6], stacked[8] = cmp_p, sel_p, win_p
    stacked[9] = jnp.concatenate([state_win[:, :, DS:], stacked[9]], axis=2)
    return (y_prompt, y_sample) + tuple(stacked)
```
